```python
import math
import jax
import jax.numpy as jnp
from jax import lax
import numpy as np


D_MODEL = 2048
BATCH = 4
SEQ = 2048
DEPTH = 2

GRID_W = 64
CTX_LEN = 256
EPS = 1e-6
ROPE_BASE = 10000.0
NEG_INF = -1e30
BLOCK = 128
HEAD_DIM = 128
S5_WIDTH = D_MODEL // 4
S5_GROUP = 16
S5_GROUPS = S5_WIDTH // S5_GROUP
S5_STATE = 64
SWA_HEADS = (3 * D_MODEL // 8) // HEAD_DIM
SWA_KV_HEADS = 2
WINDOW = 128
MLA_HEADS = (3 * D_MODEL // 8) // 128
MLA_Q_RANK = 768
MLA_KV_RANK = 512
MLA_NOPE = 128
MLA_ROPE = 64
MLA_V = 128
MIX_WIDTH = S5_WIDTH + SWA_HEADS * HEAD_DIM + MLA_HEADS * MLA_V
D_FF = 4 * D_MODEL
IN_WIDTHS = (S5_WIDTH, SWA_HEADS * HEAD_DIM, SWA_KV_HEADS * HEAD_DIM, SWA_KV_HEADS * HEAD_DIM,
             MLA_Q_RANK, MLA_KV_RANK, MLA_ROPE)
IN_WIDTH = sum(IN_WIDTHS)

kernel_name = 'hybrid_s5_swa_mla_prefix_dit_block'


def rms_norm(x, g):
    xf = x.astype(jnp.float32)
    y = xf * lax.rsqrt(jnp.mean(xf * xf, axis=-1, keepdims=True) + EPS)
    return (y * g.astype(jnp.float32)).astype(x.dtype)


def modulate(h, shift, scale):
    return h * (1.0 + scale) + shift


def split_in(z):
    out, off = [], 0
    for w in IN_WIDTHS:
        out.append(z[..., off:off + w])
        off += w
    return out


def axial_angles(row, col, rot_dim):
    quarter = rot_dim // 4
    inv_freq = ROPE_BASE ** (-jnp.arange(quarter, dtype=jnp.float32) / quarter)
    return (row.astype(jnp.float32)[:, None] * inv_freq, col.astype(jnp.float32)[:, None] * inv_freq)


def _rotate(x, ang):
    x1, x2 = jnp.split(x.astype(jnp.float32), 2, axis=-1)
    cos = jnp.cos(ang)[:, None, :]
    sin = jnp.sin(ang)[:, None, :]
    return jnp.concatenate([x1 * cos - x2 * sin, x2 * cos + x1 * sin], axis=-1)


def rope_2d(x, ang_row, ang_col):
    xr, xc = jnp.split(x, 2, axis=-1)
    return jnp.concatenate([_rotate(xr, ang_row), _rotate(xc, ang_col)], axis=-1).astype(x.dtype)


def s5_discretize(a_re, a_im, log_dt, b_re, b_im):
    dt = jnp.exp(log_dt)[:, None]
    mag = jnp.exp(a_re * dt)
    lr = mag * jnp.cos(a_im * dt)
    li = mag * jnp.sin(a_im * dt)
    den = a_re * a_re + a_im * a_im
    nr = lr - 1.0
    fr = (nr * a_re + li * a_im) / den
    fi = (li * a_re - nr * a_im) / den
    bb_r = fr[..., None] * b_re - fi[..., None] * b_im
    bb_i = fr[..., None] * b_im + fi[..., None] * b_re
    return lr, li, bb_r, bb_i


def _complex_affine_combine(e1, e2):
    a1r, a1i, b1r, b1i = e1
    a2r, a2i, b2r, b2i = e2
    return (a2r * a1r - a2i * a1i, a2r * a1i + a2i * a1r,
            a2r * b1r - a2i * b1i + b2r, a2r * b1i + a2i * b1r + b2i)


def s5_states(u, lam_r, lam_i, bb_r, bb_i, h0):
    b_r = jnp.einsum('btgp,gnp->btgn', u, bb_r)
    b_i = jnp.einsum('btgp,gnp->btgn', u, bb_i)
    if h0 is not None:
        h0_r, h0_i = h0
        b_r = b_r.at[:, 0].add(lam_r * h0_r - lam_i * h0_i)
        b_i = b_i.at[:, 0].add(lam_r * h0_i + lam_i * h0_r)
    a_r = jnp.broadcast_to(lam_r, b_r.shape)
    a_i = jnp.broadcast_to(lam_i, b_i.shape)
    _, _, h_r, h_i = lax.associative_scan(_complex_affine_combine, (a_r, a_i, b_r, b_i), axis=1)
    return h_r, h_i


def s5_readout(h_r, h_i, c_r, c_i):
    return jnp.einsum('btgn,gpn->btgp', h_r, c_r) - jnp.einsum('btgn,gpn->btgp', h_i, c_i)


def _order(t, reverse):
    return jnp.flip(t, axis=1) if reverse else t


def s5_mixer(u_c, u_l, a_re, a_im, log_dt, b_re, b_im, c_re, c_im, d_skip, glu_w, glu_b, need_ctx):
    f32 = jnp.float32
    uc = u_c.astype(f32).reshape(u_c.shape[0], u_c.shape[1], S5_GROUPS, S5_GROUP)
    ul = u_l.astype(f32).reshape(u_l.shape[0], u_l.shape[1], S5_GROUPS, S5_GROUP)
    dk = d_skip.astype(f32)
    y_l = dk * u_l.astype(f32)
    y_c = dk * u_c.astype(f32)
    for direction in range(2):
        rev = direction == 1
        lam_r, lam_i, bb_r, bb_i = s5_discretize(a_re[direction].astype(f32), a_im[direction].astype(f32),
                                                 log_dt[direction].astype(f32), b_re[direction].astype(f32),
                                                 b_im[direction].astype(f32))
        cr = c_re[direction].astype(f32)
        ci = c_im[direction].astype(f32)
        hc_r, hc_i = s5_states(_order(uc, rev), lam_r, lam_i, bb_r, bb_i, None)
        hl_r, hl_i = s5_states(_order(ul, rev), lam_r, lam_i, bb_r, bb_i, (hc_r[:, -1], hc_i[:, -1]))
        y_l = y_l + _order(s5_readout(hl_r, hl_i, cr, ci), rev).reshape(y_l.shape)
        if need_ctx:
            y_c = y_c + _order(s5_readout(hc_r, hc_i, cr, ci), rev).reshape(y_c.shape)

    def glu(y):
        g = jax.nn.gelu(y)
        return (g * jax.nn.sigmoid(g @ glu_w.astype(f32) + glu_b.astype(f32))).astype(u_l.dtype)

    return (glu(y_c) if need_ctx else None), glu(y_l)


def swa_latent(q, k, v, kc, vc, sink):
    f32 = jnp.float32
    bsz, n, n_heads, dh = q.shape
    n_kv = k.shape[2]
    rep = n_heads // n_kv
    nb = n // BLOCK
    n_ctx = kc.shape[1]
    scale = dh ** -0.5
    qb = q.reshape(bsz, nb, BLOCK, n_kv, rep, dh)

    def band(t):
        tp = jnp.pad(t, ((0, 0), (BLOCK, BLOCK), (0, 0), (0, 0))).reshape(bsz, nb + 2, BLOCK, n_kv, dh)
        return jnp.concatenate([tp[:, :-2], tp[:, 1:-1], tp[:, 2:]], axis=2)

    kb, vb = band(k), band(v)
    s_ctx = jnp.einsum('bnqgrd,bcgd->bgrnqc', qb, kc).astype(f32) * scale
    s_band = jnp.einsum('bnqgrd,bnkgd->bgrnqk', qb, kb).astype(f32) * scale
    blk = jnp.arange(nb)[:, None, None] * BLOCK
    qpos = blk + jnp.arange(BLOCK)[None, :, None]
    kpos = blk - BLOCK + jnp.arange(3 * BLOCK)[None, None, :]
    valid = (jnp.abs(qpos - kpos) <= WINDOW) & (kpos >= 0) & (kpos < n)
    s_band = jnp.where(valid, s_band, NEG_INF)
    s_sink = jnp.broadcast_to(sink.astype(f32).reshape(1, n_kv, rep, 1, 1, 1), s_band.shape[:-1] + (1,))
    p = jax.nn.softmax(jnp.concatenate([s_ctx, s_band, s_sink], axis=-1), axis=-1).astype(v.dtype)
    o = (jnp.einsum('bgrnqc,bcgd->bnqgrd', p[..., :n_ctx], vc)
         + jnp.einsum('bgrnqk,bnkgd->bnqgrd', p[..., n_ctx:n_ctx + 3 * BLOCK], vb))
    return o.reshape(bsz, n, n_heads * dh)


def swa_context(qc, kc, vc, sink):
    f32 = jnp.float32
    bsz, n_ctx, n_heads, dh = qc.shape
    n_kv = kc.shape[2]
    rep = n_heads // n_kv
    qg = qc.reshape(bsz, n_ctx, n_kv, rep, dh)
    s = jnp.einsum('bqgrd,bkgd->bgrqk', qg, kc).astype(f32) * (dh ** -0.5)
    s_sink = jnp.broadcast_to(sink.astype(f32).reshape(1, n_kv, rep, 1, 1), s.shape[:-1] + (1,))
    p = jax.nn.softmax(jnp.concatenate([s, s_sink], axis=-1), axis=-1)[..., :n_ctx].astype(vc.dtype)
    return jnp.einsum('bgrqk,bkgd->bqgrd', p, vc).reshape(bsz, n_ctx, n_heads * dh)


def mla_queries(z_cq, q_norm, w_uq, ang):
    bsz, t, _ = z_cq.shape
    q = (rms_norm(z_cq, q_norm) @ w_uq).reshape(bsz, t, MLA_HEADS, MLA_NOPE + MLA_ROPE)
    q_nope, q_rope = q[..., :MLA_NOPE], q[..., MLA_NOPE:]
    if ang is not None:
        q_rope = rope_2d(q_rope, ang[0], ang[1])
    return q_nope, q_rope


def mla_keys(z_ckv, k_rope, kv_norm, w_ukv, ang):
    bsz, t, _ = z_ckv.shape
    kv = (rms_norm(z_ckv, kv_norm) @ w_ukv).reshape(bsz, t, MLA_HEADS, MLA_NOPE + MLA_V)
    k_nope, v = kv[..., :MLA_NOPE], kv[..., MLA_NOPE:]
    if ang is not None:
        k_rope = rope_2d(k_rope[:, :, None, :], ang[0], ang[1])[:, :, 0, :]
    return k_nope, k_rope, v


def mla_attend(q_nope, q_rope, k_nope, k_rope, v):
    bsz, n_q, n_heads, _ = q_nope.shape
    nb = n_q // BLOCK
    scale = (MLA_NOPE + MLA_ROPE) ** -0.5

    def to_blocks(t):
        return jnp.moveaxis(t.reshape((bsz, nb, BLOCK) + t.shape[2:]), 1, 0)

    def one_block(args):
        qn, qr = args
        s = (jnp.einsum('bqhd,bkhd->bhqk', qn, k_nope)
             + jnp.einsum('bqhd,bkd->bhqk', qr, k_rope)).astype(jnp.float32) * scale
        p = jax.nn.softmax(s, axis=-1).astype(v.dtype)
        return jnp.einsum('bhqk,bkhd->bqhd', p, v)

    o = lax.map(one_block, (to_blocks(q_nope), to_blocks(q_rope)))
    return jnp.moveaxis(o, 0, 1).reshape(bsz, n_q, n_heads * MLA_V)


def hybrid_mixer(h_c, h_l, w_in, w_out, s5_p, swa_sink, mla_p, ang_swa, ang_mla, need_ctx):
    bsz, n, _ = h_l.shape
    n_ctx = h_c.shape[1]
    u_l, qs_l, ks_l, vs_l, cq_l, ckv_l, kr_l = split_in(h_l @ w_in)
    u_c, qs_c, ks_c, vs_c, cq_c, ckv_c, kr_c = split_in(h_c @ w_in)
    q_norm, w_uq, kv_norm, w_ukv = mla_p

    s5_c, s5_l = s5_mixer(u_c, u_l, *s5_p, need_ctx)

    q_l = rope_2d(qs_l.reshape(bsz, n, SWA_HEADS, HEAD_DIM), ang_swa[0], ang_swa[1])
    k_l = rope_2d(ks_l.reshape(bsz, n, SWA_KV_HEADS, HEAD_DIM), ang_swa[0], ang_swa[1])
    v_l = vs_l.reshape(bsz, n, SWA_KV_HEADS, HEAD_DIM)
    k_c = ks_c.reshape(bsz, n_ctx, SWA_KV_HEADS, HEAD_DIM)
    v_c = vs_c.reshape(bsz, n_ctx, SWA_KV_HEADS, HEAD_DIM)
    swa_l = swa_latent(q_l, k_l, v_l, k_c, v_c, swa_sink)

    qn_l, qr_l = mla_queries(cq_l, q_norm, w_uq, ang_mla)
    kn_l, kro_l, vm_l = mla_keys(ckv_l, kr_l, kv_norm, w_ukv, ang_mla)
    kn_c, kro_c, vm_c = mla_keys(ckv_c, kr_c, kv_norm, w_ukv, None)
    mla_l = mla_attend(qn_l, qr_l, jnp.concatenate([kn_c, kn_l], axis=1),
                       jnp.concatenate([kro_c, kro_l], axis=1), jnp.concatenate([vm_c, vm_l], axis=1))

    o_l = jnp.concatenate([s5_l, swa_l, mla_l], axis=-1) @ w_out
    if not need_ctx:
        return None, o_l
    swa_c = swa_context(qs_c.reshape(bsz, n_ctx, SWA_HEADS, HEAD_DIM), k_c, v_c, swa_sink)
    qn_c, qr_c = mla_queries(cq_c, q_norm, w_uq, None)
    mla_c = mla_attend(qn_c, qr_c, kn_c, kro_c, vm_c)
    o_c = jnp.concatenate([s5_c, swa_c, mla_c], axis=-1) @ w_out
    return o_c, o_l


def sqrelu_mlp(h, w1, w2):
    return jnp.square(jax.nn.relu(h @ w1)) @ w2


def setup_inputs(seed: int = 0) -> dict:
    key = jax.random.key(seed)
    ks = jax.random.split(key, 32)
    f32 = jnp.float32

    def nrm(k, shape, scale):
        return jax.random.normal(k, shape, f32) * scale

    G, N, P = S5_GROUPS, S5_STATE, S5_GROUP
    return {
        'x': nrm(ks[0], (BATCH, SEQ, D_MODEL), 1.0),
        'c': nrm(ks[1], (BATCH, D_MODEL), 1.0),
        'ctx': nrm(ks[2], (BATCH, CTX_LEN, D_MODEL), 1.0),
        'c_ctx': nrm(ks[3], (D_MODEL,), 1.0),
        'ada_w': nrm(ks[4], (DEPTH, D_MODEL, 6 * D_MODEL), D_MODEL ** -0.5),
        'ada_b': nrm(ks[5], (DEPTH, 6 * D_MODEL), 0.02),
        'norm_mix_pre': 1.0 + nrm(ks[6], (DEPTH, D_MODEL), 0.05),
        'norm_mix_post': 1.0 + nrm(ks[7], (DEPTH, D_MODEL), 0.05),
        'norm_ffn_pre': 1.0 + nrm(ks[8], (DEPTH, D_MODEL), 0.05),
        'norm_ffn_post': 1.0 + nrm(ks[9], (DEPTH, D_MODEL), 0.05),
        'w_in': nrm(ks[10], (DEPTH, D_MODEL, IN_WIDTH), D_MODEL ** -0.5),
        'w_out': nrm(ks[11], (DEPTH, MIX_WIDTH, D_MODEL), MIX_WIDTH ** -0.5),
        's5_a_re': -0.5 + nrm(ks[12], (DEPTH, 2, G, N), 0.01),
        's5_a_im': math.pi * jnp.arange(N, dtype=f32) + nrm(ks[13], (DEPTH, 2, G, N), 0.01),
        's5_log_dt': jax.random.uniform(ks[14], (DEPTH, 2, G), f32, minval=math.log(1e-3), maxval=math.log(1e-1)),
        's5_b_re': nrm(ks[15], (DEPTH, 2, G, N, P), (2.0 * P) ** -0.5),
        's5_b_im': nrm(ks[16], (DEPTH, 2, G, N, P), (2.0 * P) ** -0.5),
        's5_c_re': nrm(ks[17], (DEPTH, 2, G, P, N), (2.0 * N) ** -0.5),
        's5_c_im': nrm(ks[18], (DEPTH, 2, G, P, N), (2.0 * N) ** -0.5),
        's5_d': nrm(ks[19], (DEPTH, S5_WIDTH), 0.5),
        's5_glu_w': nrm(ks[20], (DEPTH, S5_WIDTH, S5_WIDTH), S5_WIDTH ** -0.5),
        's5_glu_b': nrm(ks[21], (DEPTH, S5_WIDTH), 0.02),
        'swa_sink': nrm(ks[22], (DEPTH, SWA_HEADS), 0.5),
        'mla_q_norm': 1.0 + nrm(ks[23], (DEPTH, MLA_Q_RANK), 0.05),
        'mla_w_uq': nrm(ks[24], (DEPTH, MLA_Q_RANK, MLA_HEADS * (MLA_NOPE + MLA_ROPE)), MLA_Q_RANK ** -0.5),
        'mla_kv_norm': 1.0 + nrm(ks[25], (DEPTH, MLA_KV_RANK), 0.05),
        'mla_w_ukv': nrm(ks[26], (DEPTH, MLA_KV_RANK, MLA_HEADS * (MLA_NOPE + MLA_V)), MLA_KV_RANK ** -0.5),
        'ffn_w1': nrm(ks[27], (DEPTH, D_MODEL, D_FF), D_MODEL ** -0.5),
        'ffn_w2': nrm(ks[28], (DEPTH, D_FF, D_MODEL), D_FF ** -0.5),
    }


def reference(x, c, ctx, c_ctx, ada_w, ada_b, norm_mix_pre, norm_mix_post, norm_ffn_pre, norm_ffn_post,
              w_in, w_out, s5_a_re, s5_a_im, s5_log_dt, s5_b_re, s5_b_im, s5_c_re, s5_c_im, s5_d,
              s5_glu_w, s5_glu_b, swa_sink, mla_q_norm, mla_w_uq, mla_kv_norm, mla_w_ukv, ffn_w1, ffn_w2):
    n_lat = x.shape[1]
    rows = n_lat // GRID_W
    row = jnp.repeat(jnp.arange(rows), GRID_W)
    col = jnp.tile(jnp.arange(GRID_W), rows)
    ang_swa = axial_angles(row, col, HEAD_DIM)
    ang_mla = axial_angles(row, col, MLA_ROPE)
    for i in range(DEPTH):
        need_ctx = i < DEPTH - 1
        mod_l = jnp.split((jax.nn.silu(c) @ ada_w[i] + ada_b[i])[:, None, :], 6, axis=-1)
        mod_c = jnp.split(jax.nn.silu(c_ctx) @ ada_w[i] + ada_b[i], 6, axis=-1)
        h_l = modulate(rms_norm(x, norm_mix_pre[i]), mod_l[0], mod_l[1])
        h_c = modulate(rms_norm(ctx, norm_mix_pre[i]), mod_c[0], mod_c[1])
        o_c, o_l = hybrid_mixer(
            h_c, h_l, w_in[i], w_out[i],
            (s5_a_re[i], s5_a_im[i], s5_log_dt[i], s5_b_re[i], s5_b_im[i], s5_c_re[i], s5_c_im[i],
             s5_d[i], s5_glu_w[i], s5_glu_b[i]),
            swa_sink[i],
            (mla_q_norm[i], mla_w_uq[i], mla_kv_norm[i], mla_w_ukv[i]),
            ang_swa, ang_mla, need_ctx)
        x = x + mod_l[2] * rms_norm(o_l, norm_mix_post[i])
        f_l = modulate(rms_norm(x, norm_ffn_pre[i]), mod_l[3], mod_l[4])
        x = x + mod_l[5] * rms_norm(sqrelu_mlp(f_l, ffn_w1[i], ffn_w2[i]), norm_ffn_post[i])
        if need_ctx:
            ctx = ctx + mod_c[2] * rms_norm(o_c, norm_mix_post[i])
            f_c = modulate(rms_norm(ctx, norm_ffn_pre[i]), mod_c[3], mod_c[4])
            ctx = ctx + mod_c[5] * rms_norm(sqrelu_mlp(f_c, ffn_w1[i], ffn_w2[i]), norm_ffn_post[i])
    return x
```

```python
import functools
import math

import jax
import jax.numpy as jnp
from jax import lax
from jax.experimental import pallas as pl
from jax.experimental.pallas import tpu as pltpu

F32 = jnp.float32
BF16 = jnp.bfloat16

D_MODEL = 2048
BATCH = 4
SEQ = 2048
DEPTH = 2
GRID_W = 64
CTX_LEN = 256
EPS = 1e-6
ROPE_BASE = 10000.0
NEG_INF = -1e30
BLOCK = 128
HEAD_DIM = 128
S5_WIDTH = 512
S5_GROUP = 16
S5_GROUPS = 32
S5_STATE = 64
S5_LANES = S5_GROUPS * S5_STATE
SWA_HEADS = 6
SWA_KV_HEADS = 2
SWA_REP = SWA_HEADS // SWA_KV_HEADS
MLA_HEADS = 6
MLA_Q_RANK = 768
MLA_KV_RANK = 512
MLA_NOPE = 128
MLA_ROPE = 64
MLA_V = 128
MLA_QK_PAD = 256
D_FF = 4 * D_MODEL
IN_WIDTHS = (512, 768, 256, 256, 768, 512, 64)
IN_WIDTH = sum(IN_WIDTHS)
IN_PAD = 3584
OFF_U, OFF_QS, OFF_KS, OFF_VS, OFF_CQ, OFF_CKV, OFF_KR = 0, 512, 1280, 1536, 1792, 2560, 3072

VMEM_LIMIT = 56 * 1024 * 1024


def _cparams(*sem):
    return pltpu.CompilerParams(dimension_semantics=sem, vmem_limit_bytes=VMEM_LIMIT)


def _dot(a, b):
    return jnp.dot(a, b, preferred_element_type=F32)


def _dot_nt(a, b):
    return lax.dot_general(a, b, (((1,), (1,)), ((), ())), preferred_element_type=F32)


def _rms(x, g):
    return x * lax.rsqrt(jnp.mean(x * x, axis=-1, keepdims=True) + EPS) * g


def _rope(x, cos, sin, half):
    lane = lax.broadcasted_iota(jnp.int32, x.shape, 1)
    fwd = pltpu.roll(x, 128 - half, 1)
    bwd = pltpu.roll(x, half, 1)
    sw = jnp.where((lane % (2 * half)) < half, fwd, bwd)
    return x * cos + sw * sin


def _ada_kernel(c_ref, w_ref, b_ref, o_ref):
    c = c_ref[...]
    s = c * jax.nn.sigmoid(c)
    o_ref[0] = _dot(s.astype(BF16), w_ref[0].astype(BF16)) + b_ref[0]


def _ada(cvec, ada_w, ada_b):
    tn = 1024
    n = 6 * D_MODEL
    return pl.pallas_call(
        _ada_kernel,
        grid=(DEPTH, n // tn),
        in_specs=[pl.BlockSpec((8, D_MODEL), lambda l, j: (0, 0)),
                  pl.BlockSpec((1, D_MODEL, tn), lambda l, j: (l, 0, j)),
                  pl.BlockSpec((1, 1, tn), lambda l, j: (l, 0, j))],
        out_specs=pl.BlockSpec((1, 8, tn), lambda l, j: (l, 0, j)),
        out_shape=jax.ShapeDtypeStruct((DEPTH, 8, n), F32),
        compiler_params=_cparams("arbitrary", "arbitrary"),
        name="ada",
    )(cvec, ada_w, ada_b.reshape(DEPTH, 1, n))


def _in_proj_kernel(x_ref, shift_ref, scale_ref, g_ref, w_ref, o_ref, h_ref):
    @pl.when(pl.program_id(1) == 0)
    def _():
        h = _rms(x_ref[...], g_ref[...]) * (1.0 + scale_ref[0]) + shift_ref[0]
        h_ref[...] = h.astype(BF16)

    o_ref[...] = _dot(h_ref[...], w_ref[...].astype(BF16))


def _in_proj(x2d, shift, scale, g, w_pad, tm, mod_row):
    rows = x2d.shape[0]
    tn = 512
    return pl.pallas_call(
        _in_proj_kernel,
        grid=(rows // tm, IN_PAD // tn),
        in_specs=[pl.BlockSpec((tm, D_MODEL), lambda i, j: (i, 0)),
                  pl.BlockSpec((1, 1, D_MODEL), lambda i, j: (mod_row(i), 0, 0)),
                  pl.BlockSpec((1, 1, D_MODEL), lambda i, j: (mod_row(i), 0, 0)),
                  pl.BlockSpec((1, D_MODEL), lambda i, j: (0, 0)),
                  pl.BlockSpec((D_MODEL, tn), lambda i, j: (0, j))],
        out_specs=pl.BlockSpec((tm, tn), lambda i, j: (i, j)),
        out_shape=jax.ShapeDtypeStruct((rows, IN_PAD), F32),
        scratch_shapes=[pltpu.VMEM((tm, D_MODEL), BF16)],
        compiler_params=_cparams("arbitrary", "arbitrary"),
        name="in_proj",
    )(x2d, shift, scale, g, w_pad)


def _s5_disc_kernel(are_ref, aim_ref, ldt_ref, ebr_ref, ebi_ref, cr_ref, ci_ref,
                    lam_ref, wbr_ref, wbi_ref, wcr_ref, wci_ref):
    are = are_ref[0]
    aim = aim_ref[0]
    dt = jnp.exp(ldt_ref[0])
    mag = jnp.exp(are * dt)
    lr = mag * jnp.cos(aim * dt)
    li = mag * jnp.sin(aim * dt)
    den = are * are + aim * aim
    nr = lr - 1.0
    fr = (nr * are + li * aim) / den
    fi = (li * are - nr * aim) / den
    lam_ref[0, 0:1, :] = lr
    lam_ref[0, 1:2, :] = li
    half = S5_LANES // 2
    for h in range(2):
        frh = fr[:, h * half:(h + 1) * half]
        fih = fi[:, h * half:(h + 1) * half]
        wbr_ref[0, h] = (frh * ebr_ref[0, h] - fih * ebi_ref[0, h]).astype(BF16)
        wbi_ref[0, h] = (frh * ebi_ref[0, h] + fih * ebr_ref[0, h]).astype(BF16)
        wcr_ref[0, h] = cr_ref[0, h].astype(BF16)
        wci_ref[0, h] = ci_ref[0, h].astype(BF16)


def _s5_disc(a_re, a_im, log_dt, b_re, b_im, c_re, c_im):
    gh = S5_GROUPS // 2
    eye = jnp.eye(gh, dtype=F32)
    are = a_re.reshape(2, 1, S5_LANES)
    aim = a_im.reshape(2, 1, S5_LANES)
    ldt = jnp.repeat(log_dt, S5_STATE, axis=-1).reshape(2, 1, S5_LANES)

    def expand_b(b):
        b5 = b.reshape(2, 2, gh, S5_STATE, S5_GROUP)
        return jnp.einsum('dhgnp,gk->dhgpkn', b5, eye).reshape(2, 2, gh * S5_GROUP, gh * S5_STATE)

    def expand_c(c):
        c5 = c.reshape(2, 2, gh, S5_GROUP, S5_STATE)
        return jnp.einsum('dhgpn,gk->dhgnkp', c5, eye).reshape(2, 2, gh * S5_STATE, gh * S5_GROUP)

    kin, kst = gh * S5_GROUP, gh * S5_STATE
    vec = pl.BlockSpec((1, 1, S5_LANES), lambda d: (d, 0, 0))
    bspec = pl.BlockSpec((1, 2, kin, kst), lambda d: (d, 0, 0, 0))
    cspec = pl.BlockSpec((1, 2, kst, kin), lambda d: (d, 0, 0, 0))
    return pl.pallas_call(
        _s5_disc_kernel,
        grid=(2,),
        in_specs=[vec, vec, vec, bspec, bspec, cspec, cspec],
        out_specs=[pl.BlockSpec((1, 2, S5_LANES), lambda d: (d, 0, 0)), bspec, bspec, cspec, cspec],
        out_shape=[jax.ShapeDtypeStruct((2, 2, S5_LANES), F32),
                   jax.ShapeDtypeStruct((2, 2, kin, kst), BF16),
                   jax.ShapeDtypeStruct((2, 2, kin, kst), BF16),
                   jax.ShapeDtypeStruct((2, 2, kst, kin), BF16),
                   jax.ShapeDtypeStruct((2, 2, kst, kin), BF16)],
        compiler_params=_cparams("arbitrary"),
        name="s5_disc",
    )(are, aim, ldt, expand_b(b_re), expand_b(b_im), expand_c(c_re), expand_c(c_im))


def _s5_scan_kernel(u_ref, lam_ref, wbr_ref, wbi_ref, wcr_ref, wci_ref, y_ref,
                    br_ref, bi_ref, st_ref, *, tc, skip_chunks):
    c = pl.program_id(1)
    kin, kst = S5_WIDTH // 2, S5_LANES // 2

    @pl.when(c == 0)
    def _():
        st_ref[...] = jnp.zeros_like(st_ref)

    u = u_ref[0].astype(BF16)
    for h in range(2):
        uh = u[:, h * kin:(h + 1) * kin]
        br_ref[:, h * kst:(h + 1) * kst] = _dot(uh, wbr_ref[0, h])
        bi_ref[:, h * kst:(h + 1) * kst] = _dot(uh, wbi_ref[0, h])

    lw = 1024
    for lb in range(S5_LANES // lw):
        ls = slice(lb * lw, (lb + 1) * lw)
        lr = lam_ref[0, 0:1, ls]
        li = lam_ref[0, 1:2, ls]

        def body(k, carry, ls=ls, lr=lr, li=li):
            hr, hi = carry
            rows = pl.ds(pl.multiple_of(k * 8, 8), 8)
            tr = br_ref[rows, ls]
            ti = bi_ref[rows, ls]
            out_r, out_i = [], []
            for off in (0, 4):
                nhr = lr * hr - li * hi + tr[off:off + 4]
                nhi = lr * hi + li * hr + ti[off:off + 4]
                out_r.append(nhr)
                out_i.append(nhi)
                hr, hi = nhr, nhi
            br_ref[rows, ls] = jnp.concatenate(out_r, axis=0)
            bi_ref[rows, ls] = jnp.concatenate(out_i, axis=0)
            return hr, hi

        hr, hi = lax.fori_loop(0, tc // 2, body, (st_ref[0:4, ls], st_ref[4:8, ls]))
        st_ref[0:4, ls] = hr
        st_ref[4:8, ls] = hi

    @pl.when(c >= skip_chunks)
    def _():
        for h in range(2):
            hs = slice(h * kst, (h + 1) * kst)
            y_ref[0, :, h * kin:(h + 1) * kin] = (
                _dot(br_ref[:, hs].astype(BF16), wcr_ref[0, h])
                - _dot(bi_ref[:, hs].astype(BF16), wci_ref[0, h]))


def _s5_scan(useq, lam, wbr, wbi, wcr, wci, skip_chunks):
    tc = 128
    rows = useq.shape[1]
    nchunk = rows // (tc * BATCH)
    kin, kst = S5_WIDTH // 2, S5_LANES // 2
    bspec = pl.BlockSpec((1, 2, kin, kst), lambda d, c: (d, 0, 0, 0))
    cspec = pl.BlockSpec((1, 2, kst, kin), lambda d, c: (d, 0, 0, 0))
    return pl.pallas_call(
        functools.partial(_s5_scan_kernel, tc=tc, skip_chunks=skip_chunks),
        grid=(2, nchunk),
        in_specs=[pl.BlockSpec((1, tc * BATCH, S5_WIDTH), lambda d, c: (d, c, 0)),
                  pl.BlockSpec((1, 2, S5_LANES), lambda d, c: (d, 0, 0)),
                  bspec, bspec, cspec, cspec],
        out_specs=pl.BlockSpec((1, tc * BATCH, S5_WIDTH),
                               lambda d, c: (d, jnp.maximum(c - skip_chunks, 0), 0)),
        out_shape=jax.ShapeDtypeStruct((2, rows - skip_chunks * tc * BATCH, S5_WIDTH), F32),
        scratch_shapes=[pltpu.VMEM((tc * BATCH, S5_LANES), F32),
                        pltpu.VMEM((tc * BATCH, S5_LANES), F32),
                        pltpu.VMEM((8, S5_LANES), F32)],
        compiler_params=_cparams("arbitrary", "arbitrary"),
        name="s5_scan",
    )(useq, lam, wbr, wbi, wcr, wci)


def _s5_glu_kernel(u_ref, yf_ref, yb_ref, d_ref, w_ref, b_ref, o_ref):
    y = d_ref[...] * u_ref[...] + yf_ref[...] + yb_ref[...]
    g = jax.nn.gelu(y)
    gate = jax.nn.sigmoid(_dot(g.astype(BF16), w_ref[...].astype(BF16)) + b_ref[...])
    o_ref[...] = (g * gate).astype(o_ref.dtype)


def _s5_glu(z, yf, yb, d, w, b, tm):
    rows = z.shape[0]
    row = pl.BlockSpec((tm, S5_WIDTH), lambda i: (i, 0))
    vec = pl.BlockSpec((1, S5_WIDTH), lambda i: (0, 0))
    return pl.pallas_call(
        _s5_glu_kernel,
        grid=(rows // tm,),
        in_specs=[row, row, row, vec, pl.BlockSpec((S5_WIDTH, S5_WIDTH), lambda i: (0, 0)), vec],
        out_specs=row,
        out_shape=jax.ShapeDtypeStruct((rows, S5_WIDTH), BF16),
        compiler_params=_cparams("arbitrary"),
        name="s5_glu",
    )(z, yf, yb, d.reshape(1, -1), w, b.reshape(1, -1))


def _s5_mixer(z_c, z_l, s5_w, d, glu_w, glu_b, need_ctx):
    lam, wbr, wbi, wcr, wci = s5_w
    u_c = z_c[:, :S5_WIDTH].reshape(BATCH, CTX_LEN, S5_WIDTH)
    u_l = z_l[:, :S5_WIDTH].reshape(BATCH, SEQ, S5_WIDTH)
    fwd = jnp.concatenate([u_c, u_l], axis=1)
    bwd = jnp.concatenate([jnp.flip(u_c, 1), jnp.flip(u_l, 1)], axis=1)
    t = CTX_LEN + SEQ
    useq = jnp.stack([fwd, bwd]).transpose(0, 2, 1, 3).reshape(2, t * BATCH, S5_WIDTH)
    skip = 0 if need_ctx else CTX_LEN // 128
    y = _s5_scan(useq, lam, wbr, wbi, wcr, wci, skip)
    nctx = CTX_LEN if need_ctx else 0
    y = y.reshape(2, nctx + SEQ, BATCH, S5_WIDTH).transpose(0, 2, 1, 3)
    yf_l = y[0, :, nctx:].reshape(BATCH * SEQ, S5_WIDTH)
    yb_l = jnp.flip(y[1, :, nctx:], 1).reshape(BATCH * SEQ, S5_WIDTH)
    s5_l = _s5_glu(z_l, yf_l, yb_l, d, glu_w, glu_b, 1024)
    s5_c = None
    if need_ctx:
        yf_c = y[0, :, :CTX_LEN].reshape(BATCH * CTX_LEN, S5_WIDTH)
        yb_c = jnp.flip(y[1, :, :CTX_LEN], 1).reshape(BATCH * CTX_LEN, S5_WIDTH)
        s5_c = _s5_glu(z_c, yf_c, yb_c, d, glu_w, glu_b, 1024)
    return s5_c, s5_l


def _swa_rope_kernel(za_ref, zb_ref, cos_ref, sin_ref, o_ref):
    cos = cos_ref[...]
    sin = sin_ref[...]
    scale = HEAD_DIM ** -0.5
    per = (SWA_HEADS + SWA_KV_HEADS) // 2
    for h in range(SWA_HEADS + SWA_KV_HEADS):
        src = za_ref if h < per else zb_ref
        lo = (h % per) * HEAD_DIM
        x = _rope(src[:, lo:lo + HEAD_DIM], cos, sin, HEAD_DIM // 4)
        if h < SWA_HEADS:
            x = x * scale
        o_ref[:, h * HEAD_DIM:(h + 1) * HEAD_DIM] = x.astype(BF16)


def _swa_rope(z_l, cos, sin):
    tm = 512
    w = (SWA_HEADS + SWA_KV_HEADS) * HEAD_DIM
    c0 = OFF_QS // (w // 2)
    tab = pl.BlockSpec((tm, HEAD_DIM), lambda i: (i % (SEQ // tm), 0))
    return pl.pallas_call(
        _swa_rope_kernel,
        grid=(BATCH * SEQ // tm,),
        in_specs=[pl.BlockSpec((tm, w // 2), lambda i: (i, c0)),
                  pl.BlockSpec((tm, w // 2), lambda i: (i, c0 + 1)),
                  tab, tab],
        out_specs=pl.BlockSpec((tm, w), lambda i: (i, 0)),
        out_shape=jax.ShapeDtypeStruct((BATCH * SEQ, w), BF16),
        compiler_params=_cparams("arbitrary"),
        name="swa_rope",
    )(z_l, z_l, cos, sin)


def _sink_rows(sink_ref, g, rows):
    r = lax.broadcasted_iota(jnp.int32, (rows, 1), 0) // (rows // SWA_REP)
    s0, s1, s2 = sink_ref[SWA_REP * g], sink_ref[SWA_REP * g + 1], sink_ref[SWA_REP * g + 2]
    return jnp.where(r == 0, s0, jnp.where(r == 1, s1, s2))


def _stack_heads(x, n):
    return jnp.concatenate([x[:, h * HEAD_DIM:(h + 1) * HEAD_DIM] for h in range(n)], axis=0)


def _unstack_heads(x, n):
    rows = x.shape[0] // n
    return jnp.concatenate([x[h * rows:(h + 1) * rows] for h in range(n)], axis=1)


def _swa_latent_kernel(sink_ref, q_ref, kp_ref, kc_ref, kn_ref, vp_ref, vc_ref, vn_ref,
                       kx_ref, vx_ref, o_ref):
    g = pl.program_id(1)
    n = pl.program_id(2)
    q = _stack_heads(q_ref[...], SWA_REP)
    kb = jnp.concatenate([kp_ref[...], kc_ref[...], kn_ref[...]], axis=0)
    vb = jnp.concatenate([vp_ref[...], vc_ref[...], vn_ref[...]], axis=0).astype(BF16)
    kx = (kx_ref[...]).astype(BF16)
    vx = vx_ref[...].astype(BF16)
    s_ctx = _dot_nt(q, kx)
    s_band = _dot_nt(q, kb)
    rows = SWA_REP * BLOCK
    qi = lax.broadcasted_iota(jnp.int32, (rows, 3 * BLOCK), 0) % BLOCK
    kj = lax.broadcasted_iota(jnp.int32, (rows, 3 * BLOCK), 1)
    kpos = (n - 1) * BLOCK + kj
    valid = (kj >= qi) & (kj <= qi + 2 * BLOCK) & (kpos >= 0) & (kpos < SEQ)
    s_band = jnp.where(valid, s_band, NEG_INF)
    sk = _sink_rows(sink_ref, g, rows)
    m = jnp.maximum(jnp.maximum(jnp.max(s_ctx, axis=-1, keepdims=True),
                                jnp.max(s_band, axis=-1, keepdims=True)), sk)
    p_ctx = jnp.exp(s_ctx - m)
    p_band = jnp.exp(s_band - m)
    den = (jnp.sum(p_ctx, axis=-1, keepdims=True) + jnp.sum(p_band, axis=-1, keepdims=True)
           + jnp.exp(sk - m))
    o = (_dot(p_ctx.astype(BF16), vx) + _dot(p_band.astype(BF16), vb)) / den
    o_ref[...] = _unstack_heads(o, SWA_REP).astype(o_ref.dtype)


def _swa_latent(qk, z_l, z_c, sink):
    nb = SEQ // BLOCK
    qw = SWA_REP * HEAD_DIM
    kcol = SWA_HEADS
    vcol = OFF_VS // HEAD_DIM
    kxcol = OFF_KS // HEAD_DIM

    def band(col, shift):
        return pl.BlockSpec(
            (BLOCK, HEAD_DIM),
            lambda b, g, n, s: (b * nb + jnp.clip(n + shift, 0, nb - 1), col + g))

    grid_spec = pltpu.PrefetchScalarGridSpec(
        num_scalar_prefetch=1,
        grid=(BATCH, SWA_KV_HEADS, nb),
        in_specs=[pl.BlockSpec((BLOCK, qw), lambda b, g, n, s: (b * nb + n, g)),
                  band(kcol, -1), band(kcol, 0), band(kcol, 1),
                  band(vcol, -1), band(vcol, 0), band(vcol, 1),
                  pl.BlockSpec((CTX_LEN, HEAD_DIM), lambda b, g, n, s: (b, kxcol + g)),
                  pl.BlockSpec((CTX_LEN, HEAD_DIM), lambda b, g, n, s: (b, vcol + g))],
        out_specs=pl.BlockSpec((BLOCK, qw), lambda b, g, n, s: (b * nb + n, g)),
    )
    return pl.pallas_call(
        _swa_latent_kernel,
        grid_spec=grid_spec,
        out_shape=jax.ShapeDtypeStruct((BATCH * SEQ, SWA_HEADS * HEAD_DIM), BF16),
        compiler_params=_cparams("arbitrary", "arbitrary", "arbitrary"),
        name="swa_latent",
    )(sink, qk, qk, qk, qk, z_l, z_l, z_l, z_c, z_c)


def _swa_context_kernel(sink_ref, q_ref, k_ref, v_ref, o_ref):
    g = pl.program_id(1)
    q = (_stack_heads(q_ref[...], SWA_REP) * (HEAD_DIM ** -0.5)).astype(BF16)
    s = _dot_nt(q, k_ref[...].astype(BF16))
    sk = _sink_rows(sink_ref, g, SWA_REP * CTX_LEN)
    m = jnp.maximum(jnp.max(s, axis=-1, keepdims=True), sk)
    p = jnp.exp(s - m)
    den = jnp.sum(p, axis=-1, keepdims=True) + jnp.exp(sk - m)
    o = _dot(p.astype(BF16), v_ref[...].astype(BF16)) / den
    o_ref[...] = _unstack_heads(o, SWA_REP).astype(o_ref.dtype)


def _swa_context(z_c, sink):
    qw = SWA_REP * HEAD_DIM
    grid_spec = pltpu.PrefetchScalarGridSpec(
        num_scalar_prefetch=1,
        grid=(BATCH, SWA_KV_HEADS),
        in_specs=[pl.BlockSpec((CTX_LEN, qw), lambda b, g, s: (b, g)),
                  pl.BlockSpec((CTX_LEN, HEAD_DIM), lambda b, g, s: (b, OFF_KS // HEAD_DIM + g)),
                  pl.BlockSpec((CTX_LEN, HEAD_DIM), lambda b, g, s: (b, OFF_VS // HEAD_DIM + g))],
        out_specs=pl.BlockSpec((CTX_LEN, qw), lambda b, g, s: (b, g)),
    )
    return pl.pallas_call(
        _swa_context_kernel,
        grid_spec=grid_spec,
        out_shape=jax.ShapeDtypeStruct((BATCH * CTX_LEN, SWA_HEADS * HEAD_DIM), BF16),
        compiler_params=_cparams("arbitrary", "arbitrary"),
        name="swa_context",
    )(sink, z_c[:, OFF_QS:OFF_QS + SWA_HEADS * HEAD_DIM], z_c, z_c)


def _mla_q_kernel(c0_ref, c1_ref, c2_ref, g_ref, w_ref, cos_ref, sin_ref, o_ref, *, rope):
    cq = jnp.concatenate([c0_ref[...], c1_ref[...], c2_ref[...]], axis=1)
    q = _dot(_rms(cq, g_ref[...]).astype(BF16), w_ref[...].astype(BF16))
    scale = (MLA_NOPE + MLA_ROPE) ** -0.5
    for h in range(MLA_HEADS):
        lo = h * MLA_QK_PAD
        o_ref[:, lo:lo + MLA_NOPE] = (q[:, lo:lo + MLA_NOPE] * scale).astype(BF16)
        r = q[:, lo + MLA_NOPE:lo + MLA_QK_PAD]
        if rope:
            r = _rope(r, cos_ref[...], sin_ref[...], MLA_ROPE // 4)
        o_ref[:, lo + MLA_NOPE:lo + MLA_QK_PAD] = (r * scale).astype(BF16)


def _mla_q(z, g, w_pad, cos, sin, tm, rope, seq):
    rows = z.shape[0]
    c0 = OFF_CQ // 256
    tab = pl.BlockSpec((tm, 128), lambda i: (i % (seq // tm), 0))
    return pl.pallas_call(
        functools.partial(_mla_q_kernel, rope=rope),
        grid=(rows // tm,),
        in_specs=[pl.BlockSpec((tm, 256), lambda i: (i, c0)),
                  pl.BlockSpec((tm, 256), lambda i: (i, c0 + 1)),
                  pl.BlockSpec((tm, 256), lambda i: (i, c0 + 2)),
                  pl.BlockSpec((1, MLA_Q_RANK), lambda i: (0, 0)),
                  pl.BlockSpec((MLA_Q_RANK, MLA_HEADS * MLA_QK_PAD), lambda i: (0, 0)),
                  tab, tab],
        out_specs=pl.BlockSpec((tm, MLA_HEADS * MLA_QK_PAD), lambda i: (i, 0)),
        out_shape=jax.ShapeDtypeStruct((rows, MLA_HEADS * MLA_QK_PAD), BF16),
        compiler_params=_cparams("arbitrary"),
        name="mla_q",
    )(z, z, z, g.reshape(1, -1), w_pad, cos, sin)


def _mla_kv_kernel(ckv_ref, kr_ref, g_ref, w_ref, cos_ref, sin_ref, k_ref, v_ref, *, rope):
    kv = _dot(_rms(ckv_ref[...], g_ref[...]).astype(BF16), w_ref[...].astype(BF16))
    kr = kr_ref[...]
    if rope:
        kr = _rope(kr, cos_ref[...], sin_ref[...], MLA_ROPE // 4)
    kr = kr.astype(BF16)
    for h in range(MLA_HEADS):
        lo = h * MLA_QK_PAD
        k_ref[:, lo:lo + MLA_NOPE] = kv[:, h * MLA_NOPE:(h + 1) * MLA_NOPE].astype(BF16)
        k_ref[:, lo + MLA_NOPE:lo + MLA_QK_PAD] = kr
    v_ref[...] = kv[:, MLA_HEADS * MLA_NOPE:].astype(BF16)


def _mla_kv(z, g, w_perm, cos, sin, tm, rope, seq):
    rows = z.shape[0]
    tab = pl.BlockSpec((tm, 128), lambda i: (i % (seq // tm), 0))
    return pl.pallas_call(
        functools.partial(_mla_kv_kernel, rope=rope),
        grid=(rows // tm,),
        in_specs=[pl.BlockSpec((tm, MLA_KV_RANK), lambda i: (i, OFF_CKV // MLA_KV_RANK)),
                  pl.BlockSpec((tm, 128), lambda i: (i, OFF_KR // 128)),
                  pl.BlockSpec((1, MLA_KV_RANK), lambda i: (0, 0)),
                  pl.BlockSpec((MLA_KV_RANK, MLA_HEADS * (MLA_NOPE + MLA_V)), lambda i: (0, 0)),
                  tab, tab],
        out_specs=[pl.BlockSpec((tm, MLA_HEADS * MLA_QK_PAD), lambda i: (i, 0)),
                   pl.BlockSpec((tm, MLA_HEADS * MLA_V), lambda i: (i, 0))],
        out_shape=[jax.ShapeDtypeStruct((rows, MLA_HEADS * MLA_QK_PAD), BF16),
                   jax.ShapeDtypeStruct((rows, MLA_HEADS * MLA_V), BF16)],
        compiler_params=_cparams("arbitrary"),
        name="mla_kv",
    )(z, z, g.reshape(1, -1), w_perm, cos, sin)


def _mla_attn_kernel(*refs, nseg):
    q_ref = refs[0]
    k_refs = refs[1:1 + nseg]
    v_refs = refs[1 + nseg:1 + 2 * nseg]
    o_ref = refs[1 + 2 * nseg]
    for h in range(MLA_HEADS):
        q = q_ref[:, h * MLA_QK_PAD:(h + 1) * MLA_QK_PAD]
        s = [_dot_nt(q, k[:, h * MLA_QK_PAD:(h + 1) * MLA_QK_PAD]) for k in k_refs]
        m = functools.reduce(jnp.maximum, [jnp.max(x, axis=-1, keepdims=True) for x in s])
        p = [jnp.exp(x - m) for x in s]
        den = sum(jnp.sum(x, axis=-1, keepdims=True) for x in p)
        o = sum(_dot(x.astype(BF16), v[:, h * MLA_V:(h + 1) * MLA_V]) for x, v in zip(p, v_refs))
        o_ref[:, h * MLA_V:(h + 1) * MLA_V] = (o / den).astype(o_ref.dtype)


def _mla_attn(q, ks, vs, n_q, tq):
    nseg = len(ks)
    nq = n_q // tq
    lens = [k.shape[0] // BATCH for k in ks]
    kw, vw = MLA_HEADS * MLA_QK_PAD, MLA_HEADS * MLA_V
    in_specs = [pl.BlockSpec((tq, kw), lambda b, i: (b * nq + i, 0))]
    in_specs += [pl.BlockSpec((n, kw), lambda b, i: (b, 0)) for n in lens]
    in_specs += [pl.BlockSpec((n, vw), lambda b, i: (b, 0)) for n in lens]
    return pl.pallas_call(
        functools.partial(_mla_attn_kernel, nseg=nseg),
        grid=(BATCH, nq),
        in_specs=in_specs,
        out_specs=pl.BlockSpec((tq, vw), lambda b, i: (b * nq + i, 0)),
        out_shape=jax.ShapeDtypeStruct((BATCH * n_q, vw), BF16),
        compiler_params=_cparams("arbitrary", "arbitrary"),
        name="mla_attn",
    )(q, *ks, *vs)


def _accumulate(o_ref, a, w_ref, first):
    nc = 512
    for n in range(D_MODEL // nc):
        cs = slice(n * nc, (n + 1) * nc)
        part = _dot(a, w_ref[:, cs].astype(BF16))

        @pl.when(first)
        def _(part=part, cs=cs):
            o_ref[:, cs] = part

        @pl.when(jnp.logical_not(first))
        def _(part=part, cs=cs):
            o_ref[:, cs] += part


def _out_proj_kernel(a_ref, w_ref, x_ref, g_ref, gate_ref, o_ref):
    k = pl.program_id(1)
    _accumulate(o_ref, a_ref[...], w_ref, k == 0)

    @pl.when(k == pl.num_programs(1) - 1)
    def _():
        o_ref[...] = x_ref[...] + gate_ref[0] * _rms(o_ref[...], g_ref[...])


def _out_proj(a, w, x2d, g, gate, tm, mod_row):
    rows = a.shape[0]
    tk = 512
    return pl.pallas_call(
        _out_proj_kernel,
        grid=(rows // tm, D_MODEL // tk),
        in_specs=[pl.BlockSpec((tm, tk), lambda i, k: (i, k)),
                  pl.BlockSpec((tk, D_MODEL), lambda i, k: (k, 0)),
                  pl.BlockSpec((tm, D_MODEL), lambda i, k: (i, 0)),
                  pl.BlockSpec((1, D_MODEL), lambda i, k: (0, 0)),
                  pl.BlockSpec((1, 1, D_MODEL), lambda i, k: (mod_row(i), 0, 0))],
        out_specs=pl.BlockSpec((tm, D_MODEL), lambda i, k: (i, 0)),
        out_shape=jax.ShapeDtypeStruct((rows, D_MODEL), F32),
        compiler_params=_cparams("arbitrary", "arbitrary"),
        name="out_proj",
    )(a, w, x2d, g.reshape(1, -1), gate)


def _ffn_kernel(x_ref, shift_ref, scale_ref, gpre_ref, w1_ref, w2_ref, gpost_ref, gate_ref, o_ref, f_ref):
    j = pl.program_id(1)

    @pl.when(j == 0)
    def _():
        f = _rms(x_ref[...], gpre_ref[...]) * (1.0 + scale_ref[0]) + shift_ref[0]
        f_ref[...] = f.astype(BF16)

    h = _dot(f_ref[...], w1_ref[...].astype(BF16))
    h = jnp.square(jnp.maximum(h, 0.0)).astype(BF16)
    _accumulate(o_ref, h, w2_ref, j == 0)

    @pl.when(j == pl.num_programs(1) - 1)
    def _():
        o_ref[...] = x_ref[...] + gate_ref[0] * _rms(o_ref[...], gpost_ref[...])


def _ffn(x2d, shift, scale, gpre, w1, w2, gpost, gate, tm, mod_row):
    rows = x2d.shape[0]
    tf = 256
    mod = pl.BlockSpec((1, 1, D_MODEL), lambda i, j: (mod_row(i), 0, 0))
    vec = pl.BlockSpec((1, D_MODEL), lambda i, j: (0, 0))
    return pl.pallas_call(
        _ffn_kernel,
        grid=(rows // tm, D_FF // tf),
        in_specs=[pl.BlockSpec((tm, D_MODEL), lambda i, j: (i, 0)), mod, mod, vec,
                  pl.BlockSpec((D_MODEL, tf), lambda i, j: (0, j)),
                  pl.BlockSpec((tf, D_MODEL), lambda i, j: (j, 0)),
                  vec, mod],
        out_specs=pl.BlockSpec((tm, D_MODEL), lambda i, j: (i, 0)),
        out_shape=jax.ShapeDtypeStruct((rows, D_MODEL), F32),
        scratch_shapes=[pltpu.VMEM((tm, D_MODEL), BF16)],
        compiler_params=_cparams("arbitrary", "arbitrary"),
        name="ffn",
    )(x2d, shift, scale, gpre.reshape(1, -1), w1, w2, gpost.reshape(1, -1), gate)


def _rope_tables():
    t = jnp.arange(SEQ)
    row = (t // GRID_W).astype(F32)[:, None]
    col = (t % GRID_W).astype(F32)[:, None]

    def tables(rot_dim):
        quarter = rot_dim // 4
        inv_freq = ROPE_BASE ** (-jnp.arange(quarter, dtype=F32) / quarter)
        ar, ac = row * inv_freq, col * inv_freq
        cos = jnp.concatenate([jnp.cos(ar), jnp.cos(ar), jnp.cos(ac), jnp.cos(ac)], axis=1)
        sin = jnp.concatenate([-jnp.sin(ar), jnp.sin(ar), -jnp.sin(ac), jnp.sin(ac)], axis=1)
        pad = 128 - rot_dim
        if pad:
            cos = jnp.concatenate([cos, jnp.ones((SEQ, pad), F32)], axis=1)
            sin = jnp.concatenate([sin, jnp.zeros((SEQ, pad), F32)], axis=1)
        return cos, sin

    return tables(HEAD_DIM), tables(MLA_ROPE)


def kernel(x, c, ctx, c_ctx, ada_w, ada_b, norm_mix_pre, norm_mix_post, norm_ffn_pre, norm_ffn_post, w_in, w_out, s5_a_re, s5_a_im, s5_log_dt, s5_b_re, s5_b_im, s5_c_re, s5_c_im, s5_d, s5_glu_w, s5_glu_b, swa_sink, mla_q_norm, mla_w_uq, mla_kv_norm, mla_w_ukv, ffn_w1, ffn_w2):
    (cos_swa, sin_swa), (cos_mla, sin_mla) = _rope_tables()

    cvec = jnp.concatenate([c, c_ctx[None, :], jnp.zeros((3, D_MODEL), F32)], axis=0)
    mods = _ada(cvec, ada_w, ada_b)

    tm_l = 1024
    lat_row = lambda i: i // (SEQ // tm_l)
    ctx_row = lambda i: 4

    xl = x.reshape(BATCH * SEQ, D_MODEL)
    xc = ctx.reshape(BATCH * CTX_LEN, D_MODEL)

    for i in range(DEPTH):
        need_ctx = i < DEPTH - 1
        mod = [mods[i, :, k * D_MODEL:(k + 1) * D_MODEL].reshape(8, 1, D_MODEL) for k in range(6)]

        w_in_pad = jnp.pad(w_in[i], ((0, 0), (0, IN_PAD - IN_WIDTH)))
        w_uq_pad = jnp.pad(mla_w_uq[i].reshape(MLA_Q_RANK, MLA_HEADS, MLA_NOPE + MLA_ROPE),
                           ((0, 0), (0, 0), (0, MLA_QK_PAD - MLA_NOPE - MLA_ROPE))
                           ).reshape(MLA_Q_RANK, MLA_HEADS * MLA_QK_PAD)
        w_ukv3 = mla_w_ukv[i].reshape(MLA_KV_RANK, MLA_HEADS, MLA_NOPE + MLA_V)
        w_ukv_perm = jnp.concatenate([w_ukv3[:, :, :MLA_NOPE].reshape(MLA_KV_RANK, -1),
                                      w_ukv3[:, :, MLA_NOPE:].reshape(MLA_KV_RANK, -1)], axis=1)
        g_pre = norm_mix_pre[i].reshape(1, -1)

        z_l = _in_proj(xl, mod[0], mod[1], g_pre, w_in_pad, tm_l, lat_row)
        z_c = _in_proj(xc, mod[0], mod[1], g_pre, w_in_pad, tm_l, ctx_row)

        s5_w = _s5_disc(s5_a_re[i], s5_a_im[i], s5_log_dt[i], s5_b_re[i], s5_b_im[i], s5_c_re[i], s5_c_im[i])
        s5_c, s5_l = _s5_mixer(z_c, z_l, s5_w, s5_d[i], s5_glu_w[i], s5_glu_b[i], need_ctx)

        qk = _swa_rope(z_l, cos_swa, sin_swa)
        swa_l = _swa_latent(qk, z_l, z_c, swa_sink[i])

        q_l = _mla_q(z_l, mla_q_norm[i], w_uq_pad, cos_mla, sin_mla, 1024, True, SEQ)
        k_l, v_l = _mla_kv(z_l, mla_kv_norm[i], w_ukv_perm, cos_mla, sin_mla, 1024, True, SEQ)
        k_c, v_c = _mla_kv(z_c, mla_kv_norm[i], w_ukv_perm, cos_mla, sin_mla, 1024, False, 1024)
        mla_l = _mla_attn(q_l, [k_c, k_l], [v_c, v_l], SEQ, 256)

        a_l = jnp.concatenate([s5_l, swa_l, mla_l], axis=1)
        xl = _out_proj(a_l, w_out[i], xl, norm_mix_post[i], mod[2], tm_l, lat_row)
        xl = _ffn(xl, mod[3], mod[4], norm_ffn_pre[i], ffn_w1[i], ffn_w2[i], norm_ffn_post[i], mod[5],
                  tm_l, lat_row)

        if need_ctx:
            swa_c = _swa_context(z_c, swa_sink[i])
            q_c = _mla_q(z_c, mla_q_norm[i], w_uq_pad, cos_mla, sin_mla, 1024, False, 1024)
            mla_c = _mla_attn(q_c, [k_c], [v_c], CTX_LEN, 256)
            a_c = jnp.concatenate([s5_c, swa_c, mla_c], axis=1)
            xc = _out_proj(a_c, w_out[i], xc, norm_mix_post[i], mod[2], tm_l, ctx_row)
            xc = _ffn(xc, mod[3], mod[4], norm_ffn_pre[i], ffn_w1[i], ffn_w2[i], norm_ffn_post[i], mod[5],
                      tm_l, ctx_row)

    return xl.reshape(BATCH, SEQ, D_MODEL)
```

```python
import functools
import math

import jax
import jax.numpy as jnp
from jax import lax
from jax.experimental import pallas as pl
from jax.experimental.pallas import tpu as pltpu

F32 = jnp.float32
BF16 = jnp.bfloat16

D_MODEL = 2048
BATCH = 4
SEQ = 2048
DEPTH = 2
GRID_W = 64
CTX_LEN = 256
EPS = 1e-6
ROPE_BASE = 10000.0
NEG_INF = -1e30
BLOCK = 128
HEAD_DIM = 128
S5_WIDTH = 512
S5_GROUP = 16
S5_GROUPS = 32
S5_STATE = 64
S5_LANES = S5_GROUPS * S5_STATE
SWA_HEADS = 6
SWA_KV_HEADS = 2
SWA_REP = SWA_HEADS // SWA_KV_HEADS
MLA_HEADS = 6
MLA_Q_RANK = 768
MLA_KV_RANK = 512
MLA_NOPE = 128
MLA_ROPE = 64
MLA_V = 128
MLA_QK_PAD = 256
D_FF = 4 * D_MODEL
IN_WIDTHS = (512, 768, 256, 256, 768, 512, 64)
IN_WIDTH = sum(IN_WIDTHS)
IN_PAD = 3584
OFF_U, OFF_QS, OFF_KS, OFF_VS, OFF_CQ, OFF_CKV, OFF_KR = 0, 512, 1280, 1536, 1792, 2560, 3072

VMEM_LIMIT = 56 * 1024 * 1024


def _cparams(*sem):
    return pltpu.CompilerParams(dimension_semantics=sem, vmem_limit_bytes=VMEM_LIMIT)


def _dot(a, b):
    return jnp.dot(a, b, preferred_element_type=F32)


def _dot_nt(a, b):
    return lax.dot_general(a, b, (((1,), (1,)), ((), ())), preferred_element_type=F32)


def _rms(x, g):
    return x * lax.rsqrt(jnp.mean(x * x, axis=-1, keepdims=True) + EPS) * g


def _rope(x, cos, sin, half):
    lane = lax.broadcasted_iota(jnp.int32, x.shape, 1)
    fwd = pltpu.roll(x, 128 - half, 1)
    bwd = pltpu.roll(x, half, 1)
    sw = jnp.where((lane % (2 * half)) < half, fwd, bwd)
    return x * cos + sw * sin


def _ada_kernel(c_ref, w_ref, b_ref, o_ref):
    c = c_ref[...]
    s = c * jax.nn.sigmoid(c)
    o_ref[0] = _dot(s.astype(BF16), w_ref[0].astype(BF16)) + b_ref[0]


def _ada(cvec, ada_w, ada_b):
    tn = 1024
    n = 6 * D_MODEL
    return pl.pallas_call(
        _ada_kernel,
        grid=(DEPTH, n // tn),
        in_specs=[pl.BlockSpec((8, D_MODEL), lambda l, j: (0, 0)),
                  pl.BlockSpec((1, D_MODEL, tn), lambda l, j: (l, 0, j)),
                  pl.BlockSpec((1, 1, tn), lambda l, j: (l, 0, j))],
        out_specs=pl.BlockSpec((1, 8, tn), lambda l, j: (l, 0, j)),
        out_shape=jax.ShapeDtypeStruct((DEPTH, 8, n), F32),
        compiler_params=_cparams("arbitrary", "arbitrary"),
        name="ada",
    )(cvec, ada_w, ada_b.reshape(DEPTH, 1, n))


def _in_proj_kernel(x_ref, shift_ref, scale_ref, g_ref, w_ref, o_ref, h_ref):
    h = _rms(x_ref[...], g_ref[...]) * (1.0 + scale_ref[0]) + shift_ref[0]
    h_ref[...] = h.astype(BF16)
    nc = 512
    for n in range(IN_PAD // nc):
        cs = slice(n * nc, (n + 1) * nc)
        o_ref[:, cs] = _dot(h_ref[...], w_ref[:, cs])


def _in_proj(x2d, shift, scale, g, w_pad, tm, mod_row):
    rows = x2d.shape[0]
    return pl.pallas_call(
        _in_proj_kernel,
        grid=(rows // tm,),
        in_specs=[pl.BlockSpec((tm, D_MODEL), lambda i: (i, 0)),
                  pl.BlockSpec((1, 1, D_MODEL), lambda i: (mod_row(i), 0, 0)),
                  pl.BlockSpec((1, 1, D_MODEL), lambda i: (mod_row(i), 0, 0)),
                  pl.BlockSpec((1, D_MODEL), lambda i: (0, 0)),
                  pl.BlockSpec((D_MODEL, IN_PAD), lambda i: (0, 0), pipeline_mode=pl.Buffered(1))],
        out_specs=pl.BlockSpec((tm, IN_PAD), lambda i: (i, 0)),
        out_shape=jax.ShapeDtypeStruct((rows, IN_PAD), F32),
        scratch_shapes=[pltpu.VMEM((tm, D_MODEL), BF16)],
        compiler_params=_cparams("arbitrary"),
        name="in_proj",
    )(x2d, shift, scale, g, w_pad)


def _s5_disc_kernel(are_ref, aim_ref, ldt_ref, ebr_ref, ebi_ref, cr_ref, ci_ref,
                    lam_ref, wbr_ref, wbi_ref, wcr_ref, wci_ref):
    are = are_ref[0]
    aim = aim_ref[0]
    dt = jnp.exp(ldt_ref[0])
    mag = jnp.exp(are * dt)
    lr = mag * jnp.cos(aim * dt)
    li = mag * jnp.sin(aim * dt)
    den = are * are + aim * aim
    nr = lr - 1.0
    fr = (nr * are + li * aim) / den
    fi = (li * are - nr * aim) / den
    lam_ref[0, 0:1, :] = lr
    lam_ref[0, 1:2, :] = li
    half = S5_LANES // 2
    for h in range(2):
        frh = fr[:, h * half:(h + 1) * half]
        fih = fi[:, h * half:(h + 1) * half]
        wbr_ref[0, h] = (frh * ebr_ref[0, h] - fih * ebi_ref[0, h]).astype(BF16)
        wbi_ref[0, h] = (frh * ebi_ref[0, h] + fih * ebr_ref[0, h]).astype(BF16)
        wcr_ref[0, h] = cr_ref[0, h].astype(BF16)
        wci_ref[0, h] = ci_ref[0, h].astype(BF16)


def _s5_disc(a_re, a_im, log_dt, b_re, b_im, c_re, c_im):
    gh = S5_GROUPS // 2
    eye = jnp.eye(gh, dtype=F32)
    are = a_re.reshape(2, 1, S5_LANES)
    aim = a_im.reshape(2, 1, S5_LANES)
    ldt = jnp.repeat(log_dt, S5_STATE, axis=-1).reshape(2, 1, S5_LANES)

    def expand_b(b):
        b5 = b.reshape(2, 2, gh, S5_STATE, S5_GROUP)
        return jnp.einsum('dhgnp,gk->dhgpkn', b5, eye).reshape(2, 2, gh * S5_GROUP, gh * S5_STATE)

    def expand_c(c):
        c5 = c.reshape(2, 2, gh, S5_GROUP, S5_STATE)
        return jnp.einsum('dhgpn,gk->dhgnkp', c5, eye).reshape(2, 2, gh * S5_STATE, gh * S5_GROUP)

    kin, kst = gh * S5_GROUP, gh * S5_STATE
    vec = pl.BlockSpec((1, 1, S5_LANES), lambda d: (d, 0, 0))
    bspec = pl.BlockSpec((1, 2, kin, kst), lambda d: (d, 0, 0, 0))
    cspec = pl.BlockSpec((1, 2, kst, kin), lambda d: (d, 0, 0, 0))
    return pl.pallas_call(
        _s5_disc_kernel,
        grid=(2,),
        in_specs=[vec, vec, vec, bspec, bspec, cspec, cspec],
        out_specs=[pl.BlockSpec((1, 2, S5_LANES), lambda d: (d, 0, 0)), bspec, bspec, cspec, cspec],
        out_shape=[jax.ShapeDtypeStruct((2, 2, S5_LANES), F32),
                   jax.ShapeDtypeStruct((2, 2, kin, kst), BF16),
                   jax.ShapeDtypeStruct((2, 2, kin, kst), BF16),
                   jax.ShapeDtypeStruct((2, 2, kst, kin), BF16),
                   jax.ShapeDtypeStruct((2, 2, kst, kin), BF16)],
        compiler_params=_cparams("arbitrary"),
        name="s5_disc",
    )(are, aim, ldt, expand_b(b_re), expand_b(b_im), expand_c(c_re), expand_c(c_im))


S5_TC = 128
S5_CTX_CHUNKS = CTX_LEN // S5_TC
S5_LAT_CHUNKS = SEQ // S5_TC


def _s5_scan_kernel(ucf_ref, ulf_ref, ucb_ref, ulb_ref, lam_ref, wbr_ref, wbi_ref, wcr_ref, wci_ref,
                    yf_ref, yb_ref, fr_ref, fi_ref, gr_ref, gi_ref, sr_ref, si_ref, *, skip_chunks):
    c = pl.program_id(0)
    tc = S5_TC
    kin, kst = S5_WIDTH // 2, S5_LANES // 2

    @pl.when(c == 0)
    def _():
        sr_ref[...] = jnp.zeros_like(sr_ref)
        si_ref[...] = jnp.zeros_like(si_ref)

    nlb = kst // 128

    def project(u_ref, d, dst_r, dst_i):
        u = u_ref[...].reshape(BATCH * tc, S5_WIDTH).astype(BF16)
        for h in range(2):
            uh = u[:, h * kin:(h + 1) * kin]
            for w_ref, dst in ((wbr_ref, dst_r), (wbi_ref, dst_i)):
                p = _dot(uh, w_ref[d, h])
                for j in range(nlb):
                    for b in range(BATCH):
                        dst[h * nlb + j, pl.ds(b, tc, stride=BATCH), :] = (
                            p[b * tc:(b + 1) * tc, j * 128:(j + 1) * 128])

    @pl.when(c < S5_CTX_CHUNKS)
    def _():
        project(ucf_ref, 0, fr_ref, fi_ref)
        project(ucb_ref, 1, gr_ref, gi_ref)

    @pl.when(c >= S5_CTX_CHUNKS)
    def _():
        project(ulf_ref, 0, fr_ref, fi_ref)
        project(ulb_ref, 1, gr_ref, gi_ref)

    nb = 8
    for lb in range(2 * nlb // nb):
        ls = slice(lb * nb, (lb + 1) * nb)
        fwd_rows = lax.broadcasted_iota(jnp.int32, (nb, 8, 128), 1) < BATCH
        la_r = jnp.where(fwd_rows, lam_ref[0, 0, ls], lam_ref[1, 0, ls])
        la_i = jnp.where(fwd_rows, lam_ref[0, 1, ls], lam_ref[1, 1, ls])
        lb_r = jnp.where(fwd_rows, lam_ref[1, 0, ls], lam_ref[0, 0, ls])
        lb_i = jnp.where(fwd_rows, lam_ref[1, 1, ls], lam_ref[0, 1, ls])

        def body(k, carry, ls=ls, fwd_rows=fwd_rows, la_r=la_r, la_i=la_i, lb_r=lb_r, lb_i=lb_i):
            s_r, s_i = carry
            rf = pl.ds(pl.multiple_of(k * 8, 8), 8)
            rb = pl.ds(pl.multiple_of((tc // 2 - 1 - k) * 8, 8), 8)
            f_r, f_i = fr_ref[ls, rf, :], fi_ref[ls, rf, :]
            g_r, g_i = gr_ref[ls, rb, :], gi_ref[ls, rb, :]
            a_r = la_r * s_r - la_i * s_i + jnp.where(fwd_rows, f_r, g_r)
            a_i = la_r * s_i + la_i * s_r + jnp.where(fwd_rows, f_i, g_i)
            t_r = pltpu.roll(a_r, BATCH, 1)
            t_i = pltpu.roll(a_i, BATCH, 1)
            b_r = lb_r * t_r - lb_i * t_i + jnp.where(fwd_rows, g_r, f_r)
            b_i = lb_r * t_i + lb_i * t_r + jnp.where(fwd_rows, g_i, f_i)
            fr_ref[ls, rf, :] = jnp.where(fwd_rows, a_r, b_r)
            fi_ref[ls, rf, :] = jnp.where(fwd_rows, a_i, b_i)
            gr_ref[ls, rb, :] = jnp.where(fwd_rows, b_r, a_r)
            gi_ref[ls, rb, :] = jnp.where(fwd_rows, b_i, a_i)
            return pltpu.roll(b_r, BATCH, 1), pltpu.roll(b_i, BATCH, 1)

        s_r, s_i = lax.fori_loop(0, tc // 2, body, (sr_ref[ls], si_ref[ls]))
        sr_ref[ls] = s_r
        si_ref[ls] = s_i

    def gather_rows(src, h):
        return jnp.concatenate(
            [jnp.concatenate([src[h * nlb + j, pl.ds(b, tc, stride=BATCH), :] for j in range(nlb)], axis=1)
             for b in range(BATCH)], axis=0)

    def readout(d, src_r, src_i, y_ref):
        for h in range(2):
            hr = gather_rows(src_r, h).astype(BF16)
            hi = gather_rows(src_i, h).astype(BF16)
            y = _dot(hr, wcr_ref[d, h]) - _dot(hi, wci_ref[d, h])
            y_ref[:, :, h * kin:(h + 1) * kin] = y.reshape(BATCH, tc, kin)

    @pl.when(c >= skip_chunks)
    def _():
        readout(0, fr_ref, fi_ref, yf_ref)
        readout(1, gr_ref, gi_ref, yb_ref)


def _s5_scan(z_c, z_l, lam, wbr, wbi, wcr, wci, skip_chunks):
    tc = S5_TC
    nchunk = S5_CTX_CHUNKS + S5_LAT_CHUNKS
    kin, kst = S5_WIDTH // 2, S5_LANES // 2
    zc3 = z_c.reshape(BATCH, CTX_LEN, IN_PAD)
    zl3 = z_l.reshape(BATCH, SEQ, IN_PAD)

    def bwd_chunk(c):
        return jnp.where(c < S5_CTX_CHUNKS, S5_CTX_CHUNKS - 1 - c, nchunk + S5_CTX_CHUNKS - 1 - c)

    def ctx_spec(chunk_of):
        return pl.BlockSpec((BATCH, tc, S5_WIDTH),
                            lambda c: (0, jnp.clip(chunk_of(c), 0, S5_CTX_CHUNKS - 1), 0))

    def lat_spec(chunk_of):
        return pl.BlockSpec((BATCH, tc, S5_WIDTH),
                            lambda c: (0, jnp.clip(chunk_of(c) - S5_CTX_CHUNKS, 0, S5_LAT_CHUNKS - 1), 0))

    def out_spec(chunk_of):
        return pl.BlockSpec((BATCH, tc, S5_WIDTH),
                            lambda c: (0, chunk_of(jnp.maximum(c, skip_chunks)) - skip_chunks, 0))

    once = pl.Buffered(1)
    bspec = pl.BlockSpec((2, 2, kin, kst), lambda c: (0, 0, 0, 0), pipeline_mode=once)
    cspec = pl.BlockSpec((2, 2, kst, kin), lambda c: (0, 0, 0, 0), pipeline_mode=once)
    t_out = (nchunk - skip_chunks) * tc
    nblk = S5_LANES // 128
    big = pltpu.VMEM((nblk, tc * BATCH, 128), F32)
    state = pltpu.VMEM((nblk, 8, 128), F32)
    lam = lam.reshape(2, 2, nblk, 1, 128)
    fwd_chunk = lambda c: c
    return pl.pallas_call(
        functools.partial(_s5_scan_kernel, skip_chunks=skip_chunks),
        grid=(nchunk,),
        in_specs=[ctx_spec(fwd_chunk), lat_spec(fwd_chunk), ctx_spec(bwd_chunk), lat_spec(bwd_chunk),
                  pl.BlockSpec((2, 2, nblk, 1, 128), lambda c: (0, 0, 0, 0, 0)),
                  bspec, bspec, cspec, cspec],
        out_specs=[out_spec(fwd_chunk), out_spec(bwd_chunk)],
        out_shape=[jax.ShapeDtypeStruct((BATCH, t_out, S5_WIDTH), F32)] * 2,
        scratch_shapes=[big, big, big, big, state, state],
        compiler_params=_cparams("arbitrary"),
        name="s5_scan",
    )(zc3, zl3, zc3, zl3, lam, wbr, wbi, wcr, wci)


def _s5_glu_kernel(u_ref, yf_ref, yb_ref, d_ref, w_ref, b_ref, o_ref):
    y = d_ref[...] * u_ref[0] + yf_ref[0] + yb_ref[0]
    g = jax.nn.gelu(y)
    gate = jax.nn.sigmoid(_dot(g.astype(BF16), w_ref[...]) + b_ref[...])
    o_ref[0] = (g * gate).astype(o_ref.dtype)


def _s5_glu(z3, yf, yb, d, w, b, y_off):
    tm = 256
    n = z3.shape[1]
    off = y_off // tm
    zspec = pl.BlockSpec((1, tm, S5_WIDTH), lambda bi, i: (bi, i, 0))
    yspec = pl.BlockSpec((1, tm, S5_WIDTH), lambda bi, i: (bi, i + off, 0))
    vec = pl.BlockSpec((1, S5_WIDTH), lambda bi, i: (0, 0))
    return pl.pallas_call(
        _s5_glu_kernel,
        grid=(BATCH, n // tm),
        in_specs=[zspec, yspec, yspec, vec, pl.BlockSpec((S5_WIDTH, S5_WIDTH), lambda bi, i: (0, 0)), vec],
        out_specs=zspec,
        out_shape=jax.ShapeDtypeStruct((BATCH, n, S5_WIDTH), BF16),
        compiler_params=_cparams("arbitrary", "arbitrary"),
        name="s5_glu",
    )(z3, yf, yb, d.reshape(1, -1), w, b.reshape(1, -1)).reshape(BATCH * n, S5_WIDTH)


def _s5_mixer(z_c, z_l, s5_w, d, glu_w, glu_b, need_ctx):
    lam, wbr, wbi, wcr, wci = s5_w
    skip = 0 if need_ctx else S5_CTX_CHUNKS
    yf, yb = _s5_scan(z_c, z_l, lam, wbr, wbi, wcr, wci, skip)
    nctx = CTX_LEN if need_ctx else 0
    s5_l = _s5_glu(z_l.reshape(BATCH, SEQ, IN_PAD), yf, yb, d, glu_w, glu_b, nctx)
    s5_c = None
    if need_ctx:
        s5_c = _s5_glu(z_c.reshape(BATCH, CTX_LEN, IN_PAD), yf, yb, d, glu_w, glu_b, 0)
    return s5_c, s5_l


def _swa_rope_kernel(za_ref, zb_ref, cos_ref, sin_ref, o_ref):
    cos = cos_ref[...]
    sin = sin_ref[...]
    scale = HEAD_DIM ** -0.5
    per = (SWA_HEADS + SWA_KV_HEADS) // 2
    for h in range(SWA_HEADS + SWA_KV_HEADS):
        src = za_ref if h < per else zb_ref
        lo = (h % per) * HEAD_DIM
        x = _rope(src[:, lo:lo + HEAD_DIM], cos, sin, HEAD_DIM // 4)
        if h < SWA_HEADS:
            x = x * scale
        o_ref[:, h * HEAD_DIM:(h + 1) * HEAD_DIM] = x.astype(BF16)


def _swa_rope(z_l, cos, sin):
    tm = 512
    w = (SWA_HEADS + SWA_KV_HEADS) * HEAD_DIM
    c0 = OFF_QS // (w // 2)
    tab = pl.BlockSpec((tm, HEAD_DIM), lambda i: (i % (SEQ // tm), 0))
    return pl.pallas_call(
        _swa_rope_kernel,
        grid=(BATCH * SEQ // tm,),
        in_specs=[pl.BlockSpec((tm, w // 2), lambda i: (i, c0)),
                  pl.BlockSpec((tm, w // 2), lambda i: (i, c0 + 1)),
                  tab, tab],
        out_specs=pl.BlockSpec((tm, w), lambda i: (i, 0)),
        out_shape=jax.ShapeDtypeStruct((BATCH * SEQ, w), BF16),
        compiler_params=_cparams("arbitrary"),
        name="swa_rope",
    )(z_l, z_l, cos, sin)


def _sink_rows(sink_ref, g, rows):
    r = lax.broadcasted_iota(jnp.int32, (rows, 1), 0) // (rows // SWA_REP)
    s0, s1, s2 = sink_ref[SWA_REP * g], sink_ref[SWA_REP * g + 1], sink_ref[SWA_REP * g + 2]
    return jnp.where(r == 0, s0, jnp.where(r == 1, s1, s2))


def _stack_heads(x, n):
    return jnp.concatenate([x[:, h * HEAD_DIM:(h + 1) * HEAD_DIM] for h in range(n)], axis=0)


def _unstack_heads(x, n):
    rows = x.shape[0] // n
    return jnp.concatenate([x[h * rows:(h + 1) * rows] for h in range(n)], axis=1)


def _swa_latent_kernel(sink_ref, q_ref, kp_ref, kc_ref, kn_ref, vp_ref, vc_ref, vn_ref,
                       kx_ref, vx_ref, o_ref):
    g = pl.program_id(1)
    n = pl.program_id(2)
    q = _stack_heads(q_ref[...], SWA_REP)
    kb = jnp.concatenate([kp_ref[...], kc_ref[...], kn_ref[...]], axis=0)
    vb = jnp.concatenate([vp_ref[...], vc_ref[...], vn_ref[...]], axis=0).astype(BF16)
    kx = (kx_ref[...]).astype(BF16)
    vx = vx_ref[...].astype(BF16)
    s_ctx = _dot_nt(q, kx)
    s_band = _dot_nt(q, kb)
    rows = SWA_REP * BLOCK
    qi = lax.broadcasted_iota(jnp.int32, (rows, 3 * BLOCK), 0) % BLOCK
    kj = lax.broadcasted_iota(jnp.int32, (rows, 3 * BLOCK), 1)
    kpos = (n - 1) * BLOCK + kj
    valid = (kj >= qi) & (kj <= qi + 2 * BLOCK) & (kpos >= 0) & (kpos < SEQ)
    s_band = jnp.where(valid, s_band, NEG_INF)
    sk = _sink_rows(sink_ref, g, rows)
    m = jnp.maximum(jnp.maximum(jnp.max(s_ctx, axis=-1, keepdims=True),
                                jnp.max(s_band, axis=-1, keepdims=True)), sk)
    p_ctx = jnp.exp(s_ctx - m)
    p_band = jnp.exp(s_band - m)
    den = (jnp.sum(p_ctx, axis=-1, keepdims=True) + jnp.sum(p_band, axis=-1, keepdims=True)
           + jnp.exp(sk - m))
    o = (_dot(p_ctx.astype(BF16), vx) + _dot(p_band.astype(BF16), vb)) / den
    o_ref[...] = _unstack_heads(o, SWA_REP).astype(o_ref.dtype)


def _swa_latent(qk, z_l, z_c, sink):
    nb = SEQ // BLOCK
    qw = SWA_REP * HEAD_DIM
    kcol = SWA_HEADS
    vcol = OFF_VS // HEAD_DIM
    kxcol = OFF_KS // HEAD_DIM

    def band(col, shift):
        return pl.BlockSpec(
            (BLOCK, HEAD_DIM),
            lambda b, g, n, s: (b * nb + jnp.clip(n + shift, 0, nb - 1), col + g))

    grid_spec = pltpu.PrefetchScalarGridSpec(
        num_scalar_prefetch=1,
        grid=(BATCH, SWA_KV_HEADS, nb),
        in_specs=[pl.BlockSpec((BLOCK, qw), lambda b, g, n, s: (b * nb + n, g)),
                  band(kcol, -1), band(kcol, 0), band(kcol, 1),
                  band(vcol, -1), band(vcol, 0), band(vcol, 1),
                  pl.BlockSpec((CTX_LEN, HEAD_DIM), lambda b, g, n, s: (b, kxcol + g)),
                  pl.BlockSpec((CTX_LEN, HEAD_DIM), lambda b, g, n, s: (b, vcol + g))],
        out_specs=pl.BlockSpec((BLOCK, qw), lambda b, g, n, s: (b * nb + n, g)),
    )
    return pl.pallas_call(
        _swa_latent_kernel,
        grid_spec=grid_spec,
        out_shape=jax.ShapeDtypeStruct((BATCH * SEQ, SWA_HEADS * HEAD_DIM), BF16),
        compiler_params=_cparams("arbitrary", "arbitrary", "arbitrary"),
        name="swa_latent",
    )(sink, qk, qk, qk, qk, z_l, z_l, z_l, z_c, z_c)


def _swa_context_kernel(sink_ref, q_ref, k_ref, v_ref, o_ref):
    g = pl.program_id(1)
    q = (_stack_heads(q_ref[...], SWA_REP) * (HEAD_DIM ** -0.5)).astype(BF16)
    s = _dot_nt(q, k_ref[...].astype(BF16))
    sk = _sink_rows(sink_ref, g, SWA_REP * CTX_LEN)
    m = jnp.maximum(jnp.max(s, axis=-1, keepdims=True), sk)
    p = jnp.exp(s - m)
    den = jnp.sum(p, axis=-1, keepdims=True) + jnp.exp(sk - m)
    o = _dot(p.astype(BF16), v_ref[...].astype(BF16)) / den
    o_ref[...] = _unstack_heads(o, SWA_REP).astype(o_ref.dtype)


def _swa_context(z_c, sink):
    qw = SWA_REP * HEAD_DIM
    grid_spec = pltpu.PrefetchScalarGridSpec(
        num_scalar_prefetch=1,
        grid=(BATCH, SWA_KV_HEADS),
        in_specs=[pl.BlockSpec((CTX_LEN, qw), lambda b, g, s: (b, g)),
                  pl.BlockSpec((CTX_LEN, HEAD_DIM), lambda b, g, s: (b, OFF_KS // HEAD_DIM + g)),
                  pl.BlockSpec((CTX_LEN, HEAD_DIM), lambda b, g, s: (b, OFF_VS // HEAD_DIM + g))],
        out_specs=pl.BlockSpec((CTX_LEN, qw), lambda b, g, s: (b, g)),
    )
    return pl.pallas_call(
        _swa_context_kernel,
        grid_spec=grid_spec,
        out_shape=jax.ShapeDtypeStruct((BATCH * CTX_LEN, SWA_HEADS * HEAD_DIM), BF16),
        compiler_params=_cparams("arbitrary", "arbitrary"),
        name="swa_context",
    )(sink, z_c[:, OFF_QS:OFF_QS + SWA_HEADS * HEAD_DIM], z_c, z_c)


def _mla_q_kernel(c0_ref, c1_ref, c2_ref, g_ref, w_ref, cos_ref, sin_ref, o_ref, *, rope):
    cq = jnp.concatenate([c0_ref[...], c1_ref[...], c2_ref[...]], axis=1)
    q = _dot(_rms(cq, g_ref[...]).astype(BF16), w_ref[...])
    scale = (MLA_NOPE + MLA_ROPE) ** -0.5
    for h in range(MLA_HEADS):
        lo = h * MLA_QK_PAD
        o_ref[:, lo:lo + MLA_NOPE] = (q[:, lo:lo + MLA_NOPE] * scale).astype(BF16)
        r = q[:, lo + MLA_NOPE:lo + MLA_QK_PAD]
        if rope:
            r = _rope(r, cos_ref[...], sin_ref[...], MLA_ROPE // 4)
        o_ref[:, lo + MLA_NOPE:lo + MLA_QK_PAD] = (r * scale).astype(BF16)


def _mla_q(z, g, w_pad, cos, sin, tm, rope, seq):
    rows = z.shape[0]
    c0 = OFF_CQ // 256
    tab = pl.BlockSpec((tm, 128), lambda i: (i % (seq // tm), 0))
    return pl.pallas_call(
        functools.partial(_mla_q_kernel, rope=rope),
        grid=(rows // tm,),
        in_specs=[pl.BlockSpec((tm, 256), lambda i: (i, c0)),
                  pl.BlockSpec((tm, 256), lambda i: (i, c0 + 1)),
                  pl.BlockSpec((tm, 256), lambda i: (i, c0 + 2)),
                  pl.BlockSpec((1, MLA_Q_RANK), lambda i: (0, 0)),
                  pl.BlockSpec((MLA_Q_RANK, MLA_HEADS * MLA_QK_PAD), lambda i: (0, 0)),
                  tab, tab],
        out_specs=pl.BlockSpec((tm, MLA_HEADS * MLA_QK_PAD), lambda i: (i, 0)),
        out_shape=jax.ShapeDtypeStruct((rows, MLA_HEADS * MLA_QK_PAD), BF16),
        compiler_params=_cparams("arbitrary"),
        name="mla_q",
    )(z, z, z, g.reshape(1, -1), w_pad, cos, sin)


def _mla_kv_kernel(ckv_ref, kr_ref, g_ref, w_ref, cos_ref, sin_ref, k_ref, v_ref, *, rope):
    kv = _dot(_rms(ckv_ref[...], g_ref[...]).astype(BF16), w_ref[...])
    kr = kr_ref[...]
    if rope:
        kr = _rope(kr, cos_ref[...], sin_ref[...], MLA_ROPE // 4)
    kr = kr.astype(BF16)
    for h in range(MLA_HEADS):
        lo = h * MLA_QK_PAD
        k_ref[:, lo:lo + MLA_NOPE] = kv[:, h * MLA_NOPE:(h + 1) * MLA_NOPE].astype(BF16)
        k_ref[:, lo + MLA_NOPE:lo + MLA_QK_PAD] = kr
    v_ref[...] = kv[:, MLA_HEADS * MLA_NOPE:].astype(BF16)


def _mla_kv(z, g, w_perm, cos, sin, tm, rope, seq):
    rows = z.shape[0]
    tab = pl.BlockSpec((tm, 128), lambda i: (i % (seq // tm), 0))
    return pl.pallas_call(
        functools.partial(_mla_kv_kernel, rope=rope),
        grid=(rows // tm,),
        in_specs=[pl.BlockSpec((tm, MLA_KV_RANK), lambda i: (i, OFF_CKV // MLA_KV_RANK)),
                  pl.BlockSpec((tm, 128), lambda i: (i, OFF_KR // 128)),
                  pl.BlockSpec((1, MLA_KV_RANK), lambda i: (0, 0)),
                  pl.BlockSpec((MLA_KV_RANK, MLA_HEADS * (MLA_NOPE + MLA_V)), lambda i: (0, 0)),
                  tab, tab],
        out_specs=[pl.BlockSpec((tm, MLA_HEADS * MLA_QK_PAD), lambda i: (i, 0)),
                   pl.BlockSpec((tm, MLA_HEADS * MLA_V), lambda i: (i, 0))],
        out_shape=[jax.ShapeDtypeStruct((rows, MLA_HEADS * MLA_QK_PAD), BF16),
                   jax.ShapeDtypeStruct((rows, MLA_HEADS * MLA_V), BF16)],
        compiler_params=_cparams("arbitrary"),
        name="mla_kv",
    )(z, z, g.reshape(1, -1), w_perm, cos, sin)


def _mla_attn_kernel(*refs, nseg):
    q_ref = refs[0]
    k_refs = refs[1:1 + nseg]
    v_refs = refs[1 + nseg:1 + 2 * nseg]
    o_ref = refs[1 + 2 * nseg]
    for h in range(MLA_HEADS):
        q = q_ref[:, h * MLA_QK_PAD:(h + 1) * MLA_QK_PAD]
        s = [_dot_nt(q, k[:, h * MLA_QK_PAD:(h + 1) * MLA_QK_PAD]) for k in k_refs]
        m = functools.reduce(jnp.maximum, [jnp.max(x, axis=-1, keepdims=True) for x in s])
        p = [jnp.exp(x - m) for x in s]
        den = sum(jnp.sum(x, axis=-1, keepdims=True) for x in p)
        o = sum(_dot(x.astype(BF16), v[:, h * MLA_V:(h + 1) * MLA_V]) for x, v in zip(p, v_refs))
        o_ref[:, h * MLA_V:(h + 1) * MLA_V] = (o / den).astype(o_ref.dtype)


def _mla_attn(q, ks, vs, n_q, tq):
    nseg = len(ks)
    nq = n_q // tq
    lens = [k.shape[0] // BATCH for k in ks]
    kw, vw = MLA_HEADS * MLA_QK_PAD, MLA_HEADS * MLA_V
    in_specs = [pl.BlockSpec((tq, kw), lambda b, i: (b * nq + i, 0))]
    in_specs += [pl.BlockSpec((n, kw), lambda b, i: (b, 0)) for n in lens]
    in_specs += [pl.BlockSpec((n, vw), lambda b, i: (b, 0)) for n in lens]
    return pl.pallas_call(
        functools.partial(_mla_attn_kernel, nseg=nseg),
        grid=(BATCH, nq),
        in_specs=in_specs,
        out_specs=pl.BlockSpec((tq, vw), lambda b, i: (b * nq + i, 0)),
        out_shape=jax.ShapeDtypeStruct((BATCH * n_q, vw), BF16),
        compiler_params=_cparams("arbitrary", "arbitrary"),
        name="mla_attn",
    )(q, *ks, *vs)


def _out_proj_kernel(a_ref, w_ref, x_ref, g_ref, gate_ref, o_ref):
    nc = 512
    for n in range(D_MODEL // nc):
        cs = slice(n * nc, (n + 1) * nc)
        o_ref[:, cs] = _dot(a_ref[...], w_ref[:, cs])
    o_ref[...] = x_ref[...] + gate_ref[0] * _rms(o_ref[...], g_ref[...])


def _out_proj(a, w, x2d, g, gate, tm, mod_row):
    rows = a.shape[0]
    return pl.pallas_call(
        _out_proj_kernel,
        grid=(rows // tm,),
        in_specs=[pl.BlockSpec((tm, D_MODEL), lambda i: (i, 0)),
                  pl.BlockSpec((D_MODEL, D_MODEL), lambda i: (0, 0), pipeline_mode=pl.Buffered(1)),
                  pl.BlockSpec((tm, D_MODEL), lambda i: (i, 0)),
                  pl.BlockSpec((1, D_MODEL), lambda i: (0, 0)),
                  pl.BlockSpec((1, 1, D_MODEL), lambda i: (mod_row(i), 0, 0))],
        out_specs=pl.BlockSpec((tm, D_MODEL), lambda i: (i, 0)),
        out_shape=jax.ShapeDtypeStruct((rows, D_MODEL), F32),
        compiler_params=_cparams("arbitrary"),
        name="out_proj",
    )(a, w, x2d, g.reshape(1, -1), gate)


def _ffn_kernel(x_ref, shift_ref, scale_ref, gpre_ref, w1_ref, w2_ref, gpost_ref, gate_ref, o_ref,
                f_ref, h_ref):
    j = pl.program_id(1)

    @pl.when(j == 0)
    def _():
        f = _rms(x_ref[...], gpre_ref[...]) * (1.0 + scale_ref[0]) + shift_ref[0]
        f_ref[...] = f.astype(BF16)
        o_ref[...] = jnp.zeros_like(o_ref)

    h = _dot(f_ref[...], w1_ref[...].astype(BF16))
    h_ref[...] = jnp.square(jnp.maximum(h, 0.0)).astype(BF16)
    nc = 512
    for n in range(D_MODEL // nc):
        cs = slice(n * nc, (n + 1) * nc)
        o_ref[:, cs] += _dot(h_ref[...], w2_ref[:, cs].astype(BF16))

    @pl.when(j == pl.num_programs(1) - 1)
    def _():
        o_ref[...] = x_ref[...] + gate_ref[0] * _rms(o_ref[...], gpost_ref[...])


def _ffn(x2d, shift, scale, gpre, w1, w2, gpost, gate, tm, mod_row):
    rows = x2d.shape[0]
    tf = 512
    mod = pl.BlockSpec((1, 1, D_MODEL), lambda i, j: (mod_row(i), 0, 0))
    vec = pl.BlockSpec((1, D_MODEL), lambda i, j: (0, 0))
    once = pl.Buffered(1)
    return pl.pallas_call(
        _ffn_kernel,
        grid=(rows // tm, D_FF // tf),
        in_specs=[pl.BlockSpec((tm, D_MODEL), lambda i, j: (i, 0), pipeline_mode=once), mod, mod, vec,
                  pl.BlockSpec((D_MODEL, tf), lambda i, j: (0, j)),
                  pl.BlockSpec((tf, D_MODEL), lambda i, j: (j, 0)),
                  vec, mod],
        out_specs=pl.BlockSpec((tm, D_MODEL), lambda i, j: (i, 0), pipeline_mode=once),
        out_shape=jax.ShapeDtypeStruct((rows, D_MODEL), F32),
        scratch_shapes=[pltpu.VMEM((tm, D_MODEL), BF16), pltpu.VMEM((tm, tf), BF16)],
        compiler_params=_cparams("arbitrary", "arbitrary"),
        name="ffn",
    )(x2d, shift, scale, gpre.reshape(1, -1), w1, w2, gpost.reshape(1, -1), gate)


def _rope_tables():
    t = jnp.arange(SEQ)
    row = (t // GRID_W).astype(F32)[:, None]
    col = (t % GRID_W).astype(F32)[:, None]

    def tables(rot_dim):
        quarter = rot_dim // 4
        inv_freq = ROPE_BASE ** (-jnp.arange(quarter, dtype=F32) / quarter)
        ar, ac = row * inv_freq, col * inv_freq
        cos = jnp.concatenate([jnp.cos(ar), jnp.cos(ar), jnp.cos(ac), jnp.cos(ac)], axis=1)
        sin = jnp.concatenate([-jnp.sin(ar), jnp.sin(ar), -jnp.sin(ac), jnp.sin(ac)], axis=1)
        pad = 128 - rot_dim
        if pad:
            cos = jnp.concatenate([cos, jnp.ones((SEQ, pad), F32)], axis=1)
            sin = jnp.concatenate([sin, jnp.zeros((SEQ, pad), F32)], axis=1)
        return cos, sin

    return tables(HEAD_DIM), tables(MLA_ROPE)


def kernel(x, c, ctx, c_ctx, ada_w, ada_b, norm_mix_pre, norm_mix_post, norm_ffn_pre, norm_ffn_post, w_in, w_out, s5_a_re, s5_a_im, s5_log_dt, s5_b_re, s5_b_im, s5_c_re, s5_c_im, s5_d, s5_glu_w, s5_glu_b, swa_sink, mla_q_norm, mla_w_uq, mla_kv_norm, mla_w_ukv, ffn_w1, ffn_w2):
    (cos_swa, sin_swa), (cos_mla, sin_mla) = _rope_tables()

    cvec = jnp.concatenate([c, c_ctx[None, :], jnp.zeros((3, D_MODEL), F32)], axis=0)
    mods = _ada(cvec, ada_w, ada_b)

    tm_p = 512
    tm_f = 1024

    def lat_row(tm):
        return lambda i: i // (SEQ // tm)

    ctx_row = lambda i: 4

    xl = x.reshape(BATCH * SEQ, D_MODEL)
    xc = ctx.reshape(BATCH * CTX_LEN, D_MODEL)

    for i in range(DEPTH):
        need_ctx = i < DEPTH - 1
        mod = [mods[i, :, k * D_MODEL:(k + 1) * D_MODEL].reshape(8, 1, D_MODEL) for k in range(6)]

        w_in_pad = jnp.pad(w_in[i], ((0, 0), (0, IN_PAD - IN_WIDTH))).astype(BF16)
        w_out_b = w_out[i].astype(BF16)
        glu_w_b = s5_glu_w[i].astype(BF16)
        w_uq_pad = jnp.pad(mla_w_uq[i].reshape(MLA_Q_RANK, MLA_HEADS, MLA_NOPE + MLA_ROPE),
                           ((0, 0), (0, 0), (0, MLA_QK_PAD - MLA_NOPE - MLA_ROPE))
                           ).reshape(MLA_Q_RANK, MLA_HEADS * MLA_QK_PAD).astype(BF16)
        w_ukv3 = mla_w_ukv[i].reshape(MLA_KV_RANK, MLA_HEADS, MLA_NOPE + MLA_V)
        w_ukv_perm = jnp.concatenate([w_ukv3[:, :, :MLA_NOPE].reshape(MLA_KV_RANK, -1),
                                      w_ukv3[:, :, MLA_NOPE:].reshape(MLA_KV_RANK, -1)], axis=1).astype(BF16)
        g_pre = norm_mix_pre[i].reshape(1, -1)

        z_l = _in_proj(xl, mod[0], mod[1], g_pre, w_in_pad, tm_p, lat_row(tm_p))
        z_c = _in_proj(xc, mod[0], mod[1], g_pre, w_in_pad, tm_p, ctx_row)

        s5_w = _s5_disc(s5_a_re[i], s5_a_im[i], s5_log_dt[i], s5_b_re[i], s5_b_im[i], s5_c_re[i], s5_c_im[i])
        s5_c, s5_l = _s5_mixer(z_c, z_l, s5_w, s5_d[i], glu_w_b, s5_glu_b[i], need_ctx)

        qk = _swa_rope(z_l, cos_swa, sin_swa)
        swa_l = _swa_latent(qk, z_l, z_c, swa_sink[i])

        q_l = _mla_q(z_l, mla_q_norm[i], w_uq_pad, cos_mla, sin_mla, 1024, True, SEQ)
        k_l, v_l = _mla_kv(z_l, mla_kv_norm[i], w_ukv_perm, cos_mla, sin_mla, 1024, True, SEQ)
        k_c, v_c = _mla_kv(z_c, mla_kv_norm[i], w_ukv_perm, cos_mla, sin_mla, 1024, False, 1024)
        mla_l = _mla_attn(q_l, [k_c, k_l], [v_c, v_l], SEQ, 256)

        a_l = jnp.concatenate([s5_l, swa_l, mla_l], axis=1)
        xl = _out_proj(a_l, w_out_b, xl, norm_mix_post[i], mod[2], tm_p, lat_row(tm_p))
        xl = _ffn(xl, mod[3], mod[4], norm_ffn_pre[i], ffn_w1[i], ffn_w2[i], norm_ffn_post[i], mod[5],
                  tm_f, lat_row(tm_f))

        if need_ctx:
            swa_c = _swa_context(z_c, swa_sink[i])
            q_c = _mla_q(z_c, mla_q_norm[i], w_uq_pad, cos_mla, sin_mla, 1024, False, 1024)
            mla_c = _mla_attn(q_c, [k_c], [v_c], CTX_LEN, 256)
            a_c = jnp.concatenate([s5_c, swa_c, mla_c], axis=1)
            xc = _out_proj(a_c, w_out_b, xc, norm_mix_post[i], mod[2], tm_p, ctx_row)
            xc = _ffn(xc, mod[3], mod[4], norm_ffn_pre[i], ffn_w1[i], ffn_w2[i], norm_ffn_post[i], mod[5],
                      tm_f, ctx_row)

    return xl.reshape(BATCH, SEQ, D_MODEL)
```

```python
import functools
import math

import jax
import jax.numpy as jnp
from jax import lax
from jax.experimental import pallas as pl
from jax.experimental.pallas import tpu as pltpu

F32 = jnp.float32
BF16 = jnp.bfloat16

D_MODEL = 2048
BATCH = 4
SEQ = 2048
DEPTH = 2
GRID_W = 64
CTX_LEN = 256
EPS = 1e-6
ROPE_BASE = 10000.0
NEG_INF = -1e30
BLOCK = 128
HEAD_DIM = 128
S5_WIDTH = 512
S5_GROUP = 16
S5_GROUPS = 32
S5_STATE = 64
S5_LANES = S5_GROUPS * S5_STATE
SWA_HEADS = 6
SWA_KV_HEADS = 2
SWA_REP = SWA_HEADS // SWA_KV_HEADS
MLA_HEADS = 6
MLA_Q_RANK = 768
MLA_KV_RANK = 512
MLA_NOPE = 128
MLA_ROPE = 64
MLA_V = 128
MLA_QK_PAD = 256
MLA_V_PAD = 256
D_FF = 4 * D_MODEL
IN_WIDTHS = (512, 768, 256, 256, 768, 512, 64)
IN_WIDTH = sum(IN_WIDTHS)
IN_PAD = 3584
OFF_U, OFF_QS, OFF_KS, OFF_VS, OFF_CQ, OFF_CKV, OFF_KR = 0, 512, 1280, 1536, 1792, 2560, 3072

VMEM_LIMIT = 56 * 1024 * 1024


def _cparams(*sem):
    return pltpu.CompilerParams(dimension_semantics=sem, vmem_limit_bytes=VMEM_LIMIT)


def _dot(a, b):
    return jnp.dot(a, b, preferred_element_type=F32)


def _dot_nt(a, b):
    return lax.dot_general(a, b, (((1,), (1,)), ((), ())), preferred_element_type=F32)


def _rms(x, g):
    return x * lax.rsqrt(jnp.mean(x * x, axis=-1, keepdims=True) + EPS) * g


def _rope(x, cos, sin, half):
    lane = lax.broadcasted_iota(jnp.int32, x.shape, 1)
    fwd = pltpu.roll(x, 128 - half, 1)
    bwd = pltpu.roll(x, half, 1)
    sw = jnp.where((lane % (2 * half)) < half, fwd, bwd)
    return x * cos + sw * sin


def _ada_kernel(c_ref, w_ref, b_ref, o_ref):
    c = c_ref[...]
    s = c * jax.nn.sigmoid(c)
    o_ref[0] = _dot(s.astype(BF16), w_ref[0].astype(BF16)) + b_ref[0]


def _ada(cvec, ada_w, ada_b):
    tn = 1024
    n = 6 * D_MODEL
    return pl.pallas_call(
        _ada_kernel,
        grid=(DEPTH, n // tn),
        in_specs=[pl.BlockSpec((8, D_MODEL), lambda l, j: (0, 0)),
                  pl.BlockSpec((1, D_MODEL, tn), lambda l, j: (l, 0, j)),
                  pl.BlockSpec((1, 1, tn), lambda l, j: (l, 0, j))],
        out_specs=pl.BlockSpec((1, 8, tn), lambda l, j: (l, 0, j)),
        out_shape=jax.ShapeDtypeStruct((DEPTH, 8, n), F32),
        compiler_params=_cparams("arbitrary", "arbitrary"),
        name="ada",
    )(cvec, ada_w, ada_b.reshape(DEPTH, 1, n))


def _in_proj_kernel(x_ref, shift_ref, scale_ref, g_ref, w_ref, o_ref, h_ref):
    h = _rms(x_ref[...], g_ref[...]) * (1.0 + scale_ref[0]) + shift_ref[0]
    h_ref[...] = h.astype(BF16)
    nc = 512
    for n in range(IN_PAD // nc):
        cs = slice(n * nc, (n + 1) * nc)
        o_ref[:, cs] = _dot(h_ref[...], w_ref[:, cs])


def _in_proj(x2d, shift, scale, g, w_pad, layer, tm, mod_row):
    rows = x2d.shape[0]
    return pl.pallas_call(
        _in_proj_kernel,
        grid=(rows // tm,),
        in_specs=[pl.BlockSpec((tm, D_MODEL), lambda i: (i, 0)),
                  pl.BlockSpec((1, 1, D_MODEL), lambda i: (mod_row(i), 0, 0)),
                  pl.BlockSpec((1, 1, D_MODEL), lambda i: (mod_row(i), 0, 0)),
                  pl.BlockSpec((1, D_MODEL), lambda i: (0, 0)),
                  pl.BlockSpec((None, D_MODEL, IN_PAD), lambda i: (layer, 0, 0), pipeline_mode=pl.Buffered(1))],
        out_specs=pl.BlockSpec((tm, IN_PAD), lambda i: (i, 0)),
        out_shape=jax.ShapeDtypeStruct((rows, IN_PAD), F32),
        scratch_shapes=[pltpu.VMEM((tm, D_MODEL), BF16)],
        compiler_params=_cparams("arbitrary"),
        name="in_proj",
    )(x2d, shift, scale, g, w_pad)


def _s5_disc_kernel(are_ref, aim_ref, ldt_ref, ebr_ref, ebi_ref, cr_ref, ci_ref,
                    lam_ref, wbr_ref, wbi_ref, wcr_ref, wci_ref):
    are = are_ref[0]
    aim = aim_ref[0]
    dt = jnp.exp(ldt_ref[0])
    mag = jnp.exp(are * dt)
    lr = mag * jnp.cos(aim * dt)
    li = mag * jnp.sin(aim * dt)
    den = are * are + aim * aim
    nr = lr - 1.0
    fr = (nr * are + li * aim) / den
    fi = (li * are - nr * aim) / den
    lam_ref[0, 0:1, :] = lr
    lam_ref[0, 1:2, :] = li
    half = S5_LANES // 2
    for h in range(2):
        frh = fr[:, h * half:(h + 1) * half]
        fih = fi[:, h * half:(h + 1) * half]
        wbr_ref[0, h] = (frh * ebr_ref[0, h] - fih * ebi_ref[0, h]).astype(BF16)
        wbi_ref[0, h] = (frh * ebi_ref[0, h] + fih * ebr_ref[0, h]).astype(BF16)
        wcr_ref[0, h] = cr_ref[0, h].astype(BF16)
        wci_ref[0, h] = ci_ref[0, h].astype(BF16)


def _s5_disc(a_re, a_im, log_dt, b_re, b_im, c_re, c_im):
    nd = DEPTH * 2
    gh = S5_GROUPS // 2
    eye = jnp.eye(gh, dtype=F32)
    are = a_re.reshape(nd, 1, S5_LANES)
    aim = a_im.reshape(nd, 1, S5_LANES)
    ldt = jnp.repeat(log_dt, S5_STATE, axis=-1).reshape(nd, 1, S5_LANES)

    def expand_b(b):
        b5 = b.reshape(nd, 2, gh, S5_STATE, S5_GROUP)
        return jnp.einsum('dhgnp,gk->dhgpkn', b5, eye).reshape(nd, 2, gh * S5_GROUP, gh * S5_STATE)

    def expand_c(c):
        c5 = c.reshape(nd, 2, gh, S5_GROUP, S5_STATE)
        return jnp.einsum('dhgpn,gk->dhgnkp', c5, eye).reshape(nd, 2, gh * S5_STATE, gh * S5_GROUP)

    kin, kst = gh * S5_GROUP, gh * S5_STATE
    vec = pl.BlockSpec((1, 1, S5_LANES), lambda d: (d, 0, 0))
    bspec = pl.BlockSpec((1, 2, kin, kst), lambda d: (d, 0, 0, 0))
    cspec = pl.BlockSpec((1, 2, kst, kin), lambda d: (d, 0, 0, 0))
    return pl.pallas_call(
        _s5_disc_kernel,
        grid=(nd,),
        in_specs=[vec, vec, vec, bspec, bspec, cspec, cspec],
        out_specs=[pl.BlockSpec((1, 2, S5_LANES), lambda d: (d, 0, 0)), bspec, bspec, cspec, cspec],
        out_shape=[jax.ShapeDtypeStruct((nd, 2, S5_LANES), F32),
                   jax.ShapeDtypeStruct((nd, 2, kin, kst), BF16),
                   jax.ShapeDtypeStruct((nd, 2, kin, kst), BF16),
                   jax.ShapeDtypeStruct((nd, 2, kst, kin), BF16),
                   jax.ShapeDtypeStruct((nd, 2, kst, kin), BF16)],
        compiler_params=_cparams("arbitrary"),
        name="s5_disc",
    )(are, aim, ldt, expand_b(b_re), expand_b(b_im), expand_c(c_re), expand_c(c_im))


S5_TC = 128
S5_CTX_CHUNKS = CTX_LEN // S5_TC
S5_LAT_CHUNKS = SEQ // S5_TC


def _s5_scan_kernel(ucf_ref, ulf_ref, ucb_ref, ulb_ref, lam_ref, wbr_ref, wbi_ref, wcr_ref, wci_ref,
                    yf_ref, yb_ref, fr_ref, fi_ref, gr_ref, gi_ref, sr_ref, si_ref, *, skip_chunks):
    c = pl.program_id(0)
    tc = S5_TC
    kin, kst = S5_WIDTH // 2, S5_LANES // 2

    @pl.when(c == 0)
    def _():
        sr_ref[...] = jnp.zeros_like(sr_ref)
        si_ref[...] = jnp.zeros_like(si_ref)

    nlb = kst // 128

    def project(u_ref, d, dst_r, dst_i):
        u = u_ref[...].reshape(BATCH * tc, S5_WIDTH).astype(BF16)
        for h in range(2):
            uh = u[:, h * kin:(h + 1) * kin]
            for w_ref, dst in ((wbr_ref, dst_r), (wbi_ref, dst_i)):
                p = _dot(uh, w_ref[d, h])
                for j in range(nlb):
                    for b in range(BATCH):
                        dst[h * nlb + j, pl.ds(b, tc, stride=BATCH), :] = (
                            p[b * tc:(b + 1) * tc, j * 128:(j + 1) * 128])

    @pl.when(c < S5_CTX_CHUNKS)
    def _():
        project(ucf_ref, 0, fr_ref, fi_ref)
        project(ucb_ref, 1, gr_ref, gi_ref)

    @pl.when(c >= S5_CTX_CHUNKS)
    def _():
        project(ulf_ref, 0, fr_ref, fi_ref)
        project(ulb_ref, 1, gr_ref, gi_ref)

    nb = 8
    for lb in range(2 * nlb // nb):
        ls = slice(lb * nb, (lb + 1) * nb)
        fwd_rows = lax.broadcasted_iota(jnp.int32, (nb, 8, 128), 1) < BATCH
        la_r = jnp.where(fwd_rows, lam_ref[0, 0, ls], lam_ref[1, 0, ls])
        la_i = jnp.where(fwd_rows, lam_ref[0, 1, ls], lam_ref[1, 1, ls])
        lb_r = jnp.where(fwd_rows, lam_ref[1, 0, ls], lam_ref[0, 0, ls])
        lb_i = jnp.where(fwd_rows, lam_ref[1, 1, ls], lam_ref[0, 1, ls])

        def body(k, carry, ls=ls, fwd_rows=fwd_rows, la_r=la_r, la_i=la_i, lb_r=lb_r, lb_i=lb_i):
            s_r, s_i = carry
            rf = pl.ds(pl.multiple_of(k * 8, 8), 8)
            rb = pl.ds(pl.multiple_of((tc // 2 - 1 - k) * 8, 8), 8)
            f_r, f_i = fr_ref[ls, rf, :], fi_ref[ls, rf, :]
            g_r, g_i = gr_ref[ls, rb, :], gi_ref[ls, rb, :]
            a_r = la_r * s_r - la_i * s_i + jnp.where(fwd_rows, f_r, g_r)
            a_i = la_r * s_i + la_i * s_r + jnp.where(fwd_rows, f_i, g_i)
            t_r = pltpu.roll(a_r, BATCH, 1)
            t_i = pltpu.roll(a_i, BATCH, 1)
            b_r = lb_r * t_r - lb_i * t_i + jnp.where(fwd_rows, g_r, f_r)
            b_i = lb_r * t_i + lb_i * t_r + jnp.where(fwd_rows, g_i, f_i)
            fr_ref[ls, rf, :] = jnp.where(fwd_rows, a_r, b_r)
            fi_ref[ls, rf, :] = jnp.where(fwd_rows, a_i, b_i)
            gr_ref[ls, rb, :] = jnp.where(fwd_rows, b_r, a_r)
            gi_ref[ls, rb, :] = jnp.where(fwd_rows, b_i, a_i)
            return pltpu.roll(b_r, BATCH, 1), pltpu.roll(b_i, BATCH, 1)

        s_r, s_i = lax.fori_loop(0, tc // 2, body, (sr_ref[ls], si_ref[ls]))
        sr_ref[ls] = s_r
        si_ref[ls] = s_i

    def gather_rows(src, h):
        return jnp.concatenate(
            [jnp.concatenate([src[h * nlb + j, pl.ds(b, tc, stride=BATCH), :] for j in range(nlb)], axis=1)
             for b in range(BATCH)], axis=0)

    def readout(d, src_r, src_i, y_ref):
        for h in range(2):
            hr = gather_rows(src_r, h).astype(BF16)
            hi = gather_rows(src_i, h).astype(BF16)
            y = _dot(hr, wcr_ref[d, h]) - _dot(hi, wci_ref[d, h])
            y_ref[:, :, h * kin:(h + 1) * kin] = y.reshape(BATCH, tc, kin)

    @pl.when(c >= skip_chunks)
    def _():
        readout(0, fr_ref, fi_ref, yf_ref)
        readout(1, gr_ref, gi_ref, yb_ref)


def _s5_scan(z_c, z_l, lam, wbr, wbi, wcr, wci, layer, skip_chunks):
    tc = S5_TC
    nchunk = S5_CTX_CHUNKS + S5_LAT_CHUNKS
    kin, kst = S5_WIDTH // 2, S5_LANES // 2
    zc3 = z_c.reshape(BATCH, CTX_LEN, IN_PAD)
    zl3 = z_l.reshape(BATCH, SEQ, IN_PAD)

    def bwd_chunk(c):
        return jnp.where(c < S5_CTX_CHUNKS, S5_CTX_CHUNKS - 1 - c, nchunk + S5_CTX_CHUNKS - 1 - c)

    def ctx_spec(chunk_of):
        return pl.BlockSpec((BATCH, tc, S5_WIDTH),
                            lambda c: (0, jnp.clip(chunk_of(c), 0, S5_CTX_CHUNKS - 1), 0))

    def lat_spec(chunk_of):
        return pl.BlockSpec((BATCH, tc, S5_WIDTH),
                            lambda c: (0, jnp.clip(chunk_of(c) - S5_CTX_CHUNKS, 0, S5_LAT_CHUNKS - 1), 0))

    def out_spec(chunk_of):
        return pl.BlockSpec((BATCH, tc, S5_WIDTH),
                            lambda c: (0, chunk_of(jnp.maximum(c, skip_chunks)) - skip_chunks, 0))

    once = pl.Buffered(1)
    bspec = pl.BlockSpec((2, 2, kin, kst), lambda c: (layer, 0, 0, 0), pipeline_mode=once)
    cspec = pl.BlockSpec((2, 2, kst, kin), lambda c: (layer, 0, 0, 0), pipeline_mode=once)
    t_out = (nchunk - skip_chunks) * tc
    nblk = S5_LANES // 128
    big = pltpu.VMEM((nblk, tc * BATCH, 128), F32)
    state = pltpu.VMEM((nblk, 8, 128), F32)
    lam = lam.reshape(DEPTH * 2, 2, nblk, 1, 128)
    fwd_chunk = lambda c: c
    return pl.pallas_call(
        functools.partial(_s5_scan_kernel, skip_chunks=skip_chunks),
        grid=(nchunk,),
        in_specs=[ctx_spec(fwd_chunk), lat_spec(fwd_chunk), ctx_spec(bwd_chunk), lat_spec(bwd_chunk),
                  pl.BlockSpec((2, 2, nblk, 1, 128), lambda c: (layer, 0, 0, 0, 0)),
                  bspec, bspec, cspec, cspec],
        out_specs=[out_spec(fwd_chunk), out_spec(bwd_chunk)],
        out_shape=[jax.ShapeDtypeStruct((BATCH, t_out, S5_WIDTH), F32)] * 2,
        scratch_shapes=[big, big, big, big, state, state],
        compiler_params=_cparams("arbitrary"),
        name="s5_scan",
    )(zc3, zl3, zc3, zl3, lam, wbr, wbi, wcr, wci)


def _s5_glu_kernel(u_ref, yf_ref, yb_ref, d_ref, w_ref, b_ref, o_ref):
    y = d_ref[...] * u_ref[0] + yf_ref[0] + yb_ref[0]
    g = jax.nn.gelu(y)
    gate = jax.nn.sigmoid(_dot(g.astype(BF16), w_ref[...]) + b_ref[...])
    o_ref[0] = (g * gate).astype(o_ref.dtype)


def _s5_glu(z3, yf, yb, d, w, b, layer, y_off):
    tm = 256
    n = z3.shape[1]
    off = y_off // tm
    zspec = pl.BlockSpec((1, tm, S5_WIDTH), lambda bi, i: (bi, i, 0))
    yspec = pl.BlockSpec((1, tm, S5_WIDTH), lambda bi, i: (bi, i + off, 0))
    vec = pl.BlockSpec((1, S5_WIDTH), lambda bi, i: (0, 0))
    return pl.pallas_call(
        _s5_glu_kernel,
        grid=(BATCH, n // tm),
        in_specs=[zspec, yspec, yspec, vec,
                  pl.BlockSpec((None, S5_WIDTH, S5_WIDTH), lambda bi, i: (layer, 0, 0)), vec],
        out_specs=zspec,
        out_shape=jax.ShapeDtypeStruct((BATCH, n, S5_WIDTH), BF16),
        compiler_params=_cparams("arbitrary", "arbitrary"),
        name="s5_glu",
    )(z3, yf, yb, d.reshape(1, -1), w, b.reshape(1, -1)).reshape(BATCH * n, S5_WIDTH)


def _s5_mixer(z_c, z_l, s5_w, d, glu_w, glu_b, layer, need_ctx):
    lam, wbr, wbi, wcr, wci = s5_w
    skip = 0 if need_ctx else S5_CTX_CHUNKS
    yf, yb = _s5_scan(z_c, z_l, lam, wbr, wbi, wcr, wci, layer, skip)
    nctx = CTX_LEN if need_ctx else 0
    s5_l = _s5_glu(z_l.reshape(BATCH, SEQ, IN_PAD), yf, yb, d, glu_w, glu_b, layer, nctx)
    s5_c = None
    if need_ctx:
        s5_c = _s5_glu(z_c.reshape(BATCH, CTX_LEN, IN_PAD), yf, yb, d, glu_w, glu_b, layer, 0)
    return s5_c, s5_l


def _swa_rope_kernel(za_ref, zb_ref, cos_ref, sin_ref, o_ref):
    cos = cos_ref[...]
    sin = sin_ref[...]
    scale = HEAD_DIM ** -0.5
    per = (SWA_HEADS + SWA_KV_HEADS) // 2
    for h in range(SWA_HEADS + SWA_KV_HEADS):
        src = za_ref if h < per else zb_ref
        lo = (h % per) * HEAD_DIM
        x = _rope(src[:, lo:lo + HEAD_DIM], cos, sin, HEAD_DIM // 4)
        if h < SWA_HEADS:
            x = x * scale
        o_ref[:, h * HEAD_DIM:(h + 1) * HEAD_DIM] = x.astype(BF16)


def _swa_rope(z_l, cos, sin):
    tm = 512
    w = (SWA_HEADS + SWA_KV_HEADS) * HEAD_DIM
    c0 = OFF_QS // (w // 2)
    tab = pl.BlockSpec((tm, HEAD_DIM), lambda i: (i % (SEQ // tm), 0))
    return pl.pallas_call(
        _swa_rope_kernel,
        grid=(BATCH * SEQ // tm,),
        in_specs=[pl.BlockSpec((tm, w // 2), lambda i: (i, c0)),
                  pl.BlockSpec((tm, w // 2), lambda i: (i, c0 + 1)),
                  tab, tab],
        out_specs=pl.BlockSpec((tm, w), lambda i: (i, 0)),
        out_shape=jax.ShapeDtypeStruct((BATCH * SEQ, w), BF16),
        compiler_params=_cparams("arbitrary"),
        name="swa_rope",
    )(z_l, z_l, cos, sin)


def _sink_rows(sink_ref, g, rows):
    r = lax.broadcasted_iota(jnp.int32, (rows, 1), 0) // (rows // SWA_REP)
    s0, s1, s2 = sink_ref[SWA_REP * g], sink_ref[SWA_REP * g + 1], sink_ref[SWA_REP * g + 2]
    return jnp.where(r == 0, s0, jnp.where(r == 1, s1, s2))


def _stack_heads(x, n):
    return jnp.concatenate([x[:, h * HEAD_DIM:(h + 1) * HEAD_DIM] for h in range(n)], axis=0)


def _unstack_heads(x, n):
    rows = x.shape[0] // n
    return jnp.concatenate([x[h * rows:(h + 1) * rows] for h in range(n)], axis=1)


SWA_QB = 4


def _with_ones(v):
    return jnp.concatenate([v, jnp.ones_like(v)], axis=1)


def _swa_latent_kernel(sink_ref, q_ref, kp_ref, km_ref, kn_ref, vp_ref, vm_ref, vn_ref,
                       kx_ref, vx_ref, o_ref):
    n = pl.program_id(1)
    rows = SWA_REP * BLOCK
    qi = lax.broadcasted_iota(jnp.int32, (rows, 3 * BLOCK), 0) % BLOCK
    kj = lax.broadcasted_iota(jnp.int32, (rows, 3 * BLOCK), 1)
    in_window = (kj >= qi) & (kj <= qi + 2 * BLOCK)
    for g in range(SWA_KV_HEADS):
        gs = slice(g * HEAD_DIM, (g + 1) * HEAD_DIM)
        kspan = jnp.concatenate([kp_ref[:, gs], km_ref[:, gs], kn_ref[:, gs]], axis=0)
        vspan = _with_ones(jnp.concatenate([vp_ref[:, gs], vm_ref[:, gs], vn_ref[:, gs]],
                                           axis=0).astype(BF16))
        kx = kx_ref[:, gs].astype(BF16)
        vx = _with_ones(vx_ref[:, gs].astype(BF16))
        sk = _sink_rows(sink_ref, g, rows)
        for j in range(SWA_QB):
            blk = n * SWA_QB + j
            qs = slice(g * SWA_REP * HEAD_DIM, (g + 1) * SWA_REP * HEAD_DIM)
            q = _stack_heads(q_ref[j * BLOCK:(j + 1) * BLOCK, qs], SWA_REP)
            kb = kspan[j * BLOCK:(j + 3) * BLOCK]
            vb = vspan[j * BLOCK:(j + 3) * BLOCK]
            s_ctx = _dot_nt(q, kx)
            s_band = _dot_nt(q, kb)
            kpos = (blk - 1) * BLOCK + kj
            valid = in_window & (kpos >= 0) & (kpos < SEQ)
            s_band = jnp.where(valid, s_band, NEG_INF)
            m = jnp.maximum(jnp.maximum(jnp.max(s_ctx, axis=-1, keepdims=True),
                                        jnp.max(s_band, axis=-1, keepdims=True)), sk)
            p_ctx = jnp.exp(s_ctx - m).astype(BF16)
            p_band = jnp.exp(s_band - m).astype(BF16)
            o = _dot(p_ctx, vx) + _dot(p_band, vb)
            den = o[:, HEAD_DIM:] + jnp.exp(sk - m)
            o_ref[j * BLOCK:(j + 1) * BLOCK, qs] = _unstack_heads(
                o[:, :HEAD_DIM] / den, SWA_REP).astype(o_ref.dtype)


def _swa_latent(qk, z_l, z_c, sink):
    nb = SEQ // BLOCK
    ng = nb // SWA_QB
    span = SWA_QB * BLOCK
    kvw = SWA_KV_HEADS * HEAD_DIM
    kcol = SWA_HEADS * HEAD_DIM // kvw
    vcol = OFF_VS // kvw
    kxcol = OFF_KS // kvw

    def edge(col, blk_of):
        return pl.BlockSpec((BLOCK, kvw), lambda b, n, s: (b * nb + jnp.clip(blk_of(n), 0, nb - 1), col))

    def main(col):
        return pl.BlockSpec((span, kvw), lambda b, n, s: (b * ng + n, col))

    prev_blk = lambda n: n * SWA_QB - 1
    next_blk = lambda n: (n + 1) * SWA_QB
    grid_spec = pltpu.PrefetchScalarGridSpec(
        num_scalar_prefetch=1,
        grid=(BATCH, ng),
        in_specs=[pl.BlockSpec((span, SWA_HEADS * HEAD_DIM), lambda b, n, s: (b * ng + n, 0)),
                  edge(kcol, prev_blk), main(kcol), edge(kcol, next_blk),
                  edge(vcol, prev_blk), main(vcol), edge(vcol, next_blk),
                  pl.BlockSpec((CTX_LEN, kvw), lambda b, n, s: (b, kxcol)),
                  pl.BlockSpec((CTX_LEN, kvw), lambda b, n, s: (b, vcol))],
        out_specs=pl.BlockSpec((span, SWA_HEADS * HEAD_DIM), lambda b, n, s: (b * ng + n, 0)),
    )
    return pl.pallas_call(
        _swa_latent_kernel,
        grid_spec=grid_spec,
        out_shape=jax.ShapeDtypeStruct((BATCH * SEQ, SWA_HEADS * HEAD_DIM), BF16),
        compiler_params=_cparams("arbitrary", "arbitrary"),
        name="swa_latent",
    )(sink, qk, qk, qk, qk, z_l, z_l, z_l, z_c, z_c)


def _swa_context_kernel(sink_ref, q_ref, k_ref, v_ref, o_ref):
    g = pl.program_id(1)
    q = (_stack_heads(q_ref[...], SWA_REP) * (HEAD_DIM ** -0.5)).astype(BF16)
    s = _dot_nt(q, k_ref[...].astype(BF16))
    sk = _sink_rows(sink_ref, g, SWA_REP * CTX_LEN)
    m = jnp.maximum(jnp.max(s, axis=-1, keepdims=True), sk)
    p = jnp.exp(s - m)
    den = jnp.sum(p, axis=-1, keepdims=True) + jnp.exp(sk - m)
    o = _dot(p.astype(BF16), v_ref[...].astype(BF16)) / den
    o_ref[...] = _unstack_heads(o, SWA_REP).astype(o_ref.dtype)


def _swa_context(z_c, sink):
    qw = SWA_REP * HEAD_DIM
    grid_spec = pltpu.PrefetchScalarGridSpec(
        num_scalar_prefetch=1,
        grid=(BATCH, SWA_KV_HEADS),
        in_specs=[pl.BlockSpec((CTX_LEN, qw), lambda b, g, s: (b, g)),
                  pl.BlockSpec((CTX_LEN, HEAD_DIM), lambda b, g, s: (b, OFF_KS // HEAD_DIM + g)),
                  pl.BlockSpec((CTX_LEN, HEAD_DIM), lambda b, g, s: (b, OFF_VS // HEAD_DIM + g))],
        out_specs=pl.BlockSpec((CTX_LEN, qw), lambda b, g, s: (b, g)),
    )
    return pl.pallas_call(
        _swa_context_kernel,
        grid_spec=grid_spec,
        out_shape=jax.ShapeDtypeStruct((BATCH * CTX_LEN, SWA_HEADS * HEAD_DIM), BF16),
        compiler_params=_cparams("arbitrary", "arbitrary"),
        name="swa_context",
    )(sink, z_c[:, OFF_QS:OFF_QS + SWA_HEADS * HEAD_DIM], z_c, z_c)


def _mla_q_kernel(c0_ref, c1_ref, c2_ref, g_ref, w_ref, cos_ref, sin_ref, o_ref, *, rope):
    cq = jnp.concatenate([c0_ref[...], c1_ref[...], c2_ref[...]], axis=1)
    q = _dot(_rms(cq, g_ref[...]).astype(BF16), w_ref[...])
    scale = (MLA_NOPE + MLA_ROPE) ** -0.5
    for h in range(MLA_HEADS):
        lo = h * MLA_QK_PAD
        o_ref[:, lo:lo + MLA_NOPE] = (q[:, lo:lo + MLA_NOPE] * scale).astype(BF16)
        r = q[:, lo + MLA_NOPE:lo + MLA_QK_PAD]
        if rope:
            r = _rope(r, cos_ref[...], sin_ref[...], MLA_ROPE // 4)
        o_ref[:, lo + MLA_NOPE:lo + MLA_QK_PAD] = (r * scale).astype(BF16)


def _mla_q(z, g, w_pad, layer, cos, sin, tm, rope, seq):
    rows = z.shape[0]
    c0 = OFF_CQ // 256
    tab = pl.BlockSpec((tm, 128), lambda i: (i % (seq // tm), 0))
    return pl.pallas_call(
        functools.partial(_mla_q_kernel, rope=rope),
        grid=(rows // tm,),
        in_specs=[pl.BlockSpec((tm, 256), lambda i: (i, c0)),
                  pl.BlockSpec((tm, 256), lambda i: (i, c0 + 1)),
                  pl.BlockSpec((tm, 256), lambda i: (i, c0 + 2)),
                  pl.BlockSpec((1, MLA_Q_RANK), lambda i: (0, 0)),
                  pl.BlockSpec((None, MLA_Q_RANK, MLA_HEADS * MLA_QK_PAD), lambda i: (layer, 0, 0)),
                  tab, tab],
        out_specs=pl.BlockSpec((tm, MLA_HEADS * MLA_QK_PAD), lambda i: (i, 0)),
        out_shape=jax.ShapeDtypeStruct((rows, MLA_HEADS * MLA_QK_PAD), BF16),
        compiler_params=_cparams("arbitrary"),
        name="mla_q",
    )(z, z, z, g.reshape(1, -1), w_pad, cos, sin)


def _mla_kv_kernel(ckv_ref, kr_ref, g_ref, w_ref, cos_ref, sin_ref, k_ref, v_ref, *, rope):
    kv = _dot(_rms(ckv_ref[...], g_ref[...]).astype(BF16), w_ref[...])
    kr = kr_ref[...]
    if rope:
        kr = _rope(kr, cos_ref[...], sin_ref[...], MLA_ROPE // 4)
    kr = kr.astype(BF16)
    for h in range(MLA_HEADS):
        lo = h * MLA_QK_PAD
        k_ref[:, lo:lo + MLA_NOPE] = kv[:, h * MLA_NOPE:(h + 1) * MLA_NOPE].astype(BF16)
        k_ref[:, lo + MLA_NOPE:lo + MLA_QK_PAD] = kr
        vlo = h * MLA_V_PAD
        vh = kv[:, MLA_HEADS * MLA_NOPE + h * MLA_V:MLA_HEADS * MLA_NOPE + (h + 1) * MLA_V].astype(BF16)
        v_ref[:, vlo:vlo + MLA_V] = vh
        v_ref[:, vlo + MLA_V:vlo + MLA_V_PAD] = jnp.ones_like(vh)


def _mla_kv(z, g, w_perm, layer, cos, sin, tm, rope, seq):
    rows = z.shape[0]
    tab = pl.BlockSpec((tm, 128), lambda i: (i % (seq // tm), 0))
    return pl.pallas_call(
        functools.partial(_mla_kv_kernel, rope=rope),
        grid=(rows // tm,),
        in_specs=[pl.BlockSpec((tm, MLA_KV_RANK), lambda i: (i, OFF_CKV // MLA_KV_RANK)),
                  pl.BlockSpec((tm, 128), lambda i: (i, OFF_KR // 128)),
                  pl.BlockSpec((1, MLA_KV_RANK), lambda i: (0, 0)),
                  pl.BlockSpec((None, MLA_KV_RANK, MLA_HEADS * (MLA_NOPE + MLA_V)), lambda i: (layer, 0, 0)),
                  tab, tab],
        out_specs=[pl.BlockSpec((tm, MLA_HEADS * MLA_QK_PAD), lambda i: (i, 0)),
                   pl.BlockSpec((tm, MLA_HEADS * MLA_V_PAD), lambda i: (i, 0))],
        out_shape=[jax.ShapeDtypeStruct((rows, MLA_HEADS * MLA_QK_PAD), BF16),
                   jax.ShapeDtypeStruct((rows, MLA_HEADS * MLA_V_PAD), BF16)],
        compiler_params=_cparams("arbitrary"),
        name="mla_kv",
    )(z, z, g.reshape(1, -1), w_perm, cos, sin)


def _mla_attn_kernel(*refs, nseg):
    q_ref = refs[0]
    k_refs = refs[1:1 + nseg]
    v_refs = refs[1 + nseg:1 + 2 * nseg]
    o_ref = refs[1 + 2 * nseg]
    for h in range(MLA_HEADS):
        q = q_ref[:, h * MLA_QK_PAD:(h + 1) * MLA_QK_PAD]
        s = [_dot_nt(q, k[:, h * MLA_QK_PAD:(h + 1) * MLA_QK_PAD]) for k in k_refs]
        m = functools.reduce(jnp.maximum, [jnp.max(x, axis=-1, keepdims=True) for x in s])
        p = [jnp.exp(x - m).astype(BF16) for x in s]
        o = sum(_dot(x, v[:, h * MLA_V_PAD:(h + 1) * MLA_V_PAD]) for x, v in zip(p, v_refs))
        o_ref[:, h * MLA_V:(h + 1) * MLA_V] = (o[:, :MLA_V] / o[:, MLA_V:]).astype(o_ref.dtype)


def _mla_attn(q, ks, vs, n_q, tq):
    nseg = len(ks)
    nq = n_q // tq
    lens = [k.shape[0] // BATCH for k in ks]
    kw, vw = MLA_HEADS * MLA_QK_PAD, MLA_HEADS * MLA_V
    in_specs = [pl.BlockSpec((tq, kw), lambda b, i: (b * nq + i, 0))]
    in_specs += [pl.BlockSpec((n, kw), lambda b, i: (b, 0)) for n in lens]
    in_specs += [pl.BlockSpec((n, MLA_HEADS * MLA_V_PAD), lambda b, i: (b, 0)) for n in lens]
    return pl.pallas_call(
        functools.partial(_mla_attn_kernel, nseg=nseg),
        grid=(BATCH, nq),
        in_specs=in_specs,
        out_specs=pl.BlockSpec((tq, vw), lambda b, i: (b * nq + i, 0)),
        out_shape=jax.ShapeDtypeStruct((BATCH * n_q, vw), BF16),
        compiler_params=_cparams("arbitrary", "arbitrary"),
        name="mla_attn",
    )(q, *ks, *vs)


def _out_proj_kernel(a_ref, w_ref, x_ref, g_ref, gate_ref, o_ref):
    nc = 512
    for n in range(D_MODEL // nc):
        cs = slice(n * nc, (n + 1) * nc)
        o_ref[:, cs] = _dot(a_ref[...], w_ref[:, cs])
    o_ref[...] = x_ref[...] + gate_ref[0] * _rms(o_ref[...], g_ref[...])


def _out_proj(a, w, x2d, g, gate, layer, tm, mod_row):
    rows = a.shape[0]
    return pl.pallas_call(
        _out_proj_kernel,
        grid=(rows // tm,),
        in_specs=[pl.BlockSpec((tm, D_MODEL), lambda i: (i, 0)),
                  pl.BlockSpec((None, D_MODEL, D_MODEL), lambda i: (layer, 0, 0), pipeline_mode=pl.Buffered(1)),
                  pl.BlockSpec((tm, D_MODEL), lambda i: (i, 0)),
                  pl.BlockSpec((1, D_MODEL), lambda i: (0, 0)),
                  pl.BlockSpec((1, 1, D_MODEL), lambda i: (mod_row(i), 0, 0))],
        out_specs=pl.BlockSpec((tm, D_MODEL), lambda i: (i, 0)),
        out_shape=jax.ShapeDtypeStruct((rows, D_MODEL), F32),
        compiler_params=_cparams("arbitrary"),
        name="out_proj",
    )(a, w, x2d, g.reshape(1, -1), gate)


def _ffn_kernel(x_ref, shift_ref, scale_ref, gpre_ref, w1_ref, w2_ref, gpost_ref, gate_ref, o_ref,
                f_ref, h_ref):
    j = pl.program_id(1)

    @pl.when(j == 0)
    def _():
        f = _rms(x_ref[...], gpre_ref[...]) * (1.0 + scale_ref[0]) + shift_ref[0]
        f_ref[...] = f.astype(BF16)
        o_ref[...] = jnp.zeros_like(o_ref)

    h = _dot(f_ref[...], w1_ref[...].astype(BF16))
    h_ref[...] = jnp.square(jnp.maximum(h, 0.0)).astype(BF16)
    nc = 512
    for n in range(D_MODEL // nc):
        cs = slice(n * nc, (n + 1) * nc)
        o_ref[:, cs] += _dot(h_ref[...], w2_ref[:, cs].astype(BF16))

    @pl.when(j == pl.num_programs(1) - 1)
    def _():
        o_ref[...] = x_ref[...] + gate_ref[0] * _rms(o_ref[...], gpost_ref[...])


def _ffn(x2d, shift, scale, gpre, w1, w2, gpost, gate, layer, tm, mod_row):
    rows = x2d.shape[0]
    tf = 512
    mod = pl.BlockSpec((1, 1, D_MODEL), lambda i, j: (mod_row(i), 0, 0))
    vec = pl.BlockSpec((1, D_MODEL), lambda i, j: (0, 0))
    once = pl.Buffered(1)
    return pl.pallas_call(
        _ffn_kernel,
        grid=(rows // tm, D_FF // tf),
        in_specs=[pl.BlockSpec((tm, D_MODEL), lambda i, j: (i, 0)), mod, mod, vec,
                  pl.BlockSpec((None, D_MODEL, tf), lambda i, j: (layer, 0, j)),
                  pl.BlockSpec((None, tf, D_MODEL), lambda i, j: (layer, j, 0)),
                  vec, mod],
        out_specs=pl.BlockSpec((tm, D_MODEL), lambda i, j: (i, 0), pipeline_mode=once),
        out_shape=jax.ShapeDtypeStruct((rows, D_MODEL), F32),
        scratch_shapes=[pltpu.VMEM((tm, D_MODEL), BF16), pltpu.VMEM((tm, tf), BF16)],
        compiler_params=_cparams("arbitrary", "arbitrary"),
        name="ffn",
    )(x2d, shift, scale, gpre.reshape(1, -1), w1, w2, gpost.reshape(1, -1), gate)


def _rope_tables():
    t = jnp.arange(SEQ)
    row = (t // GRID_W).astype(F32)[:, None]
    col = (t % GRID_W).astype(F32)[:, None]

    def tables(rot_dim):
        quarter = rot_dim // 4
        inv_freq = ROPE_BASE ** (-jnp.arange(quarter, dtype=F32) / quarter)
        ar, ac = row * inv_freq, col * inv_freq
        cos = jnp.concatenate([jnp.cos(ar), jnp.cos(ar), jnp.cos(ac), jnp.cos(ac)], axis=1)
        sin = jnp.concatenate([-jnp.sin(ar), jnp.sin(ar), -jnp.sin(ac), jnp.sin(ac)], axis=1)
        pad = 128 - rot_dim
        if pad:
            cos = jnp.concatenate([cos, jnp.ones((SEQ, pad), F32)], axis=1)
            sin = jnp.concatenate([sin, jnp.zeros((SEQ, pad), F32)], axis=1)
        return cos, sin

    return tables(HEAD_DIM), tables(MLA_ROPE)


def kernel(x, c, ctx, c_ctx, ada_w, ada_b, norm_mix_pre, norm_mix_post, norm_ffn_pre, norm_ffn_post, w_in, w_out, s5_a_re, s5_a_im, s5_log_dt, s5_b_re, s5_b_im, s5_c_re, s5_c_im, s5_d, s5_glu_w, s5_glu_b, swa_sink, mla_q_norm, mla_w_uq, mla_kv_norm, mla_w_ukv, ffn_w1, ffn_w2):
    (cos_swa, sin_swa), (cos_mla, sin_mla) = _rope_tables()

    cvec = jnp.concatenate([c, c_ctx[None, :], jnp.zeros((3, D_MODEL), F32)], axis=0)
    mods = _ada(cvec, ada_w, ada_b)

    tm_p = 512
    tm_f = 1024

    def lat_row(tm):
        return lambda i: i // (SEQ // tm)

    ctx_row = lambda i: 4

    xl = x.reshape(BATCH * SEQ, D_MODEL)
    xc = ctx.reshape(BATCH * CTX_LEN, D_MODEL)

    w_in_pad = jnp.pad(w_in, ((0, 0), (0, 0), (0, IN_PAD - IN_WIDTH))).astype(BF16)
    w_out_b = w_out.astype(BF16)
    glu_w_b = s5_glu_w.astype(BF16)
    w_uq_pad = jnp.pad(mla_w_uq.reshape(DEPTH, MLA_Q_RANK, MLA_HEADS, MLA_NOPE + MLA_ROPE),
                       ((0, 0), (0, 0), (0, 0), (0, MLA_QK_PAD - MLA_NOPE - MLA_ROPE))
                       ).reshape(DEPTH, MLA_Q_RANK, MLA_HEADS * MLA_QK_PAD).astype(BF16)
    w_ukv4 = mla_w_ukv.reshape(DEPTH, MLA_KV_RANK, MLA_HEADS, MLA_NOPE + MLA_V)
    w_ukv_perm = jnp.concatenate([w_ukv4[..., :MLA_NOPE].reshape(DEPTH, MLA_KV_RANK, -1),
                                  w_ukv4[..., MLA_NOPE:].reshape(DEPTH, MLA_KV_RANK, -1)], axis=2).astype(BF16)
    s5_w = _s5_disc(s5_a_re, s5_a_im, s5_log_dt, s5_b_re, s5_b_im, s5_c_re, s5_c_im)

    for i in range(DEPTH):
        need_ctx = i < DEPTH - 1
        mod = [mods[i, :, k * D_MODEL:(k + 1) * D_MODEL].reshape(8, 1, D_MODEL) for k in range(6)]
        g_pre = norm_mix_pre[i].reshape(1, -1)

        z_l = _in_proj(xl, mod[0], mod[1], g_pre, w_in_pad, i, tm_p, lat_row(tm_p))
        z_c = _in_proj(xc, mod[0], mod[1], g_pre, w_in_pad, i, tm_p, ctx_row)

        s5_c, s5_l = _s5_mixer(z_c, z_l, s5_w, s5_d[i], glu_w_b, s5_glu_b[i], i, need_ctx)

        qk = _swa_rope(z_l, cos_swa, sin_swa)
        swa_l = _swa_latent(qk, z_l, z_c, swa_sink[i])

        q_l = _mla_q(z_l, mla_q_norm[i], w_uq_pad, i, cos_mla, sin_mla, 1024, True, SEQ)
        k_l, v_l = _mla_kv(z_l, mla_kv_norm[i], w_ukv_perm, i, cos_mla, sin_mla, 1024, True, SEQ)
        k_c, v_c = _mla_kv(z_c, mla_kv_norm[i], w_ukv_perm, i, cos_mla, sin_mla, 1024, False, 1024)
        mla_l = _mla_attn(q_l, [k_c, k_l], [v_c, v_l], SEQ, 256)

        a_l = jnp.concatenate([s5_l, swa_l, mla_l], axis=1)
        xl = _out_proj(a_l, w_out_b, xl, norm_mix_post[i], mod[2], i, tm_p, lat_row(tm_p))
        xl = _ffn(xl, mod[3], mod[4], norm_ffn_pre[i], ffn_w1, ffn_w2, norm_ffn_post[i], mod[5],
                  i, tm_f, lat_row(tm_f))

        if need_ctx:
            swa_c = _swa_context(z_c, swa_sink[i])
            q_c = _mla_q(z_c, mla_q_norm[i], w_uq_pad, i, cos_mla, sin_mla, 1024, False, 1024)
            mla_c = _mla_attn(q_c, [k_c], [v_c], CTX_LEN, 256)
            a_c = jnp.concatenate([s5_c, swa_c, mla_c], axis=1)
            xc = _out_proj(a_c, w_out_b, xc, norm_mix_post[i], mod[2], i, tm_p, ctx_row)
            xc = _ffn(xc, mod[3], mod[4], norm_ffn_pre[i], ffn_w1, ffn_w2, norm_ffn_post[i], mod[5],
                      i, tm_f, ctx_row)

    return xl.reshape(BATCH, SEQ, D_MODEL)
```

```python
import functools
import math

import jax
import jax.numpy as jnp
from jax import lax
from jax.experimental import pallas as pl
from jax.experimental.pallas import tpu as pltpu

F32 = jnp.float32
BF16 = jnp.bfloat16

D_MODEL = 2048
BATCH = 4
SEQ = 2048
DEPTH = 2
GRID_W = 64
CTX_LEN = 256
EPS = 1e-6
ROPE_BASE = 10000.0
NEG_INF = -1e30
BLOCK = 128
HEAD_DIM = 128
S5_WIDTH = 512
S5_GROUP = 16
S5_GROUPS = 32
S5_STATE = 64
S5_LANES = S5_GROUPS * S5_STATE
SWA_HEADS = 6
SWA_KV_HEADS = 2
SWA_REP = SWA_HEADS // SWA_KV_HEADS
MLA_HEADS = 6
MLA_Q_RANK = 768
MLA_KV_RANK = 512
MLA_NOPE = 128
MLA_ROPE = 64
MLA_V = 128
MLA_QK_PAD = 256
MLA_V_PAD = 256
D_FF = 4 * D_MODEL
IN_WIDTHS = (512, 768, 256, 256, 768, 512, 64)
IN_WIDTH = sum(IN_WIDTHS)
OFF_U, OFF_QS, OFF_KS, OFF_VS, OFF_CQ, OFF_CKV, OFF_KR = 0, 512, 1280, 1536, 1792, 2560, 3072

VMEM_LIMIT = 56 * 1024 * 1024


def _cparams(*sem):
    return pltpu.CompilerParams(dimension_semantics=sem, vmem_limit_bytes=VMEM_LIMIT)


def _dot(a, b):
    return jnp.dot(a, b, preferred_element_type=F32)


def _dot_nt(a, b):
    return lax.dot_general(a, b, (((1,), (1,)), ((), ())), preferred_element_type=F32)


def _rms(x, g):
    return x * lax.rsqrt(jnp.mean(x * x, axis=-1, keepdims=True) + EPS) * g


def _rope(x, cos, sin, half):
    lane = lax.broadcasted_iota(jnp.int32, x.shape, 1)
    fwd = pltpu.roll(x, 128 - half, 1)
    bwd = pltpu.roll(x, half, 1)
    sw = jnp.where((lane % (2 * half)) < half, fwd, bwd)
    return x * cos + sw * sin


def _ada_kernel(c_ref, w_ref, b_ref, o_ref):
    c = c_ref[...]
    s = c * jax.nn.sigmoid(c)
    o_ref[0] = _dot(s.astype(BF16), w_ref[0].astype(BF16)) + b_ref[0]


def _ada(cvec, ada_w, ada_b):
    tn = 1024
    n = 6 * D_MODEL
    return pl.pallas_call(
        _ada_kernel,
        grid=(DEPTH, n // tn),
        in_specs=[pl.BlockSpec((8, D_MODEL), lambda l, j: (0, 0)),
                  pl.BlockSpec((1, D_MODEL, tn), lambda l, j: (l, 0, j)),
                  pl.BlockSpec((1, 1, tn), lambda l, j: (l, 0, j))],
        out_specs=pl.BlockSpec((1, 8, tn), lambda l, j: (l, 0, j)),
        out_shape=jax.ShapeDtypeStruct((DEPTH, 8, n), F32),
        compiler_params=_cparams("arbitrary", "arbitrary"),
        name="ada",
    )(cvec, ada_w, ada_b.reshape(DEPTH, 1, n))


IN_COLS = 3200
SWA_QK_W = (SWA_HEADS + SWA_KV_HEADS) * HEAD_DIM


def _mix_in_kernel(x_ref, shift_ref, scale_ref, g_ref, w_ref, wq_ref, wkv_ref, qn_ref, kvn_ref,
                   cs_ref, ss_ref, cm_ref, sm_ref,
                   u_ref, qk_ref, vs_ref, q_ref, k_ref, v_ref, h_ref, *, rope):
    h = _rms(x_ref[...], g_ref[...]) * (1.0 + scale_ref[0]) + shift_ref[0]
    h_ref[...] = h.astype(BF16)

    def proj(lo, width):
        return _dot(h_ref[...], w_ref[:, lo:lo + width])

    u_ref[...] = proj(OFF_U, S5_WIDTH)

    swa_scale = HEAD_DIM ** -0.5
    per = 4
    for c in range(SWA_QK_W // (per * HEAD_DIM)):
        zc = proj(OFF_QS + c * per * HEAD_DIM, per * HEAD_DIM)
        for hh in range(per):
            head = c * per + hh
            xh = zc[:, hh * HEAD_DIM:(hh + 1) * HEAD_DIM]
            if rope:
                xh = _rope(xh, cs_ref[...], ss_ref[...], HEAD_DIM // 4)
            if head < SWA_HEADS:
                xh = xh * swa_scale
            qk_ref[:, head * HEAD_DIM:(head + 1) * HEAD_DIM] = xh.astype(BF16)

    vs_ref[...] = proj(OFF_VS, SWA_KV_HEADS * HEAD_DIM).astype(BF16)

    mla_scale = (MLA_NOPE + MLA_ROPE) ** -0.5
    cq = proj(OFF_CQ, MLA_Q_RANK)
    q = _dot(_rms(cq, qn_ref[...]).astype(BF16), wq_ref[...])
    for hd in range(MLA_HEADS):
        lo = hd * MLA_QK_PAD
        q_ref[:, lo:lo + MLA_NOPE] = (q[:, lo:lo + MLA_NOPE] * mla_scale).astype(BF16)
        r = q[:, lo + MLA_NOPE:lo + MLA_QK_PAD]
        if rope:
            r = _rope(r, cm_ref[...], sm_ref[...], MLA_ROPE // 4)
        q_ref[:, lo + MLA_NOPE:lo + MLA_QK_PAD] = (r * mla_scale).astype(BF16)

    ckv = proj(OFF_CKV, MLA_KV_RANK)
    kv = _dot(_rms(ckv, kvn_ref[...]).astype(BF16), wkv_ref[...])
    kr = proj(OFF_KR, 128)
    if rope:
        kr = _rope(kr, cm_ref[...], sm_ref[...], MLA_ROPE // 4)
    kr = kr.astype(BF16)
    for hd in range(MLA_HEADS):
        lo = hd * MLA_QK_PAD
        k_ref[:, lo:lo + MLA_NOPE] = kv[:, hd * MLA_NOPE:(hd + 1) * MLA_NOPE].astype(BF16)
        k_ref[:, lo + MLA_NOPE:lo + MLA_QK_PAD] = kr
        vlo = hd * MLA_V_PAD
        vh = kv[:, MLA_HEADS * MLA_NOPE + hd * MLA_V:MLA_HEADS * MLA_NOPE + (hd + 1) * MLA_V].astype(BF16)
        v_ref[:, vlo:vlo + MLA_V] = vh
        v_ref[:, vlo + MLA_V:vlo + MLA_V_PAD] = jnp.ones_like(vh)


def _mix_in(x2d, shift, scale, g, w_in, w_uq, w_ukv, q_norm, kv_norm, tabs, layer, tm, mod_row, rope, seq):
    rows = x2d.shape[0]
    once = pl.Buffered(1)
    tab = pl.BlockSpec((tm, 128), lambda i: (i % (seq // tm), 0))

    def res(shape):
        return pl.BlockSpec((None,) + shape, lambda i: (layer, 0, 0), pipeline_mode=once)

    def out(width):
        return pl.BlockSpec((tm, width), lambda i: (i, 0))

    widths = (S5_WIDTH, SWA_QK_W, SWA_KV_HEADS * HEAD_DIM,
              MLA_HEADS * MLA_QK_PAD, MLA_HEADS * MLA_QK_PAD, MLA_HEADS * MLA_V_PAD)
    dtypes = (F32, BF16, BF16, BF16, BF16, BF16)
    return pl.pallas_call(
        functools.partial(_mix_in_kernel, rope=rope),
        grid=(rows // tm,),
        in_specs=[pl.BlockSpec((tm, D_MODEL), lambda i: (i, 0)),
                  pl.BlockSpec((1, 1, D_MODEL), lambda i: (mod_row(i), 0, 0)),
                  pl.BlockSpec((1, 1, D_MODEL), lambda i: (mod_row(i), 0, 0)),
                  pl.BlockSpec((1, D_MODEL), lambda i: (0, 0)),
                  res((D_MODEL, IN_COLS)),
                  res((MLA_Q_RANK, MLA_HEADS * MLA_QK_PAD)),
                  res((MLA_KV_RANK, MLA_HEADS * (MLA_NOPE + MLA_V))),
                  pl.BlockSpec((1, MLA_Q_RANK), lambda i: (0, 0)),
                  pl.BlockSpec((1, MLA_KV_RANK), lambda i: (0, 0)),
                  tab, tab, tab, tab],
        out_specs=[out(w) for w in widths],
        out_shape=[jax.ShapeDtypeStruct((rows, w), dt) for w, dt in zip(widths, dtypes)],
        scratch_shapes=[pltpu.VMEM((tm, D_MODEL), BF16)],
        compiler_params=_cparams("arbitrary"),
        name="mix_in",
    )(x2d, shift, scale, g, w_in, w_uq, w_ukv, q_norm.reshape(1, -1), kv_norm.reshape(1, -1), *tabs)


def _s5_disc_kernel(are_ref, aim_ref, ldt_ref, ebr_ref, ebi_ref, cr_ref, ci_ref,
                    lam_ref, wbr_ref, wbi_ref, wcr_ref, wci_ref):
    are = are_ref[0]
    aim = aim_ref[0]
    dt = jnp.exp(ldt_ref[0])
    mag = jnp.exp(are * dt)
    lr = mag * jnp.cos(aim * dt)
    li = mag * jnp.sin(aim * dt)
    den = are * are + aim * aim
    nr = lr - 1.0
    fr = (nr * are + li * aim) / den
    fi = (li * are - nr * aim) / den
    lam_ref[0, 0:1, :] = lr
    lam_ref[0, 1:2, :] = li
    half = S5_LANES // 2
    for h in range(2):
        frh = fr[:, h * half:(h + 1) * half]
        fih = fi[:, h * half:(h + 1) * half]
        wbr_ref[0, h] = (frh * ebr_ref[0, h] - fih * ebi_ref[0, h]).astype(BF16)
        wbi_ref[0, h] = (frh * ebi_ref[0, h] + fih * ebr_ref[0, h]).astype(BF16)
        wcr_ref[0, h] = cr_ref[0, h].astype(BF16)
        wci_ref[0, h] = ci_ref[0, h].astype(BF16)


def _s5_disc(a_re, a_im, log_dt, b_re, b_im, c_re, c_im):
    nd = DEPTH * 2
    gh = S5_GROUPS // 2
    eye = jnp.eye(gh, dtype=F32)
    are = a_re.reshape(nd, 1, S5_LANES)
    aim = a_im.reshape(nd, 1, S5_LANES)
    ldt = jnp.repeat(log_dt, S5_STATE, axis=-1).reshape(nd, 1, S5_LANES)

    def expand_b(b):
        b5 = b.reshape(nd, 2, gh, S5_STATE, S5_GROUP)
        return jnp.einsum('dhgnp,gk->dhgpkn', b5, eye).reshape(nd, 2, gh * S5_GROUP, gh * S5_STATE)

    def expand_c(c):
        c5 = c.reshape(nd, 2, gh, S5_GROUP, S5_STATE)
        return jnp.einsum('dhgpn,gk->dhgnkp', c5, eye).reshape(nd, 2, gh * S5_STATE, gh * S5_GROUP)

    kin, kst = gh * S5_GROUP, gh * S5_STATE
    vec = pl.BlockSpec((1, 1, S5_LANES), lambda d: (d, 0, 0))
    bspec = pl.BlockSpec((1, 2, kin, kst), lambda d: (d, 0, 0, 0))
    cspec = pl.BlockSpec((1, 2, kst, kin), lambda d: (d, 0, 0, 0))
    return pl.pallas_call(
        _s5_disc_kernel,
        grid=(nd,),
        in_specs=[vec, vec, vec, bspec, bspec, cspec, cspec],
        out_specs=[pl.BlockSpec((1, 2, S5_LANES), lambda d: (d, 0, 0)), bspec, bspec, cspec, cspec],
        out_shape=[jax.ShapeDtypeStruct((nd, 2, S5_LANES), F32),
                   jax.ShapeDtypeStruct((nd, 2, kin, kst), BF16),
                   jax.ShapeDtypeStruct((nd, 2, kin, kst), BF16),
                   jax.ShapeDtypeStruct((nd, 2, kst, kin), BF16),
                   jax.ShapeDtypeStruct((nd, 2, kst, kin), BF16)],
        compiler_params=_cparams("arbitrary"),
        name="s5_disc",
    )(are, aim, ldt, expand_b(b_re), expand_b(b_im), expand_c(c_re), expand_c(c_im))


S5_TC = 128
S5_CTX_CHUNKS = CTX_LEN // S5_TC
S5_LAT_CHUNKS = SEQ // S5_TC


def _s5_scan_kernel(ucf_ref, ulf_ref, ucb_ref, ulb_ref, lam_ref, wbr_ref, wbi_ref, wcr_ref, wci_ref,
                    yf_ref, yb_ref, fr_ref, fi_ref, gr_ref, gi_ref, sr_ref, si_ref, *, skip_chunks):
    c = pl.program_id(0)
    tc = S5_TC
    kin, kst = S5_WIDTH // 2, S5_LANES // 2

    @pl.when(c == 0)
    def _():
        sr_ref[...] = jnp.zeros_like(sr_ref)
        si_ref[...] = jnp.zeros_like(si_ref)

    nlb = kst // 128

    def project(u_ref, d, dst_r, dst_i):
        u = u_ref[...].reshape(BATCH * tc, S5_WIDTH).astype(BF16)
        for h in range(2):
            uh = u[:, h * kin:(h + 1) * kin]
            for w_ref, dst in ((wbr_ref, dst_r), (wbi_ref, dst_i)):
                p = _dot(uh, w_ref[d, h])
                for j in range(nlb):
                    for b in range(BATCH):
                        dst[h * nlb + j, pl.ds(b, tc, stride=BATCH), :] = (
                            p[b * tc:(b + 1) * tc, j * 128:(j + 1) * 128])

    @pl.when(c < S5_CTX_CHUNKS)
    def _():
        project(ucf_ref, 0, fr_ref, fi_ref)
        project(ucb_ref, 1, gr_ref, gi_ref)

    @pl.when(c >= S5_CTX_CHUNKS)
    def _():
        project(ulf_ref, 0, fr_ref, fi_ref)
        project(ulb_ref, 1, gr_ref, gi_ref)

    nb = 8
    for lb in range(2 * nlb // nb):
        ls = slice(lb * nb, (lb + 1) * nb)
        fwd_rows = lax.broadcasted_iota(jnp.int32, (nb, 8, 128), 1) < BATCH
        la_r = jnp.where(fwd_rows, lam_ref[0, 0, ls], lam_ref[1, 0, ls])
        la_i = jnp.where(fwd_rows, lam_ref[0, 1, ls], lam_ref[1, 1, ls])
        lb_r = jnp.where(fwd_rows, lam_ref[1, 0, ls], lam_ref[0, 0, ls])
        lb_i = jnp.where(fwd_rows, lam_ref[1, 1, ls], lam_ref[0, 1, ls])

        def body(k, carry, ls=ls, fwd_rows=fwd_rows, la_r=la_r, la_i=la_i, lb_r=lb_r, lb_i=lb_i):
            s_r, s_i = carry
            rf = pl.ds(pl.multiple_of(k * 8, 8), 8)
            rb = pl.ds(pl.multiple_of((tc // 2 - 1 - k) * 8, 8), 8)
            f_r, f_i = fr_ref[ls, rf, :], fi_ref[ls, rf, :]
            g_r, g_i = gr_ref[ls, rb, :], gi_ref[ls, rb, :]
            a_r = la_r * s_r - la_i * s_i + jnp.where(fwd_rows, f_r, g_r)
            a_i = la_r * s_i + la_i * s_r + jnp.where(fwd_rows, f_i, g_i)
            t_r = pltpu.roll(a_r, BATCH, 1)
            t_i = pltpu.roll(a_i, BATCH, 1)
            b_r = lb_r * t_r - lb_i * t_i + jnp.where(fwd_rows, g_r, f_r)
            b_i = lb_r * t_i + lb_i * t_r + jnp.where(fwd_rows, g_i, f_i)
            fr_ref[ls, rf, :] = jnp.where(fwd_rows, a_r, b_r)
            fi_ref[ls, rf, :] = jnp.where(fwd_rows, a_i, b_i)
            gr_ref[ls, rb, :] = jnp.where(fwd_rows, b_r, a_r)
            gi_ref[ls, rb, :] = jnp.where(fwd_rows, b_i, a_i)
            return pltpu.roll(b_r, BATCH, 1), pltpu.roll(b_i, BATCH, 1)

        s_r, s_i = lax.fori_loop(0, tc // 2, body, (sr_ref[ls], si_ref[ls]))
        sr_ref[ls] = s_r
        si_ref[ls] = s_i

    def gather_rows(src, h):
        return jnp.concatenate(
            [jnp.concatenate([src[h * nlb + j, pl.ds(b, tc, stride=BATCH), :] for j in range(nlb)], axis=1)
             for b in range(BATCH)], axis=0)

    def readout(d, src_r, src_i, y_ref):
        for h in range(2):
            hr = gather_rows(src_r, h).astype(BF16)
            hi = gather_rows(src_i, h).astype(BF16)
            y = _dot(hr, wcr_ref[d, h]) - _dot(hi, wci_ref[d, h])
            y_ref[:, :, h * kin:(h + 1) * kin] = y.reshape(BATCH, tc, kin)

    @pl.when(c >= skip_chunks)
    def _():
        readout(0, fr_ref, fi_ref, yf_ref)
        readout(1, gr_ref, gi_ref, yb_ref)


def _s5_scan(u_c, u_l, lam, wbr, wbi, wcr, wci, layer, skip_chunks):
    tc = S5_TC
    nchunk = S5_CTX_CHUNKS + S5_LAT_CHUNKS
    kin, kst = S5_WIDTH // 2, S5_LANES // 2
    zc3 = u_c.reshape(BATCH, CTX_LEN, S5_WIDTH)
    zl3 = u_l.reshape(BATCH, SEQ, S5_WIDTH)

    def bwd_chunk(c):
        return jnp.where(c < S5_CTX_CHUNKS, S5_CTX_CHUNKS - 1 - c, nchunk + S5_CTX_CHUNKS - 1 - c)

    def ctx_spec(chunk_of):
        return pl.BlockSpec((BATCH, tc, S5_WIDTH),
                            lambda c: (0, jnp.clip(chunk_of(c), 0, S5_CTX_CHUNKS - 1), 0))

    def lat_spec(chunk_of):
        return pl.BlockSpec((BATCH, tc, S5_WIDTH),
                            lambda c: (0, jnp.clip(chunk_of(c) - S5_CTX_CHUNKS, 0, S5_LAT_CHUNKS - 1), 0))

    def out_spec(chunk_of):
        return pl.BlockSpec((BATCH, tc, S5_WIDTH),
                            lambda c: (0, chunk_of(jnp.maximum(c, skip_chunks)) - skip_chunks, 0))

    once = pl.Buffered(1)
    bspec = pl.BlockSpec((2, 2, kin, kst), lambda c: (layer, 0, 0, 0), pipeline_mode=once)
    cspec = pl.BlockSpec((2, 2, kst, kin), lambda c: (layer, 0, 0, 0), pipeline_mode=once)
    t_out = (nchunk - skip_chunks) * tc
    nblk = S5_LANES // 128
    big = pltpu.VMEM((nblk, tc * BATCH, 128), F32)
    state = pltpu.VMEM((nblk, 8, 128), F32)
    lam = lam.reshape(DEPTH * 2, 2, nblk, 1, 128)
    fwd_chunk = lambda c: c
    return pl.pallas_call(
        functools.partial(_s5_scan_kernel, skip_chunks=skip_chunks),
        grid=(nchunk,),
        in_specs=[ctx_spec(fwd_chunk), lat_spec(fwd_chunk), ctx_spec(bwd_chunk), lat_spec(bwd_chunk),
                  pl.BlockSpec((2, 2, nblk, 1, 128), lambda c: (layer, 0, 0, 0, 0)),
                  bspec, bspec, cspec, cspec],
        out_specs=[out_spec(fwd_chunk), out_spec(bwd_chunk)],
        out_shape=[jax.ShapeDtypeStruct((BATCH, t_out, S5_WIDTH), F32)] * 2,
        scratch_shapes=[big, big, big, big, state, state],
        compiler_params=_cparams("arbitrary"),
        name="s5_scan",
    )(zc3, zl3, zc3, zl3, lam, wbr, wbi, wcr, wci)


def _s5_glu_kernel(u_ref, yf_ref, yb_ref, d_ref, w_ref, b_ref, o_ref):
    y = d_ref[...] * u_ref[0] + yf_ref[0] + yb_ref[0]
    g = jax.nn.gelu(y)
    gate = jax.nn.sigmoid(_dot(g.astype(BF16), w_ref[...]) + b_ref[...])
    o_ref[0] = (g * gate).astype(o_ref.dtype)


def _s5_glu(z3, yf, yb, d, w, b, layer, y_off):
    tm = 256
    n = z3.shape[1]
    off = y_off // tm
    zspec = pl.BlockSpec((1, tm, S5_WIDTH), lambda bi, i: (bi, i, 0))
    yspec = pl.BlockSpec((1, tm, S5_WIDTH), lambda bi, i: (bi, i + off, 0))
    vec = pl.BlockSpec((1, S5_WIDTH), lambda bi, i: (0, 0))
    return pl.pallas_call(
        _s5_glu_kernel,
        grid=(BATCH, n // tm),
        in_specs=[zspec, yspec, yspec, vec,
                  pl.BlockSpec((None, S5_WIDTH, S5_WIDTH), lambda bi, i: (layer, 0, 0)), vec],
        out_specs=zspec,
        out_shape=jax.ShapeDtypeStruct((BATCH, n, S5_WIDTH), BF16),
        compiler_params=_cparams("arbitrary", "arbitrary"),
        name="s5_glu",
    )(z3, yf, yb, d.reshape(1, -1), w, b.reshape(1, -1)).reshape(BATCH * n, S5_WIDTH)


def _s5_mixer(u_c, u_l, s5_w, d, glu_w, glu_b, layer, need_ctx):
    lam, wbr, wbi, wcr, wci = s5_w
    skip = 0 if need_ctx else S5_CTX_CHUNKS
    yf, yb = _s5_scan(u_c, u_l, lam, wbr, wbi, wcr, wci, layer, skip)
    nctx = CTX_LEN if need_ctx else 0
    s5_l = _s5_glu(u_l.reshape(BATCH, SEQ, S5_WIDTH), yf, yb, d, glu_w, glu_b, layer, nctx)
    s5_c = None
    if need_ctx:
        s5_c = _s5_glu(u_c.reshape(BATCH, CTX_LEN, S5_WIDTH), yf, yb, d, glu_w, glu_b, layer, 0)
    return s5_c, s5_l


def _sink_rows(sink_ref, g, rows):
    r = lax.broadcasted_iota(jnp.int32, (rows, 1), 0) // (rows // SWA_REP)
    s0, s1, s2 = sink_ref[SWA_REP * g], sink_ref[SWA_REP * g + 1], sink_ref[SWA_REP * g + 2]
    return jnp.where(r == 0, s0, jnp.where(r == 1, s1, s2))


def _stack_heads(x, n):
    return jnp.concatenate([x[:, h * HEAD_DIM:(h + 1) * HEAD_DIM] for h in range(n)], axis=0)


def _unstack_heads(x, n):
    rows = x.shape[0] // n
    return jnp.concatenate([x[h * rows:(h + 1) * rows] for h in range(n)], axis=1)


SWA_QB = 4


def _with_ones(v):
    return jnp.concatenate([v, jnp.ones_like(v)], axis=1)


def _swa_latent_kernel(sink_ref, q_ref, kp_ref, km_ref, kn_ref, vp_ref, vm_ref, vn_ref,
                       kx_ref, vx_ref, o_ref):
    n = pl.program_id(1)
    rows = SWA_REP * BLOCK
    qi = lax.broadcasted_iota(jnp.int32, (rows, 3 * BLOCK), 0) % BLOCK
    kj = lax.broadcasted_iota(jnp.int32, (rows, 3 * BLOCK), 1)
    in_window = (kj >= qi) & (kj <= qi + 2 * BLOCK)
    for g in range(SWA_KV_HEADS):
        gs = slice(g * HEAD_DIM, (g + 1) * HEAD_DIM)
        kspan = jnp.concatenate([kp_ref[:, gs], km_ref[:, gs], kn_ref[:, gs]], axis=0)
        vspan = _with_ones(jnp.concatenate([vp_ref[:, gs], vm_ref[:, gs], vn_ref[:, gs]], axis=0))
        kx = kx_ref[:, gs]
        vx = _with_ones(vx_ref[:, gs])
        sk = _sink_rows(sink_ref, g, rows)
        for j in range(SWA_QB):
            blk = n * SWA_QB + j
            qs = slice(g * SWA_REP * HEAD_DIM, (g + 1) * SWA_REP * HEAD_DIM)
            q = _stack_heads(q_ref[j * BLOCK:(j + 1) * BLOCK, qs], SWA_REP)
            kb = kspan[j * BLOCK:(j + 3) * BLOCK]
            vb = vspan[j * BLOCK:(j + 3) * BLOCK]
            s_ctx = _dot_nt(q, kx)
            s_band = _dot_nt(q, kb)
            kpos = (blk - 1) * BLOCK + kj
            valid = in_window & (kpos >= 0) & (kpos < SEQ)
            s_band = jnp.where(valid, s_band, NEG_INF)
            m = jnp.maximum(jnp.maximum(jnp.max(s_ctx, axis=-1, keepdims=True),
                                        jnp.max(s_band, axis=-1, keepdims=True)), sk)
            p_ctx = jnp.exp(s_ctx - m).astype(BF16)
            p_band = jnp.exp(s_band - m).astype(BF16)
            o = _dot(p_ctx, vx) + _dot(p_band, vb)
            den = o[:, HEAD_DIM:] + jnp.exp(sk - m)
            o_ref[j * BLOCK:(j + 1) * BLOCK, qs] = _unstack_heads(
                o[:, :HEAD_DIM] / den, SWA_REP).astype(o_ref.dtype)


def _swa_latent(qk, vs, qk_c, vs_c, sink):
    nb = SEQ // BLOCK
    ng = nb // SWA_QB
    span = SWA_QB * BLOCK
    kvw = SWA_KV_HEADS * HEAD_DIM
    kcol = SWA_HEADS * HEAD_DIM // kvw
    vcol = 0
    kxcol = kcol

    def edge(col, blk_of):
        return pl.BlockSpec((BLOCK, kvw), lambda b, n, s: (b * nb + jnp.clip(blk_of(n), 0, nb - 1), col))

    def main(col):
        return pl.BlockSpec((span, kvw), lambda b, n, s: (b * ng + n, col))

    prev_blk = lambda n: n * SWA_QB - 1
    next_blk = lambda n: (n + 1) * SWA_QB
    grid_spec = pltpu.PrefetchScalarGridSpec(
        num_scalar_prefetch=1,
        grid=(BATCH, ng),
        in_specs=[pl.BlockSpec((span, SWA_HEADS * HEAD_DIM), lambda b, n, s: (b * ng + n, 0)),
                  edge(kcol, prev_blk), main(kcol), edge(kcol, next_blk),
                  edge(vcol, prev_blk), main(vcol), edge(vcol, next_blk),
                  pl.BlockSpec((CTX_LEN, kvw), lambda b, n, s: (b, kxcol)),
                  pl.BlockSpec((CTX_LEN, kvw), lambda b, n, s: (b, vcol))],
        out_specs=pl.BlockSpec((span, SWA_HEADS * HEAD_DIM), lambda b, n, s: (b * ng + n, 0)),
    )
    return pl.pallas_call(
        _swa_latent_kernel,
        grid_spec=grid_spec,
        out_shape=jax.ShapeDtypeStruct((BATCH * SEQ, SWA_HEADS * HEAD_DIM), BF16),
        compiler_params=_cparams("arbitrary", "arbitrary"),
        name="swa_latent",
    )(sink, qk, qk, qk, qk, vs, vs, vs, qk_c, vs_c)


def _swa_context_kernel(sink_ref, q_ref, k_ref, v_ref, o_ref):
    g = pl.program_id(1)
    q = _stack_heads(q_ref[...], SWA_REP)
    s = _dot_nt(q, k_ref[...])
    sk = _sink_rows(sink_ref, g, SWA_REP * CTX_LEN)
    m = jnp.maximum(jnp.max(s, axis=-1, keepdims=True), sk)
    p = jnp.exp(s - m)
    den = jnp.sum(p, axis=-1, keepdims=True) + jnp.exp(sk - m)
    o = _dot(p.astype(BF16), v_ref[...]) / den
    o_ref[...] = _unstack_heads(o, SWA_REP).astype(o_ref.dtype)


def _swa_context(qk_c, vs_c, sink):
    qw = SWA_REP * HEAD_DIM
    grid_spec = pltpu.PrefetchScalarGridSpec(
        num_scalar_prefetch=1,
        grid=(BATCH, SWA_KV_HEADS),
        in_specs=[pl.BlockSpec((CTX_LEN, qw), lambda b, g, s: (b, g)),
                  pl.BlockSpec((CTX_LEN, HEAD_DIM), lambda b, g, s: (b, SWA_HEADS + g)),
                  pl.BlockSpec((CTX_LEN, HEAD_DIM), lambda b, g, s: (b, g))],
        out_specs=pl.BlockSpec((CTX_LEN, qw), lambda b, g, s: (b, g)),
    )
    return pl.pallas_call(
        _swa_context_kernel,
        grid_spec=grid_spec,
        out_shape=jax.ShapeDtypeStruct((BATCH * CTX_LEN, SWA_HEADS * HEAD_DIM), BF16),
        compiler_params=_cparams("arbitrary", "arbitrary"),
        name="swa_context",
    )(sink, qk_c, qk_c, vs_c)


def _mla_attn_kernel(*refs, nseg):
    q_ref = refs[0]
    k_refs = refs[1:1 + nseg]
    v_refs = refs[1 + nseg:1 + 2 * nseg]
    o_ref = refs[1 + 2 * nseg]
    for h in range(MLA_HEADS):
        q = q_ref[:, h * MLA_QK_PAD:(h + 1) * MLA_QK_PAD]
        s = [_dot_nt(q, k[:, h * MLA_QK_PAD:(h + 1) * MLA_QK_PAD]) for k in k_refs]
        m = functools.reduce(jnp.maximum, [jnp.max(x, axis=-1, keepdims=True) for x in s])
        p = [jnp.exp(x - m).astype(BF16) for x in s]
        o = sum(_dot(x, v[:, h * MLA_V_PAD:(h + 1) * MLA_V_PAD]) for x, v in zip(p, v_refs))
        o_ref[:, h * MLA_V:(h + 1) * MLA_V] = (o[:, :MLA_V] / o[:, MLA_V:]).astype(o_ref.dtype)


def _mla_attn(q, ks, vs, n_q, tq):
    nseg = len(ks)
    nq = n_q // tq
    lens = [k.shape[0] // BATCH for k in ks]
    kw, vw = MLA_HEADS * MLA_QK_PAD, MLA_HEADS * MLA_V
    in_specs = [pl.BlockSpec((tq, kw), lambda b, i: (b * nq + i, 0))]
    in_specs += [pl.BlockSpec((n, kw), lambda b, i: (b, 0)) for n in lens]
    in_specs += [pl.BlockSpec((n, MLA_HEADS * MLA_V_PAD), lambda b, i: (b, 0)) for n in lens]
    return pl.pallas_call(
        functools.partial(_mla_attn_kernel, nseg=nseg),
        grid=(BATCH, nq),
        in_specs=in_specs,
        out_specs=pl.BlockSpec((tq, vw), lambda b, i: (b * nq + i, 0)),
        out_shape=jax.ShapeDtypeStruct((BATCH * n_q, vw), BF16),
        compiler_params=_cparams("arbitrary", "arbitrary"),
        name="mla_attn",
    )(q, *ks, *vs)


def _out_proj_kernel(a1_ref, a2_ref, a3_ref, w_ref, x_ref, g_ref, gate_ref, o_ref):
    nc = 512
    k1 = a1_ref.shape[1]
    k2 = k1 + a2_ref.shape[1]
    for n in range(D_MODEL // nc):
        cs = slice(n * nc, (n + 1) * nc)
        o_ref[:, cs] = (_dot(a1_ref[...], w_ref[0:k1, cs]) + _dot(a2_ref[...], w_ref[k1:k2, cs])
                        + _dot(a3_ref[...], w_ref[k2:D_MODEL, cs]))
    o_ref[...] = x_ref[...] + gate_ref[0] * _rms(o_ref[...], g_ref[...])


def _out_proj(a1, a2, a3, w, x2d, g, gate, layer, tm, mod_row):
    rows = a1.shape[0]
    return pl.pallas_call(
        _out_proj_kernel,
        grid=(rows // tm,),
        in_specs=[pl.BlockSpec((tm, a1.shape[1]), lambda i: (i, 0)),
                  pl.BlockSpec((tm, a2.shape[1]), lambda i: (i, 0)),
                  pl.BlockSpec((tm, a3.shape[1]), lambda i: (i, 0)),
                  pl.BlockSpec((None, D_MODEL, D_MODEL), lambda i: (layer, 0, 0), pipeline_mode=pl.Buffered(1)),
                  pl.BlockSpec((tm, D_MODEL), lambda i: (i, 0)),
                  pl.BlockSpec((1, D_MODEL), lambda i: (0, 0)),
                  pl.BlockSpec((1, 1, D_MODEL), lambda i: (mod_row(i), 0, 0))],
        out_specs=pl.BlockSpec((tm, D_MODEL), lambda i: (i, 0)),
        out_shape=jax.ShapeDtypeStruct((rows, D_MODEL), F32),
        compiler_params=_cparams("arbitrary"),
        name="out_proj",
    )(a1, a2, a3, w, x2d, g.reshape(1, -1), gate)


def _ffn_kernel(x_ref, shift_ref, scale_ref, gpre_ref, w1_ref, w2_ref, gpost_ref, gate_ref, o_ref,
                f_ref, h_ref):
    j = pl.program_id(1)

    @pl.when(j == 0)
    def _():
        f = _rms(x_ref[...], gpre_ref[...]) * (1.0 + scale_ref[0]) + shift_ref[0]
        f_ref[...] = f.astype(BF16)
        o_ref[...] = jnp.zeros_like(o_ref)

    h = _dot(f_ref[...], w1_ref[...].astype(BF16))
    h_ref[...] = jnp.square(jnp.maximum(h, 0.0)).astype(BF16)
    nc = 512
    for n in range(D_MODEL // nc):
        cs = slice(n * nc, (n + 1) * nc)
        o_ref[:, cs] += _dot(h_ref[...], w2_ref[:, cs].astype(BF16))

    @pl.when(j == pl.num_programs(1) - 1)
    def _():
        o_ref[...] = x_ref[...] + gate_ref[0] * _rms(o_ref[...], gpost_ref[...])


def _ffn(x2d, shift, scale, gpre, w1, w2, gpost, gate, layer, tm, mod_row):
    rows = x2d.shape[0]
    tf = 512
    mod = pl.BlockSpec((1, 1, D_MODEL), lambda i, j: (mod_row(i), 0, 0))
    vec = pl.BlockSpec((1, D_MODEL), lambda i, j: (0, 0))
    once = pl.Buffered(1)
    return pl.pallas_call(
        _ffn_kernel,
        grid=(rows // tm, D_FF // tf),
        in_specs=[pl.BlockSpec((tm, D_MODEL), lambda i, j: (i, 0)), mod, mod, vec,
                  pl.BlockSpec((None, D_MODEL, tf), lambda i, j: (layer, 0, j)),
                  pl.BlockSpec((None, tf, D_MODEL), lambda i, j: (layer, j, 0)),
                  vec, mod],
        out_specs=pl.BlockSpec((tm, D_MODEL), lambda i, j: (i, 0), pipeline_mode=once),
        out_shape=jax.ShapeDtypeStruct((rows, D_MODEL), F32),
        scratch_shapes=[pltpu.VMEM((tm, D_MODEL), BF16), pltpu.VMEM((tm, tf), BF16)],
        compiler_params=_cparams("arbitrary", "arbitrary"),
        name="ffn",
    )(x2d, shift, scale, gpre.reshape(1, -1), w1, w2, gpost.reshape(1, -1), gate)


def _rope_tables():
    t = jnp.arange(SEQ)
    row = (t // GRID_W).astype(F32)[:, None]
    col = (t % GRID_W).astype(F32)[:, None]

    def tables(rot_dim):
        quarter = rot_dim // 4
        inv_freq = ROPE_BASE ** (-jnp.arange(quarter, dtype=F32) / quarter)
        ar, ac = row * inv_freq, col * inv_freq
        cos = jnp.concatenate([jnp.cos(ar), jnp.cos(ar), jnp.cos(ac), jnp.cos(ac)], axis=1)
        sin = jnp.concatenate([-jnp.sin(ar), jnp.sin(ar), -jnp.sin(ac), jnp.sin(ac)], axis=1)
        pad = 128 - rot_dim
        if pad:
            cos = jnp.concatenate([cos, jnp.ones((SEQ, pad), F32)], axis=1)
            sin = jnp.concatenate([sin, jnp.zeros((SEQ, pad), F32)], axis=1)
        return cos, sin

    return tables(HEAD_DIM), tables(MLA_ROPE)


def kernel(x, c, ctx, c_ctx, ada_w, ada_b, norm_mix_pre, norm_mix_post, norm_ffn_pre, norm_ffn_post, w_in, w_out, s5_a_re, s5_a_im, s5_log_dt, s5_b_re, s5_b_im, s5_c_re, s5_c_im, s5_d, s5_glu_w, s5_glu_b, swa_sink, mla_q_norm, mla_w_uq, mla_kv_norm, mla_w_ukv, ffn_w1, ffn_w2):
    (cos_swa, sin_swa), (cos_mla, sin_mla) = _rope_tables()
    tabs = (cos_swa, sin_swa, cos_mla, sin_mla)

    cvec = jnp.concatenate([c, c_ctx[None, :], jnp.zeros((3, D_MODEL), F32)], axis=0)
    mods = _ada(cvec, ada_w, ada_b)

    tm_p = 512
    tm_f = 1024

    def lat_row(tm):
        return lambda i: i // (SEQ // tm)

    ctx_row = lambda i: 4

    xl = x.reshape(BATCH * SEQ, D_MODEL)
    xc = ctx.reshape(BATCH * CTX_LEN, D_MODEL)

    w_in_pad = jnp.pad(w_in, ((0, 0), (0, 0), (0, IN_COLS - IN_WIDTH))).astype(BF16)
    w_out_b = w_out.astype(BF16)
    glu_w_b = s5_glu_w.astype(BF16)
    w_uq_pad = jnp.pad(mla_w_uq.reshape(DEPTH, MLA_Q_RANK, MLA_HEADS, MLA_NOPE + MLA_ROPE),
                       ((0, 0), (0, 0), (0, 0), (0, MLA_QK_PAD - MLA_NOPE - MLA_ROPE))
                       ).reshape(DEPTH, MLA_Q_RANK, MLA_HEADS * MLA_QK_PAD).astype(BF16)
    w_ukv4 = mla_w_ukv.reshape(DEPTH, MLA_KV_RANK, MLA_HEADS, MLA_NOPE + MLA_V)
    w_ukv_perm = jnp.concatenate([w_ukv4[..., :MLA_NOPE].reshape(DEPTH, MLA_KV_RANK, -1),
                                  w_ukv4[..., MLA_NOPE:].reshape(DEPTH, MLA_KV_RANK, -1)], axis=2).astype(BF16)
    s5_w = _s5_disc(s5_a_re, s5_a_im, s5_log_dt, s5_b_re, s5_b_im, s5_c_re, s5_c_im)

    for i in range(DEPTH):
        need_ctx = i < DEPTH - 1
        mod = [mods[i, :, k * D_MODEL:(k + 1) * D_MODEL].reshape(8, 1, D_MODEL) for k in range(6)]
        g_pre = norm_mix_pre[i].reshape(1, -1)

        u_l, qk_l, vs_l, q_l, k_l, v_l = _mix_in(
            xl, mod[0], mod[1], g_pre, w_in_pad, w_uq_pad, w_ukv_perm, mla_q_norm[i], mla_kv_norm[i],
            tabs, i, tm_p, lat_row(tm_p), True, SEQ)
        u_c, qk_c, vs_c, q_c, k_c, v_c = _mix_in(
            xc, mod[0], mod[1], g_pre, w_in_pad, w_uq_pad, w_ukv_perm, mla_q_norm[i], mla_kv_norm[i],
            tabs, i, tm_p, ctx_row, False, tm_p)

        s5_c, s5_l = _s5_mixer(u_c, u_l, s5_w, s5_d[i], glu_w_b, s5_glu_b[i], i, need_ctx)

        swa_l = _swa_latent(qk_l, vs_l, qk_c, vs_c, swa_sink[i])

        mla_l = _mla_attn(q_l, [k_c, k_l], [v_c, v_l], SEQ, 512)

        xl = _out_proj(s5_l, swa_l, mla_l, w_out_b, xl, norm_mix_post[i], mod[2], i, tm_p, lat_row(tm_p))
        xl = _ffn(xl, mod[3], mod[4], norm_ffn_pre[i], ffn_w1, ffn_w2, norm_ffn_post[i], mod[5],
                  i, tm_f, lat_row(tm_f))

        if need_ctx:
            swa_c = _swa_context(qk_c, vs_c, swa_sink[i])
            mla_c = _mla_attn(q_c, [k_c], [v_c], CTX_LEN, 256)
            xc = _out_proj(s5_c, swa_c, mla_c, w_out_b, xc, norm_mix_post[i], mod[2], i, tm_p, ctx_row)
            xc = _ffn(xc, mod[3], mod[4], norm_ffn_pre[i], ffn_w1, ffn_w2, norm_ffn_post[i], mod[5],
                      i, tm_f, ctx_row)

    return xl.reshape(BATCH, SEQ, D_MODEL)
```

```python
import functools
import math

import jax
import jax.numpy as jnp
from jax import lax
from jax.experimental import pallas as pl
from jax.experimental.pallas import tpu as pltpu

F32 = jnp.float32
BF16 = jnp.bfloat16

D_MODEL = 2048
BATCH = 4
SEQ = 2048
DEPTH = 2
GRID_W = 64
CTX_LEN = 256
EPS = 1e-6
ROPE_BASE = 10000.0
NEG_INF = -1e30
BLOCK = 128
HEAD_DIM = 128
S5_WIDTH = 512
S5_GROUP = 16
S5_GROUPS = 32
S5_STATE = 64
S5_LANES = S5_GROUPS * S5_STATE
SWA_HEADS = 6
SWA_KV_HEADS = 2
SWA_REP = SWA_HEADS // SWA_KV_HEADS
MLA_HEADS = 6
MLA_Q_RANK = 768
MLA_KV_RANK = 512
MLA_NOPE = 128
MLA_ROPE = 64
MLA_V = 128
MLA_QK_PAD = 256
MLA_V_PAD = 256
D_FF = 4 * D_MODEL
IN_WIDTHS = (512, 768, 256, 256, 768, 512, 64)
IN_WIDTH = sum(IN_WIDTHS)
OFF_U, OFF_QS, OFF_KS, OFF_VS, OFF_CQ, OFF_CKV, OFF_KR = 0, 512, 1280, 1536, 1792, 2560, 3072

VMEM_LIMIT = 56 * 1024 * 1024


def _cparams(*sem):
    return pltpu.CompilerParams(dimension_semantics=sem, vmem_limit_bytes=VMEM_LIMIT)


def _dot(a, b):
    return jnp.dot(a, b, preferred_element_type=F32)


def _dot_nt(a, b):
    return lax.dot_general(a, b, (((1,), (1,)), ((), ())), preferred_element_type=F32)


def _rms(x, g):
    return x * lax.rsqrt(jnp.mean(x * x, axis=-1, keepdims=True) + EPS) * g


def _rope(x, cos, sin, half):
    lane = lax.broadcasted_iota(jnp.int32, x.shape, 1)
    fwd = pltpu.roll(x, 128 - half, 1)
    bwd = pltpu.roll(x, half, 1)
    sw = jnp.where((lane % (2 * half)) < half, fwd, bwd)
    return x * cos + sw * sin


def _ada_kernel(c_ref, w_ref, b_ref, o_ref):
    c = c_ref[...]
    s = c * jax.nn.sigmoid(c)
    o_ref[0] = _dot(s.astype(BF16), w_ref[0].astype(BF16)) + b_ref[0]


def _ada(cvec, ada_w, ada_b):
    tn = 1024
    n = 6 * D_MODEL
    return pl.pallas_call(
        _ada_kernel,
        grid=(DEPTH, n // tn),
        in_specs=[pl.BlockSpec((8, D_MODEL), lambda l, j: (0, 0)),
                  pl.BlockSpec((1, D_MODEL, tn), lambda l, j: (l, 0, j)),
                  pl.BlockSpec((1, 1, tn), lambda l, j: (l, 0, j))],
        out_specs=pl.BlockSpec((1, 8, tn), lambda l, j: (l, 0, j)),
        out_shape=jax.ShapeDtypeStruct((DEPTH, 8, n), F32),
        compiler_params=_cparams("arbitrary", "arbitrary"),
        name="ada",
    )(cvec, ada_w, ada_b.reshape(DEPTH, 1, n))


IN_COLS = 3200
SWA_QK_W = (SWA_HEADS + SWA_KV_HEADS) * HEAD_DIM


def _mix_in_kernel(x_ref, shift_ref, scale_ref, g_ref, w_ref, wq_ref, wkv_ref, qn_ref, kvn_ref,
                   cs_ref, ss_ref, cm_ref, sm_ref,
                   u_ref, qk_ref, vs_ref, q_ref, k_ref, v_ref, h_ref, *, rope):
    h = _rms(x_ref[...], g_ref[...]) * (1.0 + scale_ref[0]) + shift_ref[0]
    h_ref[...] = h.astype(BF16)

    def proj(lo, width):
        return _dot(h_ref[...], w_ref[:, lo:lo + width])

    u_ref[...] = proj(OFF_U, S5_WIDTH)

    swa_scale = HEAD_DIM ** -0.5
    per = 4
    for c in range(SWA_QK_W // (per * HEAD_DIM)):
        zc = proj(OFF_QS + c * per * HEAD_DIM, per * HEAD_DIM)
        for hh in range(per):
            head = c * per + hh
            xh = zc[:, hh * HEAD_DIM:(hh + 1) * HEAD_DIM]
            if rope:
                xh = _rope(xh, cs_ref[...], ss_ref[...], HEAD_DIM // 4)
            if head < SWA_HEADS:
                xh = xh * swa_scale
            qk_ref[:, head * HEAD_DIM:(head + 1) * HEAD_DIM] = xh.astype(BF16)

    vs_ref[...] = proj(OFF_VS, SWA_KV_HEADS * HEAD_DIM).astype(BF16)

    mla_scale = (MLA_NOPE + MLA_ROPE) ** -0.5
    cq = proj(OFF_CQ, MLA_Q_RANK)
    q = _dot(_rms(cq, qn_ref[...]).astype(BF16), wq_ref[...])
    for hd in range(MLA_HEADS):
        lo = hd * MLA_QK_PAD
        q_ref[:, lo:lo + MLA_NOPE] = (q[:, lo:lo + MLA_NOPE] * mla_scale).astype(BF16)
        r = q[:, lo + MLA_NOPE:lo + MLA_QK_PAD]
        if rope:
            r = _rope(r, cm_ref[...], sm_ref[...], MLA_ROPE // 4)
        q_ref[:, lo + MLA_NOPE:lo + MLA_QK_PAD] = (r * mla_scale).astype(BF16)

    ckv = proj(OFF_CKV, MLA_KV_RANK)
    kv = _dot(_rms(ckv, kvn_ref[...]).astype(BF16), wkv_ref[...])
    kr = proj(OFF_KR, 128)
    if rope:
        kr = _rope(kr, cm_ref[...], sm_ref[...], MLA_ROPE // 4)
    kr = kr.astype(BF16)
    for hd in range(MLA_HEADS):
        lo = hd * MLA_QK_PAD
        k_ref[:, lo:lo + MLA_NOPE] = kv[:, hd * MLA_NOPE:(hd + 1) * MLA_NOPE].astype(BF16)
        k_ref[:, lo + MLA_NOPE:lo + MLA_QK_PAD] = kr
        vlo = hd * MLA_V_PAD
        vh = kv[:, MLA_HEADS * MLA_NOPE + hd * MLA_V:MLA_HEADS * MLA_NOPE + (hd + 1) * MLA_V].astype(BF16)
        v_ref[:, vlo:vlo + MLA_V] = vh
        v_ref[:, vlo + MLA_V:vlo + MLA_V_PAD] = jnp.ones_like(vh)


def _mix_in(x2d, shift, scale, g, w_in, w_uq, w_ukv, q_norm, kv_norm, tabs, layer, tm, mod_row, rope, seq):
    rows = x2d.shape[0]
    once = pl.Buffered(1)
    tab = pl.BlockSpec((tm, 128), lambda i: (i % (seq // tm), 0))

    def res(shape):
        return pl.BlockSpec((None,) + shape, lambda i: (layer, 0, 0), pipeline_mode=once)

    def out(width):
        return pl.BlockSpec((tm, width), lambda i: (i, 0))

    widths = (S5_WIDTH, SWA_QK_W, SWA_KV_HEADS * HEAD_DIM,
              MLA_HEADS * MLA_QK_PAD, MLA_HEADS * MLA_QK_PAD, MLA_HEADS * MLA_V_PAD)
    dtypes = (F32, BF16, BF16, BF16, BF16, BF16)
    return pl.pallas_call(
        functools.partial(_mix_in_kernel, rope=rope),
        grid=(rows // tm,),
        in_specs=[pl.BlockSpec((tm, D_MODEL), lambda i: (i, 0)),
                  pl.BlockSpec((1, 1, D_MODEL), lambda i: (mod_row(i), 0, 0)),
                  pl.BlockSpec((1, 1, D_MODEL), lambda i: (mod_row(i), 0, 0)),
                  pl.BlockSpec((1, D_MODEL), lambda i: (0, 0)),
                  res((D_MODEL, IN_COLS)),
                  res((MLA_Q_RANK, MLA_HEADS * MLA_QK_PAD)),
                  res((MLA_KV_RANK, MLA_HEADS * (MLA_NOPE + MLA_V))),
                  pl.BlockSpec((1, MLA_Q_RANK), lambda i: (0, 0)),
                  pl.BlockSpec((1, MLA_KV_RANK), lambda i: (0, 0)),
                  tab, tab, tab, tab],
        out_specs=[out(w) for w in widths],
        out_shape=[jax.ShapeDtypeStruct((rows, w), dt) for w, dt in zip(widths, dtypes)],
        scratch_shapes=[pltpu.VMEM((tm, D_MODEL), BF16)],
        compiler_params=_cparams("arbitrary"),
        name="mix_in",
    )(x2d, shift, scale, g, w_in, w_uq, w_ukv, q_norm.reshape(1, -1), kv_norm.reshape(1, -1), *tabs)


S5_GH = S5_GROUPS // 2


def _zoh(are, aim, ldt):
    dt = jnp.exp(ldt)
    mag = jnp.exp(are * dt)
    lr = mag * jnp.cos(aim * dt)
    li = mag * jnp.sin(aim * dt)
    den = are * are + aim * aim
    nr = lr - 1.0
    return lr, li, (nr * are + li * aim) / den, (li * are - nr * aim) / den


def _s5_disc_kernel(are_ref, aim_ref, ldt_ref, arec_ref, aimc_ref, ldtc_ref, br_ref, bi_ref, cr_ref, ci_ref,
                    lam_ref, wbr_ref, wbi_ref, wcr_ref, wci_ref):
    lr, li, _, _ = _zoh(are_ref[0], aim_ref[0], ldt_ref[0])
    lam_ref[0, 0:1, :] = lr
    lam_ref[0, 1:2, :] = li
    gh, n, p = S5_GH, S5_STATE, S5_GROUP
    kin, kst = gh * p, gh * n

    def iota(shape, axis):
        return lax.broadcasted_iota(jnp.int32, shape, axis)

    tile_n = jnp.where(iota((n, kst), 0) == (iota((n, kst), 1) & (n - 1)), 1.0, 0.0).astype(BF16)
    tile_p = jnp.where(iota((p, kin), 0) == (iota((p, kin), 1) & (p - 1)), 1.0, 0.0).astype(BF16)
    lp, ln = p.bit_length() - 1, n.bit_length() - 1
    diag_b = (iota((kin, kst), 0) >> lp) == (iota((kin, kst), 1) >> ln)
    diag_c = (iota((kst, kin), 0) >> ln) == (iota((kst, kin), 1) >> lp)
    for h in range(2):
        _, _, fr, fi = _zoh(arec_ref[0, h], aimc_ref[0, h], ldtc_ref[0, h])
        fr = jnp.broadcast_to(fr[:, None, :], (gh, p, n)).reshape(kin, n)
        fi = jnp.broadcast_to(fi[:, None, :], (gh, p, n)).reshape(kin, n)
        bbr = (fr * br_ref[0, h] - fi * bi_ref[0, h]).astype(BF16)
        bbi = (fr * bi_ref[0, h] + fi * br_ref[0, h]).astype(BF16)
        wbr_ref[0, h] = jnp.where(diag_b, _dot(bbr, tile_n), 0.0).astype(BF16)
        wbi_ref[0, h] = jnp.where(diag_b, _dot(bbi, tile_n), 0.0).astype(BF16)
        wcr_ref[0, h] = jnp.where(diag_c, _dot(cr_ref[0, h].astype(BF16), tile_p), 0.0).astype(BF16)
        wci_ref[0, h] = jnp.where(diag_c, _dot(ci_ref[0, h].astype(BF16), tile_p), 0.0).astype(BF16)


def _s5_disc(a_re, a_im, log_dt, b_re, b_im, c_re, c_im):
    nd = DEPTH * 2
    gh, n, p = S5_GH, S5_STATE, S5_GROUP
    kin, kst = gh * p, gh * n
    are = a_re.reshape(nd, 1, S5_LANES)
    aim = a_im.reshape(nd, 1, S5_LANES)
    ldt_gn = jnp.broadcast_to(log_dt[..., None], a_re.shape)
    ldt = ldt_gn.reshape(nd, 1, S5_LANES)
    compact = lambda a: a.reshape(nd, 2, gh, n)
    bt = lambda b: b.reshape(nd, 2, gh, n, p).transpose(0, 1, 2, 4, 3).reshape(nd, 2, kin, n)
    ct = lambda c: c.reshape(nd, 2, gh, p, n).transpose(0, 1, 2, 4, 3).reshape(nd, 2, kst, p)
    vec = pl.BlockSpec((1, 1, S5_LANES), lambda d: (d, 0, 0))
    cvec = pl.BlockSpec((1, 2, gh, n), lambda d: (d, 0, 0, 0))
    bin_spec = pl.BlockSpec((1, 2, kin, n), lambda d: (d, 0, 0, 0))
    cin_spec = pl.BlockSpec((1, 2, kst, p), lambda d: (d, 0, 0, 0))
    bspec = pl.BlockSpec((1, 2, kin, kst), lambda d: (d, 0, 0, 0))
    cspec = pl.BlockSpec((1, 2, kst, kin), lambda d: (d, 0, 0, 0))
    return pl.pallas_call(
        _s5_disc_kernel,
        grid=(nd,),
        in_specs=[vec, vec, vec, cvec, cvec, cvec, bin_spec, bin_spec, cin_spec, cin_spec],
        out_specs=[pl.BlockSpec((1, 2, S5_LANES), lambda d: (d, 0, 0)), bspec, bspec, cspec, cspec],
        out_shape=[jax.ShapeDtypeStruct((nd, 2, S5_LANES), F32),
                   jax.ShapeDtypeStruct((nd, 2, kin, kst), BF16),
                   jax.ShapeDtypeStruct((nd, 2, kin, kst), BF16),
                   jax.ShapeDtypeStruct((nd, 2, kst, kin), BF16),
                   jax.ShapeDtypeStruct((nd, 2, kst, kin), BF16)],
        compiler_params=_cparams("arbitrary"),
        name="s5_disc",
    )(are, aim, ldt, compact(a_re), compact(a_im), compact(ldt_gn),
      bt(b_re), bt(b_im), ct(c_re), ct(c_im))


S5_TC = 128
S5_CTX_CHUNKS = CTX_LEN // S5_TC
S5_LAT_CHUNKS = SEQ // S5_TC


def _reverse_time(x):
    bsz, tc, w = x.shape
    g = x.reshape(bsz * tc // 8, 8, w)
    sub = lax.broadcasted_iota(jnp.int32, g.shape, 1)
    g = pltpu.roll(g, 4, 1)
    g = jnp.where((sub & 2) == 0, pltpu.roll(g, 6, 1), pltpu.roll(g, 2, 1))
    g = jnp.where((sub & 1) == 0, pltpu.roll(g, 7, 1), pltpu.roll(g, 1, 1))
    g = g.reshape(bsz, tc // 8, 8, w)
    g = jnp.concatenate([g[:, k:k + 1] for k in reversed(range(tc // 8))], axis=1)
    return g.reshape(bsz, tc, w)


def _s5_scan_kernel(ucf_ref, ulf_ref, ucb_ref, ulb_ref, lam_ref, wbr_ref, wbi_ref, wcr_ref, wci_ref,
                    yf_ref, yb_ref, hr_ref, hi_ref, sr_ref, si_ref):
    c = pl.program_id(0)
    tc = S5_TC
    kin, kst = S5_WIDTH // 2, S5_LANES // 2
    nlb = kst // 128

    @pl.when(c == 0)
    def _():
        sr_ref[...] = jnp.zeros_like(sr_ref)
        si_ref[...] = jnp.zeros_like(si_ref)

    def project(u, d, row0):
        u = u.reshape(BATCH * tc, S5_WIDTH).astype(BF16)
        for h in range(2):
            uh = u[:, h * kin:(h + 1) * kin]
            for w_ref, dst in ((wbr_ref, hr_ref), (wbi_ref, hi_ref)):
                p = _dot(uh, w_ref[d, h])
                for j in range(nlb):
                    for b in range(BATCH):
                        dst[h * nlb + j, pl.ds(row0 + b, tc, stride=8), :] = (
                            p[b * tc:(b + 1) * tc, j * 128:(j + 1) * 128])

    in_ctx = c < S5_CTX_CHUNKS
    project(jnp.where(in_ctx, ucf_ref[...], ulf_ref[...]), 0, 0)
    project(_reverse_time(jnp.where(in_ctx, ucb_ref[...], ulb_ref[...])), 1, BATCH)

    nb = 8
    for lb in range(2 * nlb // nb):
        ls = slice(lb * nb, (lb + 1) * nb)
        fwd_rows = lax.broadcasted_iota(jnp.int32, (nb, 8, 128), 1) < BATCH
        l_r = jnp.where(fwd_rows, lam_ref[0, 0, ls], lam_ref[1, 0, ls])
        l_i = jnp.where(fwd_rows, lam_ref[0, 1, ls], lam_ref[1, 1, ls])

        def body(k, carry, ls=ls, l_r=l_r, l_i=l_i):
            s_r, s_i = carry
            rows = pl.ds(pl.multiple_of(k * 8, 8), 8)
            n_r = l_r * s_r - l_i * s_i + hr_ref[ls, rows, :]
            n_i = l_r * s_i + l_i * s_r + hi_ref[ls, rows, :]
            hr_ref[ls, rows, :] = n_r
            hi_ref[ls, rows, :] = n_i
            return n_r, n_i

        s_r, s_i = lax.fori_loop(0, tc, body, (sr_ref[ls], si_ref[ls]), unroll=True)
        sr_ref[ls] = s_r
        si_ref[ls] = s_i

    def gather_rows(src, h, row0):
        return jnp.concatenate(
            [jnp.concatenate([src[h * nlb + j, pl.ds(row0 + b, tc, stride=8), :] for j in range(nlb)], axis=1)
             for b in range(BATCH)], axis=0)

    def readout(d, row0):
        ys = []
        for h in range(2):
            hr = gather_rows(hr_ref, h, row0).astype(BF16)
            hi = gather_rows(hi_ref, h, row0).astype(BF16)
            ys.append((_dot(hr, wcr_ref[d, h]) - _dot(hi, wci_ref[d, h])).reshape(BATCH, tc, kin))
        return jnp.concatenate(ys, axis=2)

    yf_ref[...] = readout(0, 0)
    yb_ref[...] = _reverse_time(readout(1, BATCH))


def _s5_scan(u_c, u_l, lam, wbr, wbi, wcr, wci, layer, skip_chunks):
    tc = S5_TC
    nchunk = S5_CTX_CHUNKS + S5_LAT_CHUNKS
    kin, kst = S5_WIDTH // 2, S5_LANES // 2
    zc3 = u_c.reshape(BATCH, CTX_LEN, S5_WIDTH)
    zl3 = u_l.reshape(BATCH, SEQ, S5_WIDTH)

    def bwd_chunk(c):
        return jnp.where(c < S5_CTX_CHUNKS, S5_CTX_CHUNKS - 1 - c, nchunk + S5_CTX_CHUNKS - 1 - c)

    def ctx_spec(chunk_of):
        return pl.BlockSpec((BATCH, tc, S5_WIDTH),
                            lambda c: (0, jnp.clip(chunk_of(c), 0, S5_CTX_CHUNKS - 1), 0))

    def lat_spec(chunk_of):
        return pl.BlockSpec((BATCH, tc, S5_WIDTH),
                            lambda c: (0, jnp.clip(chunk_of(c) - S5_CTX_CHUNKS, 0, S5_LAT_CHUNKS - 1), 0))

    def out_block(chunk):
        return jnp.where(chunk >= S5_CTX_CHUNKS, chunk - S5_CTX_CHUNKS, S5_LAT_CHUNKS + chunk)

    def out_spec(chunk_of):
        return pl.BlockSpec((BATCH, tc, S5_WIDTH),
                            lambda c: (0, out_block(chunk_of(jnp.maximum(c, skip_chunks))), 0))

    once = pl.Buffered(1)
    bspec = pl.BlockSpec((2, 2, kin, kst), lambda c: (layer, 0, 0, 0), pipeline_mode=once)
    cspec = pl.BlockSpec((2, 2, kst, kin), lambda c: (layer, 0, 0, 0), pipeline_mode=once)
    t_out = (nchunk - skip_chunks) * tc
    nblk = S5_LANES // 128
    big = pltpu.VMEM((nblk, tc * 8, 128), F32)
    state = pltpu.VMEM((nblk, 8, 128), F32)
    lam = lam.reshape(DEPTH * 2, 2, nblk, 1, 128)
    fwd_chunk = lambda c: c
    return pl.pallas_call(
        _s5_scan_kernel,
        grid=(nchunk,),
        in_specs=[ctx_spec(fwd_chunk), lat_spec(fwd_chunk), ctx_spec(bwd_chunk), lat_spec(bwd_chunk),
                  pl.BlockSpec((2, 2, nblk, 1, 128), lambda c: (layer, 0, 0, 0, 0)),
                  bspec, bspec, cspec, cspec],
        out_specs=[out_spec(fwd_chunk), out_spec(bwd_chunk)],
        out_shape=[jax.ShapeDtypeStruct((BATCH, t_out, S5_WIDTH), F32)] * 2,
        scratch_shapes=[big, big, state, state],
        compiler_params=_cparams("arbitrary"),
        name="s5_scan",
    )(zc3, zl3, zc3, zl3, lam, wbr, wbi, wcr, wci)


def _sink_rows(sink_ref, g, rows):
    r = lax.broadcasted_iota(jnp.int32, (rows, 1), 0) // (rows // SWA_REP)
    s0, s1, s2 = sink_ref[SWA_REP * g], sink_ref[SWA_REP * g + 1], sink_ref[SWA_REP * g + 2]
    return jnp.where(r == 0, s0, jnp.where(r == 1, s1, s2))


def _stack_heads(x, n):
    return jnp.concatenate([x[:, h * HEAD_DIM:(h + 1) * HEAD_DIM] for h in range(n)], axis=0)


def _unstack_heads(x, n):
    rows = x.shape[0] // n
    return jnp.concatenate([x[h * rows:(h + 1) * rows] for h in range(n)], axis=1)


SWA_QB = 4


def _with_ones(v):
    return jnp.concatenate([v, jnp.ones_like(v)], axis=1)


def _swa_latent_kernel(sink_ref, q_ref, kp_ref, km_ref, kn_ref, vp_ref, vm_ref, vn_ref,
                       kx_ref, vx_ref, o_ref):
    n = pl.program_id(1)
    rows = SWA_REP * BLOCK
    qi = lax.broadcasted_iota(jnp.int32, (rows, 3 * BLOCK), 0) % BLOCK
    kj = lax.broadcasted_iota(jnp.int32, (rows, 3 * BLOCK), 1)
    in_window = (kj >= qi) & (kj <= qi + 2 * BLOCK)
    for g in range(SWA_KV_HEADS):
        gs = slice(g * HEAD_DIM, (g + 1) * HEAD_DIM)
        kspan = jnp.concatenate([kp_ref[:, gs], km_ref[:, gs], kn_ref[:, gs]], axis=0)
        vspan = _with_ones(jnp.concatenate([vp_ref[:, gs], vm_ref[:, gs], vn_ref[:, gs]], axis=0))
        kx = kx_ref[:, gs]
        vx = _with_ones(vx_ref[:, gs])
        sk = _sink_rows(sink_ref, g, rows)
        for j in range(SWA_QB):
            blk = n * SWA_QB + j
            qs = slice(g * SWA_REP * HEAD_DIM, (g + 1) * SWA_REP * HEAD_DIM)
            q = _stack_heads(q_ref[j * BLOCK:(j + 1) * BLOCK, qs], SWA_REP)
            kb = kspan[j * BLOCK:(j + 3) * BLOCK]
            vb = vspan[j * BLOCK:(j + 3) * BLOCK]
            s_ctx = _dot_nt(q, kx)
            s_band = _dot_nt(q, kb)
            kpos = (blk - 1) * BLOCK + kj
            valid = in_window & (kpos >= 0) & (kpos < SEQ)
            s_band = jnp.where(valid, s_band, NEG_INF)
            m = jnp.maximum(jnp.maximum(jnp.max(s_ctx, axis=-1, keepdims=True),
                                        jnp.max(s_band, axis=-1, keepdims=True)), sk)
            p_ctx = jnp.exp(s_ctx - m).astype(BF16)
            p_band = jnp.exp(s_band - m).astype(BF16)
            o = _dot(p_ctx, vx) + _dot(p_band, vb)
            den = o[:, HEAD_DIM:] + jnp.exp(sk - m)
            o_ref[j * BLOCK:(j + 1) * BLOCK, qs] = _unstack_heads(
                o[:, :HEAD_DIM] / den, SWA_REP).astype(o_ref.dtype)


def _swa_latent(qk, vs, qk_c, vs_c, sink):
    nb = SEQ // BLOCK
    ng = nb // SWA_QB
    span = SWA_QB * BLOCK
    kvw = SWA_KV_HEADS * HEAD_DIM
    kcol = SWA_HEADS * HEAD_DIM // kvw
    vcol = 0
    kxcol = kcol

    def edge(col, blk_of):
        return pl.BlockSpec((BLOCK, kvw), lambda b, n, s: (b * nb + jnp.clip(blk_of(n), 0, nb - 1), col))

    def main(col):
        return pl.BlockSpec((span, kvw), lambda b, n, s: (b * ng + n, col))

    prev_blk = lambda n: n * SWA_QB - 1
    next_blk = lambda n: (n + 1) * SWA_QB
    grid_spec = pltpu.PrefetchScalarGridSpec(
        num_scalar_prefetch=1,
        grid=(BATCH, ng),
        in_specs=[pl.BlockSpec((span, SWA_HEADS * HEAD_DIM), lambda b, n, s: (b * ng + n, 0)),
                  edge(kcol, prev_blk), main(kcol), edge(kcol, next_blk),
                  edge(vcol, prev_blk), main(vcol), edge(vcol, next_blk),
                  pl.BlockSpec((CTX_LEN, kvw), lambda b, n, s: (b, kxcol)),
                  pl.BlockSpec((CTX_LEN, kvw), lambda b, n, s: (b, vcol))],
        out_specs=pl.BlockSpec((span, SWA_HEADS * HEAD_DIM), lambda b, n, s: (b * ng + n, 0)),
    )
    return pl.pallas_call(
        _swa_latent_kernel,
        grid_spec=grid_spec,
        out_shape=jax.ShapeDtypeStruct((BATCH * SEQ, SWA_HEADS * HEAD_DIM), BF16),
        compiler_params=_cparams("arbitrary", "arbitrary"),
        name="swa_latent",
    )(sink, qk, qk, qk, qk, vs, vs, vs, qk_c, vs_c)


def _swa_context_kernel(sink_ref, q_ref, k_ref, v_ref, o_ref):
    g = pl.program_id(1)
    q = _stack_heads(q_ref[...], SWA_REP)
    s = _dot_nt(q, k_ref[...])
    sk = _sink_rows(sink_ref, g, SWA_REP * CTX_LEN)
    m = jnp.maximum(jnp.max(s, axis=-1, keepdims=True), sk)
    p = jnp.exp(s - m)
    den = jnp.sum(p, axis=-1, keepdims=True) + jnp.exp(sk - m)
    o = _dot(p.astype(BF16), v_ref[...]) / den
    o_ref[...] = _unstack_heads(o, SWA_REP).astype(o_ref.dtype)


def _swa_context(qk_c, vs_c, sink):
    qw = SWA_REP * HEAD_DIM
    grid_spec = pltpu.PrefetchScalarGridSpec(
        num_scalar_prefetch=1,
        grid=(BATCH, SWA_KV_HEADS),
        in_specs=[pl.BlockSpec((CTX_LEN, qw), lambda b, g, s: (b, g)),
                  pl.BlockSpec((CTX_LEN, HEAD_DIM), lambda b, g, s: (b, SWA_HEADS + g)),
                  pl.BlockSpec((CTX_LEN, HEAD_DIM), lambda b, g, s: (b, g))],
        out_specs=pl.BlockSpec((CTX_LEN, qw), lambda b, g, s: (b, g)),
    )
    return pl.pallas_call(
        _swa_context_kernel,
        grid_spec=grid_spec,
        out_shape=jax.ShapeDtypeStruct((BATCH * CTX_LEN, SWA_HEADS * HEAD_DIM), BF16),
        compiler_params=_cparams("arbitrary", "arbitrary"),
        name="swa_context",
    )(sink, qk_c, qk_c, vs_c)


def _mla_attn_kernel(*refs, nseg):
    q_ref = refs[0]
    k_refs = refs[1:1 + nseg]
    v_refs = refs[1 + nseg:1 + 2 * nseg]
    o_ref = refs[1 + 2 * nseg]
    for h in range(MLA_HEADS):
        q = q_ref[:, h * MLA_QK_PAD:(h + 1) * MLA_QK_PAD]
        s = [_dot_nt(q, k[:, h * MLA_QK_PAD:(h + 1) * MLA_QK_PAD]) for k in k_refs]
        m = functools.reduce(jnp.maximum, [jnp.max(x, axis=-1, keepdims=True) for x in s])
        p = [jnp.exp(x - m).astype(BF16) for x in s]
        o = sum(_dot(x, v[:, h * MLA_V_PAD:(h + 1) * MLA_V_PAD]) for x, v in zip(p, v_refs))
        o_ref[:, h * MLA_V:(h + 1) * MLA_V] = (o[:, :MLA_V] / o[:, MLA_V:]).astype(o_ref.dtype)


def _mla_attn(q, ks, vs, n_q, tq):
    nseg = len(ks)
    nq = n_q // tq
    lens = [k.shape[0] // BATCH for k in ks]
    kw, vw = MLA_HEADS * MLA_QK_PAD, MLA_HEADS * MLA_V
    in_specs = [pl.BlockSpec((tq, kw), lambda b, i: (b * nq + i, 0))]
    in_specs += [pl.BlockSpec((n, kw), lambda b, i: (b, 0)) for n in lens]
    in_specs += [pl.BlockSpec((n, MLA_HEADS * MLA_V_PAD), lambda b, i: (b, 0)) for n in lens]
    return pl.pallas_call(
        functools.partial(_mla_attn_kernel, nseg=nseg),
        grid=(BATCH, nq),
        in_specs=in_specs,
        out_specs=pl.BlockSpec((tq, vw), lambda b, i: (b * nq + i, 0)),
        out_shape=jax.ShapeDtypeStruct((BATCH * n_q, vw), BF16),
        compiler_params=_cparams("arbitrary", "arbitrary"),
        name="mla_attn",
    )(q, *ks, *vs)


def _out_proj_kernel(u_ref, yf_ref, yb_ref, d_ref, gw_ref, gb_ref, a2_ref, a3_ref, w_ref, x_ref, g_ref, gate_ref,
                     o_ref):
    y = d_ref[...] * u_ref[0] + yf_ref[0] + yb_ref[0]
    gl = jax.nn.gelu(y)
    a1 = (gl * jax.nn.sigmoid(_dot(gl.astype(BF16), gw_ref[...]) + gb_ref[...])).astype(BF16)
    nc = 512
    k1 = a1.shape[1]
    k2 = k1 + a2_ref.shape[1]
    for n in range(D_MODEL // nc):
        cs = slice(n * nc, (n + 1) * nc)
        o_ref[:, cs] = (_dot(a1, w_ref[0:k1, cs]) + _dot(a2_ref[...], w_ref[k1:k2, cs])
                        + _dot(a3_ref[...], w_ref[k2:D_MODEL, cs]))
    o_ref[...] = x_ref[...] + gate_ref[0] * _rms(o_ref[...], g_ref[...])


def _out_proj(u3, yf, yb, y_off, d, glu_w, glu_b, a2, a3, w, x2d, g, gate, layer, tm, mod_row):
    rows = a2.shape[0]
    per_b = u3.shape[1] // tm
    once = pl.Buffered(1)
    uspec = pl.BlockSpec((1, tm, S5_WIDTH), lambda i: (i // per_b, i % per_b, 0))
    yspec = pl.BlockSpec((1, tm, S5_WIDTH), lambda i: (i // per_b, i % per_b + y_off // tm, 0))
    vec5 = pl.BlockSpec((1, S5_WIDTH), lambda i: (0, 0))
    return pl.pallas_call(
        _out_proj_kernel,
        grid=(rows // tm,),
        in_specs=[uspec, yspec, yspec, vec5,
                  pl.BlockSpec((None, S5_WIDTH, S5_WIDTH), lambda i: (layer, 0, 0), pipeline_mode=once), vec5,
                  pl.BlockSpec((tm, a2.shape[1]), lambda i: (i, 0)),
                  pl.BlockSpec((tm, a3.shape[1]), lambda i: (i, 0)),
                  pl.BlockSpec((None, D_MODEL, D_MODEL), lambda i: (layer, 0, 0), pipeline_mode=once),
                  pl.BlockSpec((tm, D_MODEL), lambda i: (i, 0)),
                  pl.BlockSpec((1, D_MODEL), lambda i: (0, 0)),
                  pl.BlockSpec((1, 1, D_MODEL), lambda i: (mod_row(i), 0, 0))],
        out_specs=pl.BlockSpec((tm, D_MODEL), lambda i: (i, 0)),
        out_shape=jax.ShapeDtypeStruct((rows, D_MODEL), F32),
        compiler_params=_cparams("arbitrary"),
        name="out_proj",
    )(u3, yf, yb, d.reshape(1, -1), glu_w, glu_b.reshape(1, -1), a2, a3, w, x2d, g.reshape(1, -1), gate)


def _ffn_kernel(x_ref, shift_ref, scale_ref, gpre_ref, w1_ref, w2_ref, gpost_ref, gate_ref, o_ref,
                f_ref, h_ref):
    j = pl.program_id(1)

    @pl.when(j == 0)
    def _():
        f = _rms(x_ref[...], gpre_ref[...]) * (1.0 + scale_ref[0]) + shift_ref[0]
        f_ref[...] = f.astype(BF16)
        o_ref[...] = jnp.zeros_like(o_ref)

    h = _dot(f_ref[...], w1_ref[...].astype(BF16))
    h_ref[...] = jnp.square(jnp.maximum(h, 0.0)).astype(BF16)
    nc = 512
    for n in range(D_MODEL // nc):
        cs = slice(n * nc, (n + 1) * nc)
        o_ref[:, cs] += _dot(h_ref[...], w2_ref[:, cs].astype(BF16))

    @pl.when(j == pl.num_programs(1) - 1)
    def _():
        o_ref[...] = x_ref[...] + gate_ref[0] * _rms(o_ref[...], gpost_ref[...])


def _ffn(x2d, shift, scale, gpre, w1, w2, gpost, gate, layer, tm, mod_row):
    rows = x2d.shape[0]
    tf = 512
    mod = pl.BlockSpec((1, 1, D_MODEL), lambda i, j: (mod_row(i), 0, 0))
    vec = pl.BlockSpec((1, D_MODEL), lambda i, j: (0, 0))
    once = pl.Buffered(1)
    return pl.pallas_call(
        _ffn_kernel,
        grid=(rows // tm, D_FF // tf),
        in_specs=[pl.BlockSpec((tm, D_MODEL), lambda i, j: (i, 0)), mod, mod, vec,
                  pl.BlockSpec((None, D_MODEL, tf), lambda i, j: (layer, 0, j)),
                  pl.BlockSpec((None, tf, D_MODEL), lambda i, j: (layer, j, 0)),
                  vec, mod],
        out_specs=pl.BlockSpec((tm, D_MODEL), lambda i, j: (i, 0), pipeline_mode=once),
        out_shape=jax.ShapeDtypeStruct((rows, D_MODEL), F32),
        scratch_shapes=[pltpu.VMEM((tm, D_MODEL), BF16), pltpu.VMEM((tm, tf), BF16)],
        compiler_params=_cparams("arbitrary", "arbitrary"),
        name="ffn",
    )(x2d, shift, scale, gpre.reshape(1, -1), w1, w2, gpost.reshape(1, -1), gate)


def _rope_tables():
    t = jnp.arange(SEQ)
    row = (t // GRID_W).astype(F32)[:, None]
    col = (t % GRID_W).astype(F32)[:, None]

    def tables(rot_dim):
        quarter = rot_dim // 4
        inv_freq = ROPE_BASE ** (-jnp.arange(quarter, dtype=F32) / quarter)
        ar, ac = row * inv_freq, col * inv_freq
        cos = jnp.concatenate([jnp.cos(ar), jnp.cos(ar), jnp.cos(ac), jnp.cos(ac)], axis=1)
        sin = jnp.concatenate([-jnp.sin(ar), jnp.sin(ar), -jnp.sin(ac), jnp.sin(ac)], axis=1)
        pad = 128 - rot_dim
        if pad:
            cos = jnp.concatenate([cos, jnp.ones((SEQ, pad), F32)], axis=1)
            sin = jnp.concatenate([sin, jnp.zeros((SEQ, pad), F32)], axis=1)
        return cos, sin

    return tables(HEAD_DIM), tables(MLA_ROPE)


def kernel(x, c, ctx, c_ctx, ada_w, ada_b, norm_mix_pre, norm_mix_post, norm_ffn_pre, norm_ffn_post, w_in, w_out, s5_a_re, s5_a_im, s5_log_dt, s5_b_re, s5_b_im, s5_c_re, s5_c_im, s5_d, s5_glu_w, s5_glu_b, swa_sink, mla_q_norm, mla_w_uq, mla_kv_norm, mla_w_ukv, ffn_w1, ffn_w2):
    (cos_swa, sin_swa), (cos_mla, sin_mla) = _rope_tables()
    tabs = (cos_swa, sin_swa, cos_mla, sin_mla)

    cvec = jnp.concatenate([c, c_ctx[None, :], jnp.zeros((3, D_MODEL), F32)], axis=0)
    mods = _ada(cvec, ada_w, ada_b)

    tm_p = 512
    tm_f = 1024

    def lat_row(tm):
        return lambda i: i // (SEQ // tm)

    ctx_row = lambda i: 4

    xl = x.reshape(BATCH * SEQ, D_MODEL)
    xc = ctx.reshape(BATCH * CTX_LEN, D_MODEL)

    w_in_pad = jnp.pad(w_in, ((0, 0), (0, 0), (0, IN_COLS - IN_WIDTH))).astype(BF16)
    w_out_b = w_out.astype(BF16)
    glu_w_b = s5_glu_w.astype(BF16)
    w_uq_pad = jnp.pad(mla_w_uq.reshape(DEPTH, MLA_Q_RANK, MLA_HEADS, MLA_NOPE + MLA_ROPE),
                       ((0, 0), (0, 0), (0, 0), (0, MLA_QK_PAD - MLA_NOPE - MLA_ROPE))
                       ).reshape(DEPTH, MLA_Q_RANK, MLA_HEADS * MLA_QK_PAD).astype(BF16)
    w_ukv4 = mla_w_ukv.reshape(DEPTH, MLA_KV_RANK, MLA_HEADS, MLA_NOPE + MLA_V)
    w_ukv_perm = jnp.concatenate([w_ukv4[..., :MLA_NOPE].reshape(DEPTH, MLA_KV_RANK, -1),
                                  w_ukv4[..., MLA_NOPE:].reshape(DEPTH, MLA_KV_RANK, -1)], axis=2).astype(BF16)
    s5_w = _s5_disc(s5_a_re, s5_a_im, s5_log_dt, s5_b_re, s5_b_im, s5_c_re, s5_c_im)

    for i in range(DEPTH):
        need_ctx = i < DEPTH - 1
        mod = [mods[i, :, k * D_MODEL:(k + 1) * D_MODEL].reshape(8, 1, D_MODEL) for k in range(6)]
        g_pre = norm_mix_pre[i].reshape(1, -1)

        u_l, qk_l, vs_l, q_l, k_l, v_l = _mix_in(
            xl, mod[0], mod[1], g_pre, w_in_pad, w_uq_pad, w_ukv_perm, mla_q_norm[i], mla_kv_norm[i],
            tabs, i, tm_p, lat_row(tm_p), True, SEQ)
        u_c, qk_c, vs_c, q_c, k_c, v_c = _mix_in(
            xc, mod[0], mod[1], g_pre, w_in_pad, w_uq_pad, w_ukv_perm, mla_q_norm[i], mla_kv_norm[i],
            tabs, i, tm_p, ctx_row, False, tm_p)

        yf, yb = _s5_scan(u_c, u_l, *s5_w, i, 0 if need_ctx else S5_CTX_CHUNKS)
        u3_l = u_l.reshape(BATCH, SEQ, S5_WIDTH)
        u3_c = u_c.reshape(BATCH, CTX_LEN, S5_WIDTH)

        swa_l = _swa_latent(qk_l, vs_l, qk_c, vs_c, swa_sink[i])

        mla_l = _mla_attn(q_l, [k_c, k_l], [v_c, v_l], SEQ, 512)

        xl = _out_proj(u3_l, yf, yb, 0, s5_d[i], glu_w_b, s5_glu_b[i], swa_l, mla_l, w_out_b, xl,
                       norm_mix_post[i], mod[2], i, tm_p, lat_row(tm_p))
        xl = _ffn(xl, mod[3], mod[4], norm_ffn_pre[i], ffn_w1, ffn_w2, norm_ffn_post[i], mod[5],
                  i, tm_f, lat_row(tm_f))

        if need_ctx:
            swa_c = _swa_context(qk_c, vs_c, swa_sink[i])
            mla_c = _mla_attn(q_c, [k_c], [v_c], CTX_LEN, 256)
            xc = _out_proj(u3_c, yf, yb, SEQ, s5_d[i], glu_w_b, s5_glu_b[i], swa_c, mla_c, w_out_b, xc,
                           norm_mix_post[i], mod[2], i, CTX_LEN, ctx_row)
            xc = _ffn(xc, mod[3], mod[4], norm_ffn_pre[i], ffn_w1, ffn_w2, norm_ffn_post[i], mod[5],
                      i, tm_f, ctx_row)

    return xl.reshape(BATCH, SEQ, D_MODEL)
```

```python
import functools
import math

import jax
import jax.numpy as jnp
from jax import lax
from jax.experimental import pallas as pl
from jax.experimental.pallas import tpu as pltpu

F32 = jnp.float32
BF16 = jnp.bfloat16

D_MODEL = 2048
BATCH = 4
SEQ = 2048
DEPTH = 2
GRID_W = 64
CTX_LEN = 256
EPS = 1e-6
ROPE_BASE = 10000.0
NEG_INF = -1e30
BLOCK = 128
HEAD_DIM = 128
S5_WIDTH = 512
S5_GROUP = 16
S5_GROUPS = 32
S5_STATE = 64
S5_LANES = S5_GROUPS * S5_STATE
SWA_HEADS = 6
SWA_KV_HEADS = 2
SWA_REP = SWA_HEADS // SWA_KV_HEADS
MLA_HEADS = 6
MLA_Q_RANK = 768
MLA_KV_RANK = 512
MLA_NOPE = 128
MLA_ROPE = 64
MLA_V = 128
MLA_QK_PAD = 256
MLA_V_PAD = 256
D_FF = 4 * D_MODEL
IN_WIDTHS = (512, 768, 256, 256, 768, 512, 64)
IN_WIDTH = sum(IN_WIDTHS)
OFF_U, OFF_QS, OFF_KS, OFF_VS, OFF_CQ, OFF_CKV, OFF_KR = 0, 512, 1280, 1536, 1792, 2560, 3072

VMEM_LIMIT = 56 * 1024 * 1024


def _cparams(*sem):
    return pltpu.CompilerParams(dimension_semantics=sem, vmem_limit_bytes=VMEM_LIMIT)


def _dot(a, b):
    return jnp.dot(a, b, preferred_element_type=F32)


def _dot_nt(a, b):
    return lax.dot_general(a, b, (((1,), (1,)), ((), ())), preferred_element_type=F32)


def _rms(x, g):
    return x * lax.rsqrt(jnp.mean(x * x, axis=-1, keepdims=True) + EPS) * g


def _rope(x, cos, sin, half):
    lane = lax.broadcasted_iota(jnp.int32, x.shape, 1)
    fwd = pltpu.roll(x, 128 - half, 1)
    bwd = pltpu.roll(x, half, 1)
    sw = jnp.where((lane % (2 * half)) < half, fwd, bwd)
    return x * cos + sw * sin


def _ada_kernel(c_ref, w_ref, b_ref, o_ref):
    c = c_ref[...]
    s = c * jax.nn.sigmoid(c)
    o_ref[0] = _dot(s.astype(BF16), w_ref[0].astype(BF16)) + b_ref[0]


def _ada(cvec, ada_w, ada_b):
    tn = 1024
    n = 6 * D_MODEL
    return pl.pallas_call(
        _ada_kernel,
        grid=(DEPTH, n // tn),
        in_specs=[pl.BlockSpec((8, D_MODEL), lambda l, j: (0, 0)),
                  pl.BlockSpec((1, D_MODEL, tn), lambda l, j: (l, 0, j)),
                  pl.BlockSpec((1, 1, tn), lambda l, j: (l, 0, j))],
        out_specs=pl.BlockSpec((1, 8, tn), lambda l, j: (l, 0, j)),
        out_shape=jax.ShapeDtypeStruct((DEPTH, 8, n), F32),
        compiler_params=_cparams("arbitrary", "arbitrary"),
        name="ada",
    )(cvec, ada_w, ada_b.reshape(DEPTH, 1, n))


IN_COLS = 3200
SWA_QK_W = (SWA_HEADS + SWA_KV_HEADS) * HEAD_DIM


def _mix_in_kernel(x_ref, shift_ref, scale_ref, g_ref, w_ref, wq_ref, wkv_ref, qn_ref, kvn_ref,
                   cs_ref, ss_ref, cm_ref, sm_ref,
                   u_ref, qk_ref, vs_ref, q_ref, k_ref, v_ref, h_ref, *, rope):
    h = _rms(x_ref[...], g_ref[...]) * (1.0 + scale_ref[0]) + shift_ref[0]
    h_ref[...] = h.astype(BF16)

    def proj(lo, width):
        return _dot(h_ref[...], w_ref[:, lo:lo + width])

    u_ref[...] = proj(OFF_U, S5_WIDTH)

    swa_scale = HEAD_DIM ** -0.5
    per = 4
    for c in range(SWA_QK_W // (per * HEAD_DIM)):
        zc = proj(OFF_QS + c * per * HEAD_DIM, per * HEAD_DIM)
        for hh in range(per):
            head = c * per + hh
            xh = zc[:, hh * HEAD_DIM:(hh + 1) * HEAD_DIM]
            if rope:
                xh = _rope(xh, cs_ref[...], ss_ref[...], HEAD_DIM // 4)
            if head < SWA_HEADS:
                xh = xh * swa_scale
            qk_ref[:, head * HEAD_DIM:(head + 1) * HEAD_DIM] = xh.astype(BF16)

    vs_ref[...] = proj(OFF_VS, SWA_KV_HEADS * HEAD_DIM).astype(BF16)

    mla_scale = (MLA_NOPE + MLA_ROPE) ** -0.5
    cq = proj(OFF_CQ, MLA_Q_RANK)
    q = _dot(_rms(cq, qn_ref[...]).astype(BF16), wq_ref[...])
    for hd in range(MLA_HEADS):
        lo = hd * MLA_QK_PAD
        q_ref[:, lo:lo + MLA_NOPE] = (q[:, lo:lo + MLA_NOPE] * mla_scale).astype(BF16)
        r = q[:, lo + MLA_NOPE:lo + MLA_QK_PAD]
        if rope:
            r = _rope(r, cm_ref[...], sm_ref[...], MLA_ROPE // 4)
        q_ref[:, lo + MLA_NOPE:lo + MLA_QK_PAD] = (r * mla_scale).astype(BF16)

    ckv = proj(OFF_CKV, MLA_KV_RANK)
    kv = _dot(_rms(ckv, kvn_ref[...]).astype(BF16), wkv_ref[...])
    kr = proj(OFF_KR, 128)
    if rope:
        kr = _rope(kr, cm_ref[...], sm_ref[...], MLA_ROPE // 4)
    kr = kr.astype(BF16)
    for hd in range(MLA_HEADS):
        lo = hd * MLA_QK_PAD
        k_ref[:, lo:lo + MLA_NOPE] = kv[:, hd * MLA_NOPE:(hd + 1) * MLA_NOPE].astype(BF16)
        k_ref[:, lo + MLA_NOPE:lo + MLA_QK_PAD] = kr
        vlo = hd * MLA_V_PAD
        vh = kv[:, MLA_HEADS * MLA_NOPE + hd * MLA_V:MLA_HEADS * MLA_NOPE + (hd + 1) * MLA_V].astype(BF16)
        v_ref[:, vlo:vlo + MLA_V] = vh
        v_ref[:, vlo + MLA_V:vlo + MLA_V_PAD] = jnp.ones_like(vh)


def _mix_in(x2d, shift, scale, g, w_in, w_uq, w_ukv, q_norm, kv_norm, tabs, layer, tm, mod_row, rope, seq):
    rows = x2d.shape[0]
    once = pl.Buffered(1)
    tab = pl.BlockSpec((tm, 128), lambda i: (i % (seq // tm), 0))

    def res(shape):
        return pl.BlockSpec((None,) + shape, lambda i: (layer, 0, 0), pipeline_mode=once)

    def out(width):
        return pl.BlockSpec((tm, width), lambda i: (i, 0))

    widths = (S5_WIDTH, SWA_QK_W, SWA_KV_HEADS * HEAD_DIM,
              MLA_HEADS * MLA_QK_PAD, MLA_HEADS * MLA_QK_PAD, MLA_HEADS * MLA_V_PAD)
    dtypes = (F32, BF16, BF16, BF16, BF16, BF16)
    return pl.pallas_call(
        functools.partial(_mix_in_kernel, rope=rope),
        grid=(rows // tm,),
        in_specs=[pl.BlockSpec((tm, D_MODEL), lambda i: (i, 0)),
                  pl.BlockSpec((1, 1, D_MODEL), lambda i: (mod_row(i), 0, 0)),
                  pl.BlockSpec((1, 1, D_MODEL), lambda i: (mod_row(i), 0, 0)),
                  pl.BlockSpec((1, D_MODEL), lambda i: (0, 0)),
                  res((D_MODEL, IN_COLS)),
                  res((MLA_Q_RANK, MLA_HEADS * MLA_QK_PAD)),
                  res((MLA_KV_RANK, MLA_HEADS * (MLA_NOPE + MLA_V))),
                  pl.BlockSpec((1, MLA_Q_RANK), lambda i: (0, 0)),
                  pl.BlockSpec((1, MLA_KV_RANK), lambda i: (0, 0)),
                  tab, tab, tab, tab],
        out_specs=[out(w) for w in widths],
        out_shape=[jax.ShapeDtypeStruct((rows, w), dt) for w, dt in zip(widths, dtypes)],
        scratch_shapes=[pltpu.VMEM((tm, D_MODEL), BF16)],
        compiler_params=_cparams("arbitrary"),
        name="mix_in",
    )(x2d, shift, scale, g, w_in, w_uq, w_ukv, q_norm.reshape(1, -1), kv_norm.reshape(1, -1), *tabs)


S5_GH = S5_GROUPS // 2


def _zoh(are, aim, ldt):
    dt = jnp.exp(ldt)
    mag = jnp.exp(are * dt)
    lr = mag * jnp.cos(aim * dt)
    li = mag * jnp.sin(aim * dt)
    den = are * are + aim * aim
    nr = lr - 1.0
    return lr, li, (nr * are + li * aim) / den, (li * are - nr * aim) / den


def _s5_disc_kernel(are_ref, aim_ref, ldt_ref, arec_ref, aimc_ref, ldtc_ref, br_ref, bi_ref, cr_ref, ci_ref,
                    lam_ref, wbr_ref, wbi_ref, wcr_ref, wci_ref):
    lr, li, _, _ = _zoh(are_ref[0], aim_ref[0], ldt_ref[0])
    lam_ref[0, 0:1, :] = lr
    lam_ref[0, 1:2, :] = li
    gh, n, p = S5_GH, S5_STATE, S5_GROUP
    kin, kst = gh * p, gh * n

    def iota(shape, axis):
        return lax.broadcasted_iota(jnp.int32, shape, axis)

    tile_n = jnp.where(iota((n, kst), 0) == (iota((n, kst), 1) & (n - 1)), 1.0, 0.0).astype(BF16)
    tile_p = jnp.where(iota((p, kin), 0) == (iota((p, kin), 1) & (p - 1)), 1.0, 0.0).astype(BF16)
    lp, ln = p.bit_length() - 1, n.bit_length() - 1
    diag_b = (iota((kin, kst), 0) >> lp) == (iota((kin, kst), 1) >> ln)
    diag_c = (iota((kst, kin), 0) >> ln) == (iota((kst, kin), 1) >> lp)
    for h in range(2):
        _, _, fr, fi = _zoh(arec_ref[0, h], aimc_ref[0, h], ldtc_ref[0, h])
        fr = jnp.broadcast_to(fr[:, None, :], (gh, p, n)).reshape(kin, n)
        fi = jnp.broadcast_to(fi[:, None, :], (gh, p, n)).reshape(kin, n)
        bbr = (fr * br_ref[0, h] - fi * bi_ref[0, h]).astype(BF16)
        bbi = (fr * bi_ref[0, h] + fi * br_ref[0, h]).astype(BF16)
        wbr_ref[0, h] = jnp.where(diag_b, _dot(bbr, tile_n), 0.0).astype(BF16)
        wbi_ref[0, h] = jnp.where(diag_b, _dot(bbi, tile_n), 0.0).astype(BF16)
        wcr_ref[0, h] = jnp.where(diag_c, _dot(cr_ref[0, h].astype(BF16), tile_p), 0.0).astype(BF16)
        wci_ref[0, h] = jnp.where(diag_c, _dot(ci_ref[0, h].astype(BF16), tile_p), 0.0).astype(BF16)


def _s5_disc(a_re, a_im, log_dt, b_re, b_im, c_re, c_im):
    nd = DEPTH * 2
    gh, n, p = S5_GH, S5_STATE, S5_GROUP
    kin, kst = gh * p, gh * n
    are = a_re.reshape(nd, 1, S5_LANES)
    aim = a_im.reshape(nd, 1, S5_LANES)
    ldt_gn = jnp.broadcast_to(log_dt[..., None], a_re.shape)
    ldt = ldt_gn.reshape(nd, 1, S5_LANES)
    compact = lambda a: a.reshape(nd, 2, gh, n)
    bt = lambda b: b.reshape(nd, 2, gh, n, p).transpose(0, 1, 2, 4, 3).reshape(nd, 2, kin, n)
    ct = lambda c: c.reshape(nd, 2, gh, p, n).transpose(0, 1, 2, 4, 3).reshape(nd, 2, kst, p)
    vec = pl.BlockSpec((1, 1, S5_LANES), lambda d: (d, 0, 0))
    cvec = pl.BlockSpec((1, 2, gh, n), lambda d: (d, 0, 0, 0))
    bin_spec = pl.BlockSpec((1, 2, kin, n), lambda d: (d, 0, 0, 0))
    cin_spec = pl.BlockSpec((1, 2, kst, p), lambda d: (d, 0, 0, 0))
    bspec = pl.BlockSpec((1, 2, kin, kst), lambda d: (d, 0, 0, 0))
    cspec = pl.BlockSpec((1, 2, kst, kin), lambda d: (d, 0, 0, 0))
    return pl.pallas_call(
        _s5_disc_kernel,
        grid=(nd,),
        in_specs=[vec, vec, vec, cvec, cvec, cvec, bin_spec, bin_spec, cin_spec, cin_spec],
        out_specs=[pl.BlockSpec((1, 2, S5_LANES), lambda d: (d, 0, 0)), bspec, bspec, cspec, cspec],
        out_shape=[jax.ShapeDtypeStruct((nd, 2, S5_LANES), F32),
                   jax.ShapeDtypeStruct((nd, 2, kin, kst), BF16),
                   jax.ShapeDtypeStruct((nd, 2, kin, kst), BF16),
                   jax.ShapeDtypeStruct((nd, 2, kst, kin), BF16),
                   jax.ShapeDtypeStruct((nd, 2, kst, kin), BF16)],
        compiler_params=_cparams("arbitrary"),
        name="s5_disc",
    )(are, aim, ldt, compact(a_re), compact(a_im), compact(ldt_gn),
      bt(b_re), bt(b_im), ct(c_re), ct(c_im))


S5_TC = 128
S5_CTX_CHUNKS = CTX_LEN // S5_TC
S5_LAT_CHUNKS = SEQ // S5_TC


def _s5_scan_kernel(ucf_ref, ulf_ref, ucb_ref, ulb_ref, lam_ref, wbr_ref, wbi_ref, wcr_ref, wci_ref,
                    yf_ref, yb_ref, fr_ref, fi_ref, gr_ref, gi_ref, sr_ref, si_ref):
    c = pl.program_id(0)
    tc = S5_TC
    kin, kst = S5_WIDTH // 2, S5_LANES // 2
    nlb = kst // 128

    @pl.when(c == 0)
    def _():
        sr_ref[...] = jnp.zeros_like(sr_ref)
        si_ref[...] = jnp.zeros_like(si_ref)

    def project(u, d, dst_r, dst_i):
        u = u.reshape(BATCH * tc, S5_WIDTH).astype(BF16)
        for h in range(2):
            uh = u[:, h * kin:(h + 1) * kin]
            for w_ref, dst in ((wbr_ref, dst_r), (wbi_ref, dst_i)):
                p = _dot(uh, w_ref[d, h])
                for j in range(nlb):
                    for b in range(BATCH):
                        dst[h * nlb + j, pl.ds(b, tc, stride=BATCH), :] = (
                            p[b * tc:(b + 1) * tc, j * 128:(j + 1) * 128])

    in_ctx = c < S5_CTX_CHUNKS
    project(jnp.where(in_ctx, ucf_ref[...], ulf_ref[...]), 0, fr_ref, fi_ref)
    project(jnp.where(in_ctx, ucb_ref[...], ulb_ref[...]), 1, gr_ref, gi_ref)

    nb = 8
    for lb in range(2 * nlb // nb):
        ls = slice(lb * nb, (lb + 1) * nb)
        fwd_rows = lax.broadcasted_iota(jnp.int32, (nb, 8, 128), 1) < BATCH
        la_r = jnp.where(fwd_rows, lam_ref[0, 0, ls], lam_ref[1, 0, ls])
        la_i = jnp.where(fwd_rows, lam_ref[0, 1, ls], lam_ref[1, 1, ls])
        lb_r = jnp.where(fwd_rows, lam_ref[1, 0, ls], lam_ref[0, 0, ls])
        lb_i = jnp.where(fwd_rows, lam_ref[1, 1, ls], lam_ref[0, 1, ls])

        def body(k, carry, ls=ls, fwd_rows=fwd_rows, la_r=la_r, la_i=la_i, lb_r=lb_r, lb_i=lb_i):
            s_r, s_i = carry
            rf = pl.ds(pl.multiple_of(k * 8, 8), 8)
            rb = pl.ds(pl.multiple_of((tc // 2 - 1 - k) * 8, 8), 8)
            f_r, f_i = fr_ref[ls, rf, :], fi_ref[ls, rf, :]
            g_r, g_i = gr_ref[ls, rb, :], gi_ref[ls, rb, :]
            a_r = la_r * s_r - la_i * s_i + jnp.where(fwd_rows, f_r, g_r)
            a_i = la_r * s_i + la_i * s_r + jnp.where(fwd_rows, f_i, g_i)
            t_r = pltpu.roll(a_r, BATCH, 1)
            t_i = pltpu.roll(a_i, BATCH, 1)
            b_r = lb_r * t_r - lb_i * t_i + jnp.where(fwd_rows, g_r, f_r)
            b_i = lb_r * t_i + lb_i * t_r + jnp.where(fwd_rows, g_i, f_i)
            fr_ref[ls, rf, :] = jnp.where(fwd_rows, a_r, b_r)
            fi_ref[ls, rf, :] = jnp.where(fwd_rows, a_i, b_i)
            gr_ref[ls, rb, :] = jnp.where(fwd_rows, b_r, a_r)
            gi_ref[ls, rb, :] = jnp.where(fwd_rows, b_i, a_i)
            return pltpu.roll(b_r, BATCH, 1), pltpu.roll(b_i, BATCH, 1)

        s_r, s_i = lax.fori_loop(0, tc // 2, body, (sr_ref[ls], si_ref[ls]), unroll=True)
        sr_ref[ls] = s_r
        si_ref[ls] = s_i

    def gather_rows(src, h):
        return jnp.concatenate(
            [jnp.concatenate([src[h * nlb + j, pl.ds(b, tc, stride=BATCH), :] for j in range(nlb)], axis=1)
             for b in range(BATCH)], axis=0)

    def readout(d, src_r, src_i, y_ref):
        for h in range(2):
            hr = gather_rows(src_r, h).astype(BF16)
            hi = gather_rows(src_i, h).astype(BF16)
            y = _dot(hr, wcr_ref[d, h]) - _dot(hi, wci_ref[d, h])
            y_ref[:, :, h * kin:(h + 1) * kin] = y.reshape(BATCH, tc, kin)

    readout(0, fr_ref, fi_ref, yf_ref)
    readout(1, gr_ref, gi_ref, yb_ref)


def _s5_scan(u_c, u_l, lam, wbr, wbi, wcr, wci, layer, skip_chunks):
    tc = S5_TC
    nchunk = S5_CTX_CHUNKS + S5_LAT_CHUNKS
    kin, kst = S5_WIDTH // 2, S5_LANES // 2
    zc3 = u_c.reshape(BATCH, CTX_LEN, S5_WIDTH)
    zl3 = u_l.reshape(BATCH, SEQ, S5_WIDTH)

    def bwd_chunk(c):
        return jnp.where(c < S5_CTX_CHUNKS, S5_CTX_CHUNKS - 1 - c, nchunk + S5_CTX_CHUNKS - 1 - c)

    def ctx_spec(chunk_of):
        return pl.BlockSpec((BATCH, tc, S5_WIDTH),
                            lambda c: (0, jnp.clip(chunk_of(c), 0, S5_CTX_CHUNKS - 1), 0))

    def lat_spec(chunk_of):
        return pl.BlockSpec((BATCH, tc, S5_WIDTH),
                            lambda c: (0, jnp.clip(chunk_of(c) - S5_CTX_CHUNKS, 0, S5_LAT_CHUNKS - 1), 0))

    def out_block(chunk):
        return jnp.where(chunk >= S5_CTX_CHUNKS, chunk - S5_CTX_CHUNKS, S5_LAT_CHUNKS + chunk)

    def out_spec(chunk_of):
        return pl.BlockSpec((BATCH, tc, S5_WIDTH),
                            lambda c: (0, out_block(chunk_of(jnp.maximum(c, skip_chunks))), 0))

    once = pl.Buffered(1)
    bspec = pl.BlockSpec((2, 2, kin, kst), lambda c: (layer, 0, 0, 0), pipeline_mode=once)
    cspec = pl.BlockSpec((2, 2, kst, kin), lambda c: (layer, 0, 0, 0), pipeline_mode=once)
    t_out = (nchunk - skip_chunks) * tc
    nblk = S5_LANES // 128
    big = pltpu.VMEM((nblk, tc * BATCH, 128), F32)
    state = pltpu.VMEM((nblk, 8, 128), F32)
    lam = lam.reshape(DEPTH * 2, 2, nblk, 1, 128)
    fwd_chunk = lambda c: c
    return pl.pallas_call(
        _s5_scan_kernel,
        grid=(nchunk,),
        in_specs=[ctx_spec(fwd_chunk), lat_spec(fwd_chunk), ctx_spec(bwd_chunk), lat_spec(bwd_chunk),
                  pl.BlockSpec((2, 2, nblk, 1, 128), lambda c: (layer, 0, 0, 0, 0)),
                  bspec, bspec, cspec, cspec],
        out_specs=[out_spec(fwd_chunk), out_spec(bwd_chunk)],
        out_shape=[jax.ShapeDtypeStruct((BATCH, t_out, S5_WIDTH), F32)] * 2,
        scratch_shapes=[big, big, big, big, state, state],
        compiler_params=_cparams("arbitrary"),
        name="s5_scan",
    )(zc3, zl3, zc3, zl3, lam, wbr, wbi, wcr, wci)


def _sink_rows(sink_ref, g, rows):
    r = lax.broadcasted_iota(jnp.int32, (rows, 1), 0) // (rows // SWA_REP)
    s0, s1, s2 = sink_ref[SWA_REP * g], sink_ref[SWA_REP * g + 1], sink_ref[SWA_REP * g + 2]
    return jnp.where(r == 0, s0, jnp.where(r == 1, s1, s2))


def _stack_heads(x, n):
    return jnp.concatenate([x[:, h * HEAD_DIM:(h + 1) * HEAD_DIM] for h in range(n)], axis=0)


def _unstack_heads(x, n):
    rows = x.shape[0] // n
    return jnp.concatenate([x[h * rows:(h + 1) * rows] for h in range(n)], axis=1)


SWA_QB = 4


def _with_ones(v):
    return jnp.concatenate([v, jnp.ones_like(v)], axis=1)


def _swa_latent_kernel(sink_ref, q_ref, kp_ref, km_ref, kn_ref, vp_ref, vm_ref, vn_ref,
                       kx_ref, vx_ref, o_ref):
    n = pl.program_id(1)
    rows = SWA_REP * BLOCK
    qi = lax.broadcasted_iota(jnp.int32, (rows, 3 * BLOCK), 0) % BLOCK
    kj = lax.broadcasted_iota(jnp.int32, (rows, 3 * BLOCK), 1)
    in_window = (kj >= qi) & (kj <= qi + 2 * BLOCK)
    for g in range(SWA_KV_HEADS):
        gs = slice(g * HEAD_DIM, (g + 1) * HEAD_DIM)
        kspan = jnp.concatenate([kp_ref[:, gs], km_ref[:, gs], kn_ref[:, gs]], axis=0)
        vspan = _with_ones(jnp.concatenate([vp_ref[:, gs], vm_ref[:, gs], vn_ref[:, gs]], axis=0))
        kx = kx_ref[:, gs]
        vx = _with_ones(vx_ref[:, gs])
        sk = _sink_rows(sink_ref, g, rows)
        for j in range(SWA_QB):
            blk = n * SWA_QB + j
            qs = slice(g * SWA_REP * HEAD_DIM, (g + 1) * SWA_REP * HEAD_DIM)
            q = _stack_heads(q_ref[j * BLOCK:(j + 1) * BLOCK, qs], SWA_REP)
            kb = kspan[j * BLOCK:(j + 3) * BLOCK]
            vb = vspan[j * BLOCK:(j + 3) * BLOCK]
            s_ctx = _dot_nt(q, kx)
            s_band = _dot_nt(q, kb)
            kpos = (blk - 1) * BLOCK + kj
            valid = in_window & (kpos >= 0) & (kpos < SEQ)
            s_band = jnp.where(valid, s_band, NEG_INF)
            m = jnp.maximum(jnp.maximum(jnp.max(s_ctx, axis=-1, keepdims=True),
                                        jnp.max(s_band, axis=-1, keepdims=True)), sk)
            p_ctx = jnp.exp(s_ctx - m).astype(BF16)
            p_band = jnp.exp(s_band - m).astype(BF16)
            o = _dot(p_ctx, vx) + _dot(p_band, vb)
            den = o[:, HEAD_DIM:] + jnp.exp(sk - m)
            o_ref[j * BLOCK:(j + 1) * BLOCK, qs] = _unstack_heads(
                o[:, :HEAD_DIM] / den, SWA_REP).astype(o_ref.dtype)


def _swa_latent(qk, vs, qk_c, vs_c, sink):
    nb = SEQ // BLOCK
    ng = nb // SWA_QB
    span = SWA_QB * BLOCK
    kvw = SWA_KV_HEADS * HEAD_DIM
    kcol = SWA_HEADS * HEAD_DIM // kvw
    vcol = 0
    kxcol = kcol

    def edge(col, blk_of):
        return pl.BlockSpec((BLOCK, kvw), lambda b, n, s: (b * nb + jnp.clip(blk_of(n), 0, nb - 1), col))

    def main(col):
        return pl.BlockSpec((span, kvw), lambda b, n, s: (b * ng + n, col))

    prev_blk = lambda n: n * SWA_QB - 1
    next_blk = lambda n: (n + 1) * SWA_QB
    grid_spec = pltpu.PrefetchScalarGridSpec(
        num_scalar_prefetch=1,
        grid=(BATCH, ng),
        in_specs=[pl.BlockSpec((span, SWA_HEADS * HEAD_DIM), lambda b, n, s: (b * ng + n, 0)),
                  edge(kcol, prev_blk), main(kcol), edge(kcol, next_blk),
                  edge(vcol, prev_blk), main(vcol), edge(vcol, next_blk),
                  pl.BlockSpec((CTX_LEN, kvw), lambda b, n, s: (b, kxcol)),
                  pl.BlockSpec((CTX_LEN, kvw), lambda b, n, s: (b, vcol))],
        out_specs=pl.BlockSpec((span, SWA_HEADS * HEAD_DIM), lambda b, n, s: (b * ng + n, 0)),
    )
    return pl.pallas_call(
        _swa_latent_kernel,
        grid_spec=grid_spec,
        out_shape=jax.ShapeDtypeStruct((BATCH * SEQ, SWA_HEADS * HEAD_DIM), BF16),
        compiler_params=_cparams("arbitrary", "arbitrary"),
        name="swa_latent",
    )(sink, qk, qk, qk, qk, vs, vs, vs, qk_c, vs_c)


def _swa_context_kernel(sink_ref, q_ref, k_ref, v_ref, o_ref):
    g = pl.program_id(1)
    q = _stack_heads(q_ref[...], SWA_REP)
    s = _dot_nt(q, k_ref[...])
    sk = _sink_rows(sink_ref, g, SWA_REP * CTX_LEN)
    m = jnp.maximum(jnp.max(s, axis=-1, keepdims=True), sk)
    p = jnp.exp(s - m)
    den = jnp.sum(p, axis=-1, keepdims=True) + jnp.exp(sk - m)
    o = _dot(p.astype(BF16), v_ref[...]) / den
    o_ref[...] = _unstack_heads(o, SWA_REP).astype(o_ref.dtype)


def _swa_context(qk_c, vs_c, sink):
    qw = SWA_REP * HEAD_DIM
    grid_spec = pltpu.PrefetchScalarGridSpec(
        num_scalar_prefetch=1,
        grid=(BATCH, SWA_KV_HEADS),
        in_specs=[pl.BlockSpec((CTX_LEN, qw), lambda b, g, s: (b, g)),
                  pl.BlockSpec((CTX_LEN, HEAD_DIM), lambda b, g, s: (b, SWA_HEADS + g)),
                  pl.BlockSpec((CTX_LEN, HEAD_DIM), lambda b, g, s: (b, g))],
        out_specs=pl.BlockSpec((CTX_LEN, qw), lambda b, g, s: (b, g)),
    )
    return pl.pallas_call(
        _swa_context_kernel,
        grid_spec=grid_spec,
        out_shape=jax.ShapeDtypeStruct((BATCH * CTX_LEN, SWA_HEADS * HEAD_DIM), BF16),
        compiler_params=_cparams("arbitrary", "arbitrary"),
        name="swa_context",
    )(sink, qk_c, qk_c, vs_c)


def _mla_attn_kernel(*refs, nseg):
    q_ref = refs[0]
    k_refs = refs[1:1 + nseg]
    v_refs = refs[1 + nseg:1 + 2 * nseg]
    o_ref = refs[1 + 2 * nseg]
    for h in range(MLA_HEADS):
        q = q_ref[:, h * MLA_QK_PAD:(h + 1) * MLA_QK_PAD]
        s = [_dot_nt(q, k[:, h * MLA_QK_PAD:(h + 1) * MLA_QK_PAD]) for k in k_refs]
        m = functools.reduce(jnp.maximum, [jnp.max(x, axis=-1, keepdims=True) for x in s])
        p = [jnp.exp(x - m).astype(BF16) for x in s]
        o = sum(_dot(x, v[:, h * MLA_V_PAD:(h + 1) * MLA_V_PAD]) for x, v in zip(p, v_refs))
        o_ref[:, h * MLA_V:(h + 1) * MLA_V] = (o[:, :MLA_V] / o[:, MLA_V:]).astype(o_ref.dtype)


def _mla_attn(q, ks, vs, n_q, tq):
    nseg = len(ks)
    nq = n_q // tq
    lens = [k.shape[0] // BATCH for k in ks]
    kw, vw = MLA_HEADS * MLA_QK_PAD, MLA_HEADS * MLA_V
    in_specs = [pl.BlockSpec((tq, kw), lambda b, i: (b * nq + i, 0))]
    in_specs += [pl.BlockSpec((n, kw), lambda b, i: (b, 0)) for n in lens]
    in_specs += [pl.BlockSpec((n, MLA_HEADS * MLA_V_PAD), lambda b, i: (b, 0)) for n in lens]
    return pl.pallas_call(
        functools.partial(_mla_attn_kernel, nseg=nseg),
        grid=(BATCH, nq),
        in_specs=in_specs,
        out_specs=pl.BlockSpec((tq, vw), lambda b, i: (b * nq + i, 0)),
        out_shape=jax.ShapeDtypeStruct((BATCH * n_q, vw), BF16),
        compiler_params=_cparams("arbitrary", "arbitrary"),
        name="mla_attn",
    )(q, *ks, *vs)


def _out_proj_kernel(u_ref, yf_ref, yb_ref, d_ref, gw_ref, gb_ref, a2_ref, a3_ref, w_ref, x_ref, g_ref, gate_ref,
                     fg_ref, fshift_ref, fscale_ref, o_ref, f_ref):
    y = d_ref[...] * u_ref[0] + yf_ref[0] + yb_ref[0]
    gl = jax.nn.gelu(y)
    a1 = (gl * jax.nn.sigmoid(_dot(gl.astype(BF16), gw_ref[...]) + gb_ref[...])).astype(BF16)
    nc = 512
    k1 = a1.shape[1]
    k2 = k1 + a2_ref.shape[1]
    for n in range(D_MODEL // nc):
        cs = slice(n * nc, (n + 1) * nc)
        o_ref[:, cs] = (_dot(a1, w_ref[0:k1, cs]) + _dot(a2_ref[...], w_ref[k1:k2, cs])
                        + _dot(a3_ref[...], w_ref[k2:D_MODEL, cs]))
    xn = x_ref[...] + gate_ref[0] * _rms(o_ref[...], g_ref[...])
    o_ref[...] = xn
    f_ref[...] = (_rms(xn, fg_ref[...]) * (1.0 + fscale_ref[0]) + fshift_ref[0]).astype(BF16)


def _out_proj(u3, yf, yb, y_off, d, glu_w, glu_b, a2, a3, w, x2d, g, gate, fg, fshift, fscale, layer, tm, mod_row):
    rows = a2.shape[0]
    per_b = u3.shape[1] // tm
    once = pl.Buffered(1)
    uspec = pl.BlockSpec((1, tm, S5_WIDTH), lambda i: (i // per_b, i % per_b, 0))
    yspec = pl.BlockSpec((1, tm, S5_WIDTH), lambda i: (i // per_b, i % per_b + y_off // tm, 0))
    vec5 = pl.BlockSpec((1, S5_WIDTH), lambda i: (0, 0))
    vec = pl.BlockSpec((1, D_MODEL), lambda i: (0, 0))
    mod = pl.BlockSpec((1, 1, D_MODEL), lambda i: (mod_row(i), 0, 0))
    row = pl.BlockSpec((tm, D_MODEL), lambda i: (i, 0))
    return pl.pallas_call(
        _out_proj_kernel,
        grid=(rows // tm,),
        in_specs=[uspec, yspec, yspec, vec5,
                  pl.BlockSpec((None, S5_WIDTH, S5_WIDTH), lambda i: (layer, 0, 0), pipeline_mode=once), vec5,
                  pl.BlockSpec((tm, a2.shape[1]), lambda i: (i, 0)),
                  pl.BlockSpec((tm, a3.shape[1]), lambda i: (i, 0)),
                  pl.BlockSpec((None, D_MODEL, D_MODEL), lambda i: (layer, 0, 0), pipeline_mode=once),
                  row, vec, mod, vec, mod, mod],
        out_specs=[row, row],
        out_shape=[jax.ShapeDtypeStruct((rows, D_MODEL), F32), jax.ShapeDtypeStruct((rows, D_MODEL), BF16)],
        compiler_params=_cparams("arbitrary"),
        name="out_proj",
    )(u3, yf, yb, d.reshape(1, -1), glu_w, glu_b.reshape(1, -1), a2, a3, w, x2d, g.reshape(1, -1), gate,
      fg.reshape(1, -1), fshift, fscale)


def _ffn_kernel(x_ref, f_ref, w1_ref, w2_ref, gpost_ref, gate_ref, o_ref, h_ref):
    j = pl.program_id(1)

    @pl.when(j == 0)
    def _():
        o_ref[...] = jnp.zeros_like(o_ref)

    h = _dot(f_ref[...], w1_ref[...].astype(BF16))
    h_ref[...] = jnp.square(jnp.maximum(h, 0.0)).astype(BF16)
    nc = 512
    for n in range(D_MODEL // nc):
        cs = slice(n * nc, (n + 1) * nc)
        o_ref[:, cs] += _dot(h_ref[...], w2_ref[:, cs].astype(BF16))

    @pl.when(j == pl.num_programs(1) - 1)
    def _():
        o_ref[...] = x_ref[...] + gate_ref[0] * _rms(o_ref[...], gpost_ref[...])


def _ffn(x2d, f2d, w1, w2, gpost, gate, layer, tm, mod_row):
    rows = x2d.shape[0]
    tf = 512
    mod = pl.BlockSpec((1, 1, D_MODEL), lambda i, j: (mod_row(i), 0, 0))
    vec = pl.BlockSpec((1, D_MODEL), lambda i, j: (0, 0))
    once = pl.Buffered(1)
    return pl.pallas_call(
        _ffn_kernel,
        grid=(rows // tm, D_FF // tf),
        in_specs=[pl.BlockSpec((tm, D_MODEL), lambda i, j: (i, 0)),
                  pl.BlockSpec((tm, D_MODEL), lambda i, j: (i, 0), pipeline_mode=once),
                  pl.BlockSpec((None, D_MODEL, tf), lambda i, j: (layer, 0, j)),
                  pl.BlockSpec((None, tf, D_MODEL), lambda i, j: (layer, j, 0)),
                  vec, mod],
        out_specs=pl.BlockSpec((tm, D_MODEL), lambda i, j: (i, 0), pipeline_mode=once),
        out_shape=jax.ShapeDtypeStruct((rows, D_MODEL), F32),
        scratch_shapes=[pltpu.VMEM((tm, tf), BF16)],
        compiler_params=_cparams("arbitrary", "arbitrary"),
        name="ffn",
    )(x2d, f2d, w1, w2, gpost.reshape(1, -1), gate)


def _rope_tables():
    t = jnp.arange(SEQ)
    row = (t // GRID_W).astype(F32)[:, None]
    col = (t % GRID_W).astype(F32)[:, None]

    def tables(rot_dim):
        quarter = rot_dim // 4
        inv_freq = ROPE_BASE ** (-jnp.arange(quarter, dtype=F32) / quarter)
        ar, ac = row * inv_freq, col * inv_freq
        cos = jnp.concatenate([jnp.cos(ar), jnp.cos(ar), jnp.cos(ac), jnp.cos(ac)], axis=1)
        sin = jnp.concatenate([-jnp.sin(ar), jnp.sin(ar), -jnp.sin(ac), jnp.sin(ac)], axis=1)
        pad = 128 - rot_dim
        if pad:
            cos = jnp.concatenate([cos, jnp.ones((SEQ, pad), F32)], axis=1)
            sin = jnp.concatenate([sin, jnp.zeros((SEQ, pad), F32)], axis=1)
        return cos, sin

    return tables(HEAD_DIM), tables(MLA_ROPE)


def kernel(x, c, ctx, c_ctx, ada_w, ada_b, norm_mix_pre, norm_mix_post, norm_ffn_pre, norm_ffn_post, w_in, w_out, s5_a_re, s5_a_im, s5_log_dt, s5_b_re, s5_b_im, s5_c_re, s5_c_im, s5_d, s5_glu_w, s5_glu_b, swa_sink, mla_q_norm, mla_w_uq, mla_kv_norm, mla_w_ukv, ffn_w1, ffn_w2):
    (cos_swa, sin_swa), (cos_mla, sin_mla) = _rope_tables()
    tabs = (cos_swa, sin_swa, cos_mla, sin_mla)

    cvec = jnp.concatenate([c, c_ctx[None, :], jnp.zeros((3, D_MODEL), F32)], axis=0)
    mods = _ada(cvec, ada_w, ada_b)

    tm_p = 512
    tm_f = 1024

    def lat_row(tm):
        return lambda i: i // (SEQ // tm)

    ctx_row = lambda i: 4

    xl = x.reshape(BATCH * SEQ, D_MODEL)
    xc = ctx.reshape(BATCH * CTX_LEN, D_MODEL)

    w_in_pad = jnp.pad(w_in, ((0, 0), (0, 0), (0, IN_COLS - IN_WIDTH))).astype(BF16)
    w_out_b = w_out.astype(BF16)
    glu_w_b = s5_glu_w.astype(BF16)
    w_uq_pad = jnp.pad(mla_w_uq.reshape(DEPTH, MLA_Q_RANK, MLA_HEADS, MLA_NOPE + MLA_ROPE),
                       ((0, 0), (0, 0), (0, 0), (0, MLA_QK_PAD - MLA_NOPE - MLA_ROPE))
                       ).reshape(DEPTH, MLA_Q_RANK, MLA_HEADS * MLA_QK_PAD).astype(BF16)
    w_ukv4 = mla_w_ukv.reshape(DEPTH, MLA_KV_RANK, MLA_HEADS, MLA_NOPE + MLA_V)
    w_ukv_perm = jnp.concatenate([w_ukv4[..., :MLA_NOPE].reshape(DEPTH, MLA_KV_RANK, -1),
                                  w_ukv4[..., MLA_NOPE:].reshape(DEPTH, MLA_KV_RANK, -1)], axis=2).astype(BF16)
    s5_w = _s5_disc(s5_a_re, s5_a_im, s5_log_dt, s5_b_re, s5_b_im, s5_c_re, s5_c_im)

    for i in range(DEPTH):
        need_ctx = i < DEPTH - 1
        mod = [mods[i, :, k * D_MODEL:(k + 1) * D_MODEL].reshape(8, 1, D_MODEL) for k in range(6)]
        g_pre = norm_mix_pre[i].reshape(1, -1)

        u_l, qk_l, vs_l, q_l, k_l, v_l = _mix_in(
            xl, mod[0], mod[1], g_pre, w_in_pad, w_uq_pad, w_ukv_perm, mla_q_norm[i], mla_kv_norm[i],
            tabs, i, tm_p, lat_row(tm_p), True, SEQ)
        u_c, qk_c, vs_c, q_c, k_c, v_c = _mix_in(
            xc, mod[0], mod[1], g_pre, w_in_pad, w_uq_pad, w_ukv_perm, mla_q_norm[i], mla_kv_norm[i],
            tabs, i, tm_p, ctx_row, False, tm_p)

        yf, yb = _s5_scan(u_c, u_l, *s5_w, i, 0 if need_ctx else S5_CTX_CHUNKS)
        u3_l = u_l.reshape(BATCH, SEQ, S5_WIDTH)
        u3_c = u_c.reshape(BATCH, CTX_LEN, S5_WIDTH)

        swa_l = _swa_latent(qk_l, vs_l, qk_c, vs_c, swa_sink[i])

        mla_l = _mla_attn(q_l, [k_c, k_l], [v_c, v_l], SEQ, 512)

        xl, fl = _out_proj(u3_l, yf, yb, 0, s5_d[i], glu_w_b, s5_glu_b[i], swa_l, mla_l, w_out_b, xl,
                           norm_mix_post[i], mod[2], norm_ffn_pre[i], mod[3], mod[4], i, tm_p, lat_row(tm_p))
        xl = _ffn(xl, fl, ffn_w1, ffn_w2, norm_ffn_post[i], mod[5], i, tm_f, lat_row(tm_f))

        if need_ctx:
            swa_c = _swa_context(qk_c, vs_c, swa_sink[i])
            mla_c = _mla_attn(q_c, [k_c], [v_c], CTX_LEN, 256)
            xc, fc = _out_proj(u3_c, yf, yb, SEQ, s5_d[i], glu_w_b, s5_glu_b[i], swa_c, mla_c, w_out_b, xc,
                               norm_mix_post[i], mod[2], norm_ffn_pre[i], mod[3], mod[4], i, CTX_LEN, ctx_row)
            xc = _ffn(xc, fc, ffn_w1, ffn_w2, norm_ffn_post[i], mod[5], i, tm_f, ctx_row)

    return xl.reshape(BATCH, SEQ, D_MODEL)
```

```python
import functools
import math

import jax
import jax.numpy as jnp
from jax import lax
from jax.experimental import pallas as pl
from jax.experimental.pallas import tpu as pltpu

F32 = jnp.float32
BF16 = jnp.bfloat16

D_MODEL = 2048
BATCH = 4
SEQ = 2048
DEPTH = 2
GRID_W = 64
CTX_LEN = 256
EPS = 1e-6
ROPE_BASE = 10000.0
NEG_INF = -1e30
BLOCK = 128
HEAD_DIM = 128
S5_WIDTH = 512
S5_GROUP = 16
S5_GROUPS = 32
S5_STATE = 64
S5_LANES = S5_GROUPS * S5_STATE
SWA_HEADS = 6
SWA_KV_HEADS = 2
SWA_REP = SWA_HEADS // SWA_KV_HEADS
MLA_HEADS = 6
MLA_Q_RANK = 768
MLA_KV_RANK = 512
MLA_NOPE = 128
MLA_ROPE = 64
MLA_V = 128
MLA_QK_PAD = 256
MLA_V_PAD = 256
D_FF = 4 * D_MODEL
IN_WIDTHS = (512, 768, 256, 256, 768, 512, 64)
IN_WIDTH = sum(IN_WIDTHS)
OFF_U, OFF_QS, OFF_KS, OFF_VS, OFF_CQ, OFF_CKV, OFF_KR = 0, 512, 1280, 1536, 1792, 2560, 3072

VMEM_LIMIT = 56 * 1024 * 1024


def _cparams(*sem):
    return pltpu.CompilerParams(dimension_semantics=sem, vmem_limit_bytes=VMEM_LIMIT)


def _dot(a, b):
    return jnp.dot(a, b, preferred_element_type=F32)


def _dot_nt(a, b):
    return lax.dot_general(a, b, (((1,), (1,)), ((), ())), preferred_element_type=F32)


def _rms(x, g):
    return x * lax.rsqrt(jnp.mean(x * x, axis=-1, keepdims=True) + EPS) * g


def _rope(x, cos, sin, half):
    lane = lax.broadcasted_iota(jnp.int32, x.shape, 1)
    fwd = pltpu.roll(x, 128 - half, 1)
    bwd = pltpu.roll(x, half, 1)
    sw = jnp.where((lane % (2 * half)) < half, fwd, bwd)
    return x * cos + sw * sin


def _ada_kernel(c_ref, w_ref, b_ref, o_ref):
    c = c_ref[...]
    s = c * jax.nn.sigmoid(c)
    o_ref[0] = _dot(s.astype(BF16), w_ref[0].astype(BF16)) + b_ref[0]


def _ada(cvec, ada_w, ada_b):
    tn = 1024
    n = 6 * D_MODEL
    return pl.pallas_call(
        _ada_kernel,
        grid=(DEPTH, n // tn),
        in_specs=[pl.BlockSpec((8, D_MODEL), lambda l, j: (0, 0)),
                  pl.BlockSpec((1, D_MODEL, tn), lambda l, j: (l, 0, j)),
                  pl.BlockSpec((1, 1, tn), lambda l, j: (l, 0, j))],
        out_specs=pl.BlockSpec((1, 8, tn), lambda l, j: (l, 0, j)),
        out_shape=jax.ShapeDtypeStruct((DEPTH, 8, n), F32),
        compiler_params=_cparams("arbitrary", "arbitrary"),
        name="ada",
    )(cvec, ada_w, ada_b.reshape(DEPTH, 1, n))


IN_COLS = 3200
SWA_QK_W = (SWA_HEADS + SWA_KV_HEADS) * HEAD_DIM


def _mix_in_kernel(x_ref, shift_ref, scale_ref, g_ref, w_ref, wq_ref, wkv_ref, qn_ref, kvn_ref,
                   cs_ref, ss_ref, cm_ref, sm_ref,
                   u_ref, qk_ref, vs_ref, q_ref, k_ref, v_ref, h_ref, *, rope):
    h = _rms(x_ref[...], g_ref[...]) * (1.0 + scale_ref[0]) + shift_ref[0]
    h_ref[...] = h.astype(BF16)

    def proj(lo, width):
        return _dot(h_ref[...], w_ref[:, lo:lo + width])

    u_ref[...] = proj(OFF_U, S5_WIDTH)

    swa_scale = HEAD_DIM ** -0.5
    per = 4
    for c in range(SWA_QK_W // (per * HEAD_DIM)):
        zc = proj(OFF_QS + c * per * HEAD_DIM, per * HEAD_DIM)
        for hh in range(per):
            head = c * per + hh
            xh = zc[:, hh * HEAD_DIM:(hh + 1) * HEAD_DIM]
            if rope:
                xh = _rope(xh, cs_ref[...], ss_ref[...], HEAD_DIM // 4)
            if head < SWA_HEADS:
                xh = xh * swa_scale
            qk_ref[:, head * HEAD_DIM:(head + 1) * HEAD_DIM] = xh.astype(BF16)

    vs_ref[...] = proj(OFF_VS, SWA_KV_HEADS * HEAD_DIM).astype(BF16)

    mla_scale = (MLA_NOPE + MLA_ROPE) ** -0.5
    cq = proj(OFF_CQ, MLA_Q_RANK)
    q = _dot(_rms(cq, qn_ref[...]).astype(BF16), wq_ref[...])
    for hd in range(MLA_HEADS):
        lo = hd * MLA_QK_PAD
        q_ref[:, lo:lo + MLA_NOPE] = (q[:, lo:lo + MLA_NOPE] * mla_scale).astype(BF16)
        r = q[:, lo + MLA_NOPE:lo + MLA_QK_PAD]
        if rope:
            r = _rope(r, cm_ref[...], sm_ref[...], MLA_ROPE // 4)
        q_ref[:, lo + MLA_NOPE:lo + MLA_QK_PAD] = (r * mla_scale).astype(BF16)

    ckv = proj(OFF_CKV, MLA_KV_RANK)
    kv = _dot(_rms(ckv, kvn_ref[...]).astype(BF16), wkv_ref[...])
    kr = proj(OFF_KR, 128)
    if rope:
        kr = _rope(kr, cm_ref[...], sm_ref[...], MLA_ROPE // 4)
    kr = kr.astype(BF16)
    for hd in range(MLA_HEADS):
        lo = hd * MLA_QK_PAD
        k_ref[:, lo:lo + MLA_NOPE] = kv[:, hd * MLA_NOPE:(hd + 1) * MLA_NOPE].astype(BF16)
        k_ref[:, lo + MLA_NOPE:lo + MLA_QK_PAD] = kr
        vlo = hd * MLA_V_PAD
        vh = kv[:, MLA_HEADS * MLA_NOPE + hd * MLA_V:MLA_HEADS * MLA_NOPE + (hd + 1) * MLA_V].astype(BF16)
        v_ref[:, vlo:vlo + MLA_V] = vh
        v_ref[:, vlo + MLA_V:vlo + MLA_V_PAD] = jnp.ones_like(vh)


def _mix_in(x2d, shift, scale, g, w_in, w_uq, w_ukv, q_norm, kv_norm, tabs, layer, tm, mod_row, rope, seq):
    rows = x2d.shape[0]
    once = pl.Buffered(1)
    tab = pl.BlockSpec((tm, 128), lambda i: (i % (seq // tm), 0))

    def res(shape):
        return pl.BlockSpec((None,) + shape, lambda i: (layer, 0, 0), pipeline_mode=once)

    def out(width):
        return pl.BlockSpec((tm, width), lambda i: (i, 0))

    widths = (S5_WIDTH, SWA_QK_W, SWA_KV_HEADS * HEAD_DIM,
              MLA_HEADS * MLA_QK_PAD, MLA_HEADS * MLA_QK_PAD, MLA_HEADS * MLA_V_PAD)
    dtypes = (F32, BF16, BF16, BF16, BF16, BF16)
    return pl.pallas_call(
        functools.partial(_mix_in_kernel, rope=rope),
        grid=(rows // tm,),
        in_specs=[pl.BlockSpec((tm, D_MODEL), lambda i: (i, 0)),
                  pl.BlockSpec((1, 1, D_MODEL), lambda i: (mod_row(i), 0, 0)),
                  pl.BlockSpec((1, 1, D_MODEL), lambda i: (mod_row(i), 0, 0)),
                  pl.BlockSpec((1, D_MODEL), lambda i: (0, 0)),
                  res((D_MODEL, IN_COLS)),
                  res((MLA_Q_RANK, MLA_HEADS * MLA_QK_PAD)),
                  res((MLA_KV_RANK, MLA_HEADS * (MLA_NOPE + MLA_V))),
                  pl.BlockSpec((1, MLA_Q_RANK), lambda i: (0, 0)),
                  pl.BlockSpec((1, MLA_KV_RANK), lambda i: (0, 0)),
                  tab, tab, tab, tab],
        out_specs=[out(w) for w in widths],
        out_shape=[jax.ShapeDtypeStruct((rows, w), dt) for w, dt in zip(widths, dtypes)],
        scratch_shapes=[pltpu.VMEM((tm, D_MODEL), BF16)],
        compiler_params=_cparams("arbitrary"),
        name="mix_in",
    )(x2d, shift, scale, g, w_in, w_uq, w_ukv, q_norm.reshape(1, -1), kv_norm.reshape(1, -1), *tabs)


S5_GH = S5_GROUPS // 2


def _zoh(are, aim, ldt):
    dt = jnp.exp(ldt)
    mag = jnp.exp(are * dt)
    lr = mag * jnp.cos(aim * dt)
    li = mag * jnp.sin(aim * dt)
    den = are * are + aim * aim
    nr = lr - 1.0
    return lr, li, (nr * are + li * aim) / den, (li * are - nr * aim) / den


def _s5_disc_kernel(are_ref, aim_ref, ldt_ref, arec_ref, aimc_ref, ldtc_ref, br_ref, bi_ref, cr_ref, ci_ref,
                    lam_ref, wbr_ref, wbi_ref, wcr_ref, wci_ref):
    lr, li, _, _ = _zoh(are_ref[0], aim_ref[0], ldt_ref[0])
    lam_ref[0, 0:1, :] = lr
    lam_ref[0, 1:2, :] = li
    gh, n, p = S5_GH, S5_STATE, S5_GROUP
    kin, kst = gh * p, gh * n

    def iota(shape, axis):
        return lax.broadcasted_iota(jnp.int32, shape, axis)

    tile_n = jnp.where(iota((n, kst), 0) == (iota((n, kst), 1) & (n - 1)), 1.0, 0.0).astype(BF16)
    tile_p = jnp.where(iota((p, kin), 0) == (iota((p, kin), 1) & (p - 1)), 1.0, 0.0).astype(BF16)
    lp, ln = p.bit_length() - 1, n.bit_length() - 1
    diag_b = (iota((kin, kst), 0) >> lp) == (iota((kin, kst), 1) >> ln)
    diag_c = (iota((kst, kin), 0) >> ln) == (iota((kst, kin), 1) >> lp)
    for h in range(2):
        _, _, fr, fi = _zoh(arec_ref[0, h], aimc_ref[0, h], ldtc_ref[0, h])
        fr = jnp.broadcast_to(fr[:, None, :], (gh, p, n)).reshape(kin, n)
        fi = jnp.broadcast_to(fi[:, None, :], (gh, p, n)).reshape(kin, n)
        bbr = (fr * br_ref[0, h] - fi * bi_ref[0, h]).astype(BF16)
        bbi = (fr * bi_ref[0, h] + fi * br_ref[0, h]).astype(BF16)
        wbr_ref[0, h] = jnp.where(diag_b, _dot(bbr, tile_n), 0.0).astype(BF16)
        wbi_ref[0, h] = jnp.where(diag_b, _dot(bbi, tile_n), 0.0).astype(BF16)
        wcr_ref[0, h] = jnp.where(diag_c, _dot(cr_ref[0, h].astype(BF16), tile_p), 0.0).astype(BF16)
        wci_ref[0, h] = jnp.where(diag_c, _dot(ci_ref[0, h].astype(BF16), tile_p), 0.0).astype(BF16)


def _s5_disc(a_re, a_im, log_dt, b_re, b_im, c_re, c_im):
    nd = DEPTH * 2
    gh, n, p = S5_GH, S5_STATE, S5_GROUP
    kin, kst = gh * p, gh * n
    are = a_re.reshape(nd, 1, S5_LANES)
    aim = a_im.reshape(nd, 1, S5_LANES)
    ldt_gn = jnp.broadcast_to(log_dt[..., None], a_re.shape)
    ldt = ldt_gn.reshape(nd, 1, S5_LANES)
    compact = lambda a: a.reshape(nd, 2, gh, n)
    bt = lambda b: b.reshape(nd, 2, gh, n, p).transpose(0, 1, 2, 4, 3).reshape(nd, 2, kin, n)
    ct = lambda c: c.reshape(nd, 2, gh, p, n).transpose(0, 1, 2, 4, 3).reshape(nd, 2, kst, p)
    vec = pl.BlockSpec((1, 1, S5_LANES), lambda d: (d, 0, 0))
    cvec = pl.BlockSpec((1, 2, gh, n), lambda d: (d, 0, 0, 0))
    bin_spec = pl.BlockSpec((1, 2, kin, n), lambda d: (d, 0, 0, 0))
    cin_spec = pl.BlockSpec((1, 2, kst, p), lambda d: (d, 0, 0, 0))
    bspec = pl.BlockSpec((1, 2, kin, kst), lambda d: (d, 0, 0, 0))
    cspec = pl.BlockSpec((1, 2, kst, kin), lambda d: (d, 0, 0, 0))
    return pl.pallas_call(
        _s5_disc_kernel,
        grid=(nd,),
        in_specs=[vec, vec, vec, cvec, cvec, cvec, bin_spec, bin_spec, cin_spec, cin_spec],
        out_specs=[pl.BlockSpec((1, 2, S5_LANES), lambda d: (d, 0, 0)), bspec, bspec, cspec, cspec],
        out_shape=[jax.ShapeDtypeStruct((nd, 2, S5_LANES), F32),
                   jax.ShapeDtypeStruct((nd, 2, kin, kst), BF16),
                   jax.ShapeDtypeStruct((nd, 2, kin, kst), BF16),
                   jax.ShapeDtypeStruct((nd, 2, kst, kin), BF16),
                   jax.ShapeDtypeStruct((nd, 2, kst, kin), BF16)],
        compiler_params=_cparams("arbitrary"),
        name="s5_disc",
    )(are, aim, ldt, compact(a_re), compact(a_im), compact(ldt_gn),
      bt(b_re), bt(b_im), ct(c_re), ct(c_im))


S5_TC = 128
S5_CTX_CHUNKS = CTX_LEN // S5_TC
S5_LAT_CHUNKS = SEQ // S5_TC


def _s5_scan_kernel(ucf_ref, ulf_ref, ucb_ref, ulb_ref, lam_ref, wbr_ref, wbi_ref, wcr_ref, wci_ref,
                    yf_ref, yb_ref, fr_ref, fi_ref, gr_ref, gi_ref, sr_ref, si_ref, ut_ref, yt_ref):
    c = pl.program_id(0)
    tc = S5_TC
    kin, kst = S5_WIDTH // 2, S5_LANES // 2
    nlb = kst // 128

    @pl.when(c == 0)
    def _():
        sr_ref[...] = jnp.zeros_like(sr_ref)
        si_ref[...] = jnp.zeros_like(si_ref)

    in_ctx = c < S5_CTX_CHUNKS
    state = [(fr_ref, fi_ref), (gr_ref, gi_ref)]
    nub = S5_WIDTH // 128

    u_dir = []
    for d, (uc_ref, ul_ref) in enumerate(((ucf_ref, ulf_ref), (ucb_ref, ulb_ref))):
        u = jnp.where(in_ctx, uc_ref[...], ul_ref[...])
        for j in range(nub):
            for b in range(BATCH):
                ut_ref[d, j, pl.ds(b, tc, stride=BATCH), :] = u[b, :, j * 128:(j + 1) * 128]
        u_dir.append(jnp.concatenate([ut_ref[d, j] for j in range(nub)], axis=1).astype(BF16))

    def project(d, h, part):
        w_ref = (wbr_ref, wbi_ref)[part]
        dst = state[d][part]
        p = _dot(u_dir[d][:, h * kin:(h + 1) * kin], w_ref[d, h])
        for j in range(nlb):
            dst[h * nlb + j] = p[:, j * 128:(j + 1) * 128]

    def scan_steps(h, k0, k1, carry):
        ls = slice(h * nlb, (h + 1) * nlb)
        fwd_rows = lax.broadcasted_iota(jnp.int32, (nlb, 8, 128), 1) < BATCH
        la_r = jnp.where(fwd_rows, lam_ref[0, 0, ls], lam_ref[1, 0, ls])
        la_i = jnp.where(fwd_rows, lam_ref[0, 1, ls], lam_ref[1, 1, ls])
        lb_r = jnp.where(fwd_rows, lam_ref[1, 0, ls], lam_ref[0, 0, ls])
        lb_i = jnp.where(fwd_rows, lam_ref[1, 1, ls], lam_ref[0, 1, ls])
        s_r, s_i = carry
        for k in range(k0, k1):
            rf = slice(k * 8, (k + 1) * 8)
            kb = tc // 2 - 1 - k
            rb = slice(kb * 8, (kb + 1) * 8)
            f_r, f_i = fr_ref[ls, rf, :], fi_ref[ls, rf, :]
            g_r, g_i = gr_ref[ls, rb, :], gi_ref[ls, rb, :]
            a_r = la_r * s_r - la_i * s_i + jnp.where(fwd_rows, f_r, g_r)
            a_i = la_r * s_i + la_i * s_r + jnp.where(fwd_rows, f_i, g_i)
            t_r = pltpu.roll(a_r, BATCH, 1)
            t_i = pltpu.roll(a_i, BATCH, 1)
            b_r = lb_r * t_r - lb_i * t_i + jnp.where(fwd_rows, g_r, f_r)
            b_i = lb_r * t_i + lb_i * t_r + jnp.where(fwd_rows, g_i, f_i)
            fr_ref[ls, rf, :] = jnp.where(fwd_rows, a_r, b_r)
            fi_ref[ls, rf, :] = jnp.where(fwd_rows, a_i, b_i)
            gr_ref[ls, rb, :] = jnp.where(fwd_rows, b_r, a_r)
            gi_ref[ls, rb, :] = jnp.where(fwd_rows, b_i, a_i)
            s_r, s_i = pltpu.roll(b_r, BATCH, 1), pltpu.roll(b_i, BATCH, 1)
        return s_r, s_i

    def half_rows(src, h):
        return jnp.concatenate([src[h * nlb + j] for j in range(nlb)], axis=1).astype(BF16)

    def readout(d, h):
        y = _dot(half_rows(state[d][0], h), wcr_ref[d, h]) - _dot(half_rows(state[d][1], h), wci_ref[d, h])
        for j in range(kin // 128):
            yt_ref[d, h * (kin // 128) + j] = y[:, j * 128:(j + 1) * 128]

    def emit(d):
        y_ref = (yf_ref, yb_ref)[d]
        for b in range(BATCH):
            y_ref[b] = jnp.concatenate([yt_ref[d, j, pl.ds(b, tc, stride=BATCH), :] for j in range(nub)], axis=1)

    quarters = 4
    per = tc // 2 // quarters
    pieces = [(d, part) for d in range(2) for part in range(2)]
    for d, part in pieces:
        project(d, 0, part)
    carry = (sr_ref[0:nlb], si_ref[0:nlb])
    for q, (d, part) in enumerate(pieces):
        carry = scan_steps(0, q * per, (q + 1) * per, carry)
        project(d, 1, part)
    sr_ref[0:nlb], si_ref[0:nlb] = carry
    carry = (sr_ref[nlb:2 * nlb], si_ref[nlb:2 * nlb])
    for q in range(quarters):
        carry = scan_steps(1, q * per, (q + 1) * per, carry)
        if q % 2 == 1:
            readout(q // 2, 0)
    sr_ref[nlb:2 * nlb], si_ref[nlb:2 * nlb] = carry
    for d in range(2):
        readout(d, 1)
        emit(d)


def _s5_scan(u_c, u_l, lam, wbr, wbi, wcr, wci, layer, skip_chunks):
    tc = S5_TC
    nchunk = S5_CTX_CHUNKS + S5_LAT_CHUNKS
    kin, kst = S5_WIDTH // 2, S5_LANES // 2
    zc3 = u_c.reshape(BATCH, CTX_LEN, S5_WIDTH)
    zl3 = u_l.reshape(BATCH, SEQ, S5_WIDTH)

    def bwd_chunk(c):
        return jnp.where(c < S5_CTX_CHUNKS, S5_CTX_CHUNKS - 1 - c, nchunk + S5_CTX_CHUNKS - 1 - c)

    def ctx_spec(chunk_of):
        return pl.BlockSpec((BATCH, tc, S5_WIDTH),
                            lambda c: (0, jnp.clip(chunk_of(c), 0, S5_CTX_CHUNKS - 1), 0))

    def lat_spec(chunk_of):
        return pl.BlockSpec((BATCH, tc, S5_WIDTH),
                            lambda c: (0, jnp.clip(chunk_of(c) - S5_CTX_CHUNKS, 0, S5_LAT_CHUNKS - 1), 0))

    def out_block(chunk):
        return jnp.where(chunk >= S5_CTX_CHUNKS, chunk - S5_CTX_CHUNKS, S5_LAT_CHUNKS + chunk)

    def out_spec(chunk_of):
        return pl.BlockSpec((BATCH, tc, S5_WIDTH),
                            lambda c: (0, out_block(chunk_of(jnp.maximum(c, skip_chunks))), 0))

    once = pl.Buffered(1)
    bspec = pl.BlockSpec((2, 2, kin, kst), lambda c: (layer, 0, 0, 0), pipeline_mode=once)
    cspec = pl.BlockSpec((2, 2, kst, kin), lambda c: (layer, 0, 0, 0), pipeline_mode=once)
    t_out = (nchunk - skip_chunks) * tc
    nblk = S5_LANES // 128
    big = pltpu.VMEM((nblk, tc * BATCH, 128), F32)
    state = pltpu.VMEM((nblk, 8, 128), F32)
    narrow = pltpu.VMEM((2, S5_WIDTH // 128, tc * BATCH, 128), F32)
    lam = lam.reshape(DEPTH * 2, 2, nblk, 1, 128)
    fwd_chunk = lambda c: c
    return pl.pallas_call(
        _s5_scan_kernel,
        grid=(nchunk,),
        in_specs=[ctx_spec(fwd_chunk), lat_spec(fwd_chunk), ctx_spec(bwd_chunk), lat_spec(bwd_chunk),
                  pl.BlockSpec((2, 2, nblk, 1, 128), lambda c: (layer, 0, 0, 0, 0)),
                  bspec, bspec, cspec, cspec],
        out_specs=[out_spec(fwd_chunk), out_spec(bwd_chunk)],
        out_shape=[jax.ShapeDtypeStruct((BATCH, t_out, S5_WIDTH), F32)] * 2,
        scratch_shapes=[big, big, big, big, state, state, narrow, narrow],
        compiler_params=_cparams("arbitrary"),
        name="s5_scan",
    )(zc3, zl3, zc3, zl3, lam, wbr, wbi, wcr, wci)


def _sink_rows(sink_ref, g, rows):
    r = lax.broadcasted_iota(jnp.int32, (rows, 1), 0) // (rows // SWA_REP)
    s0, s1, s2 = sink_ref[SWA_REP * g], sink_ref[SWA_REP * g + 1], sink_ref[SWA_REP * g + 2]
    return jnp.where(r == 0, s0, jnp.where(r == 1, s1, s2))


def _stack_heads(x, n):
    return jnp.concatenate([x[:, h * HEAD_DIM:(h + 1) * HEAD_DIM] for h in range(n)], axis=0)


def _unstack_heads(x, n):
    rows = x.shape[0] // n
    return jnp.concatenate([x[h * rows:(h + 1) * rows] for h in range(n)], axis=1)


SWA_QB = 4


def _with_ones(v):
    return jnp.concatenate([v, jnp.ones_like(v)], axis=1)


def _swa_latent_kernel(sink_ref, q_ref, kp_ref, km_ref, kn_ref, vp_ref, vm_ref, vn_ref,
                       kx_ref, vx_ref, o_ref):
    n = pl.program_id(1)
    rows = SWA_REP * BLOCK
    qi = lax.broadcasted_iota(jnp.int32, (rows, 3 * BLOCK), 0) % BLOCK
    kj = lax.broadcasted_iota(jnp.int32, (rows, 3 * BLOCK), 1)
    in_window = (kj >= qi) & (kj <= qi + 2 * BLOCK)
    for g in range(SWA_KV_HEADS):
        gs = slice(g * HEAD_DIM, (g + 1) * HEAD_DIM)
        kspan = jnp.concatenate([kp_ref[:, gs], km_ref[:, gs], kn_ref[:, gs]], axis=0)
        vspan = _with_ones(jnp.concatenate([vp_ref[:, gs], vm_ref[:, gs], vn_ref[:, gs]], axis=0))
        kx = kx_ref[:, gs]
        vx = _with_ones(vx_ref[:, gs])
        sk = _sink_rows(sink_ref, g, rows)
        for j in range(SWA_QB):
            blk = n * SWA_QB + j
            qs = slice(g * SWA_REP * HEAD_DIM, (g + 1) * SWA_REP * HEAD_DIM)
            q = _stack_heads(q_ref[j * BLOCK:(j + 1) * BLOCK, qs], SWA_REP)
            kb = kspan[j * BLOCK:(j + 3) * BLOCK]
            vb = vspan[j * BLOCK:(j + 3) * BLOCK]
            s_ctx = _dot_nt(q, kx)
            s_band = _dot_nt(q, kb)
            kpos = (blk - 1) * BLOCK + kj
            valid = in_window & (kpos >= 0) & (kpos < SEQ)
            s_band = jnp.where(valid, s_band, NEG_INF)
            m = jnp.maximum(jnp.maximum(jnp.max(s_ctx, axis=-1, keepdims=True),
                                        jnp.max(s_band, axis=-1, keepdims=True)), sk)
            p_ctx = jnp.exp(s_ctx - m).astype(BF16)
            p_band = jnp.exp(s_band - m).astype(BF16)
            o = _dot(p_ctx, vx) + _dot(p_band, vb)
            den = o[:, HEAD_DIM:] + jnp.exp(sk - m)
            o_ref[j * BLOCK:(j + 1) * BLOCK, qs] = _unstack_heads(
                o[:, :HEAD_DIM] / den, SWA_REP).astype(o_ref.dtype)


def _swa_latent(qk, vs, qk_c, vs_c, sink):
    nb = SEQ // BLOCK
    ng = nb // SWA_QB
    span = SWA_QB * BLOCK
    kvw = SWA_KV_HEADS * HEAD_DIM
    kcol = SWA_HEADS * HEAD_DIM // kvw
    vcol = 0
    kxcol = kcol

    def edge(col, blk_of):
        return pl.BlockSpec((BLOCK, kvw), lambda b, n, s: (b * nb + jnp.clip(blk_of(n), 0, nb - 1), col))

    def main(col):
        return pl.BlockSpec((span, kvw), lambda b, n, s: (b * ng + n, col))

    prev_blk = lambda n: n * SWA_QB - 1
    next_blk = lambda n: (n + 1) * SWA_QB
    grid_spec = pltpu.PrefetchScalarGridSpec(
        num_scalar_prefetch=1,
        grid=(BATCH, ng),
        in_specs=[pl.BlockSpec((span, SWA_HEADS * HEAD_DIM), lambda b, n, s: (b * ng + n, 0)),
                  edge(kcol, prev_blk), main(kcol), edge(kcol, next_blk),
                  edge(vcol, prev_blk), main(vcol), edge(vcol, next_blk),
                  pl.BlockSpec((CTX_LEN, kvw), lambda b, n, s: (b, kxcol)),
                  pl.BlockSpec((CTX_LEN, kvw), lambda b, n, s: (b, vcol))],
        out_specs=pl.BlockSpec((span, SWA_HEADS * HEAD_DIM), lambda b, n, s: (b * ng + n, 0)),
    )
    return pl.pallas_call(
        _swa_latent_kernel,
        grid_spec=grid_spec,
        out_shape=jax.ShapeDtypeStruct((BATCH * SEQ, SWA_HEADS * HEAD_DIM), BF16),
        compiler_params=_cparams("arbitrary", "arbitrary"),
        name="swa_latent",
    )(sink, qk, qk, qk, qk, vs, vs, vs, qk_c, vs_c)


def _swa_context_kernel(sink_ref, q_ref, k_ref, v_ref, o_ref):
    g = pl.program_id(1)
    q = _stack_heads(q_ref[...], SWA_REP)
    s = _dot_nt(q, k_ref[...])
    sk = _sink_rows(sink_ref, g, SWA_REP * CTX_LEN)
    m = jnp.maximum(jnp.max(s, axis=-1, keepdims=True), sk)
    p = jnp.exp(s - m)
    den = jnp.sum(p, axis=-1, keepdims=True) + jnp.exp(sk - m)
    o = _dot(p.astype(BF16), v_ref[...]) / den
    o_ref[...] = _unstack_heads(o, SWA_REP).astype(o_ref.dtype)


def _swa_context(qk_c, vs_c, sink):
    qw = SWA_REP * HEAD_DIM
    grid_spec = pltpu.PrefetchScalarGridSpec(
        num_scalar_prefetch=1,
        grid=(BATCH, SWA_KV_HEADS),
        in_specs=[pl.BlockSpec((CTX_LEN, qw), lambda b, g, s: (b, g)),
                  pl.BlockSpec((CTX_LEN, HEAD_DIM), lambda b, g, s: (b, SWA_HEADS + g)),
                  pl.BlockSpec((CTX_LEN, HEAD_DIM), lambda b, g, s: (b, g))],
        out_specs=pl.BlockSpec((CTX_LEN, qw), lambda b, g, s: (b, g)),
    )
    return pl.pallas_call(
        _swa_context_kernel,
        grid_spec=grid_spec,
        out_shape=jax.ShapeDtypeStruct((BATCH * CTX_LEN, SWA_HEADS * HEAD_DIM), BF16),
        compiler_params=_cparams("arbitrary", "arbitrary"),
        name="swa_context",
    )(sink, qk_c, qk_c, vs_c)


def _mla_attn_kernel(*refs, nseg):
    q_ref = refs[0]
    k_refs = refs[1:1 + nseg]
    v_refs = refs[1 + nseg:1 + 2 * nseg]
    o_ref = refs[1 + 2 * nseg]
    for h in range(MLA_HEADS):
        q = q_ref[:, h * MLA_QK_PAD:(h + 1) * MLA_QK_PAD]
        s = [_dot_nt(q, k[:, h * MLA_QK_PAD:(h + 1) * MLA_QK_PAD]) for k in k_refs]
        m = functools.reduce(jnp.maximum, [jnp.max(x, axis=-1, keepdims=True) for x in s])
        p = [jnp.exp(x - m).astype(BF16) for x in s]
        o = sum(_dot(x, v[:, h * MLA_V_PAD:(h + 1) * MLA_V_PAD]) for x, v in zip(p, v_refs))
        o_ref[:, h * MLA_V:(h + 1) * MLA_V] = (o[:, :MLA_V] / o[:, MLA_V:]).astype(o_ref.dtype)


def _mla_attn(q, ks, vs, n_q, tq):
    nseg = len(ks)
    nq = n_q // tq
    lens = [k.shape[0] // BATCH for k in ks]
    kw, vw = MLA_HEADS * MLA_QK_PAD, MLA_HEADS * MLA_V
    in_specs = [pl.BlockSpec((tq, kw), lambda b, i: (b * nq + i, 0))]
    once = pl.Buffered(1)
    in_specs += [pl.BlockSpec((n, kw), lambda b, i: (b, 0), pipeline_mode=once) for n in lens]
    in_specs += [pl.BlockSpec((n, MLA_HEADS * MLA_V_PAD), lambda b, i: (b, 0), pipeline_mode=once) for n in lens]
    return pl.pallas_call(
        functools.partial(_mla_attn_kernel, nseg=nseg),
        grid=(BATCH, nq),
        in_specs=in_specs,
        out_specs=pl.BlockSpec((tq, vw), lambda b, i: (b * nq + i, 0)),
        out_shape=jax.ShapeDtypeStruct((BATCH * n_q, vw), BF16),
        compiler_params=_cparams("arbitrary", "arbitrary"),
        name="mla_attn",
    )(q, *ks, *vs)


def _out_proj_kernel(u_ref, yf_ref, yb_ref, d_ref, gw_ref, gb_ref, a2_ref, a3_ref, w_ref, x_ref, g_ref, gate_ref,
                     fg_ref, fshift_ref, fscale_ref, o_ref, f_ref):
    y = d_ref[...] * u_ref[0] + yf_ref[0] + yb_ref[0]
    gl = jax.nn.gelu(y)
    a1 = (gl * jax.nn.sigmoid(_dot(gl.astype(BF16), gw_ref[...]) + gb_ref[...])).astype(BF16)
    nc = 512
    k1 = a1.shape[1]
    k2 = k1 + a2_ref.shape[1]
    for n in range(D_MODEL // nc):
        cs = slice(n * nc, (n + 1) * nc)
        o_ref[:, cs] = (_dot(a1, w_ref[0:k1, cs]) + _dot(a2_ref[...], w_ref[k1:k2, cs])
                        + _dot(a3_ref[...], w_ref[k2:D_MODEL, cs]))
    xn = x_ref[...] + gate_ref[0] * _rms(o_ref[...], g_ref[...])
    o_ref[...] = xn
    f_ref[...] = (_rms(xn, fg_ref[...]) * (1.0 + fscale_ref[0]) + fshift_ref[0]).astype(BF16)


def _out_proj(u3, yf, yb, y_off, d, glu_w, glu_b, a2, a3, w, x2d, g, gate, fg, fshift, fscale, layer, tm, mod_row):
    rows = a2.shape[0]
    per_b = u3.shape[1] // tm
    once = pl.Buffered(1)
    uspec = pl.BlockSpec((1, tm, S5_WIDTH), lambda i: (i // per_b, i % per_b, 0))
    yspec = pl.BlockSpec((1, tm, S5_WIDTH), lambda i: (i // per_b, i % per_b + y_off // tm, 0))
    vec5 = pl.BlockSpec((1, S5_WIDTH), lambda i: (0, 0))
    vec = pl.BlockSpec((1, D_MODEL), lambda i: (0, 0))
    mod = pl.BlockSpec((1, 1, D_MODEL), lambda i: (mod_row(i), 0, 0))
    row = pl.BlockSpec((tm, D_MODEL), lambda i: (i, 0))
    return pl.pallas_call(
        _out_proj_kernel,
        grid=(rows // tm,),
        in_specs=[uspec, yspec, yspec, vec5,
                  pl.BlockSpec((None, S5_WIDTH, S5_WIDTH), lambda i: (layer, 0, 0), pipeline_mode=once), vec5,
                  pl.BlockSpec((tm, a2.shape[1]), lambda i: (i, 0)),
                  pl.BlockSpec((tm, a3.shape[1]), lambda i: (i, 0)),
                  pl.BlockSpec((None, D_MODEL, D_MODEL), lambda i: (layer, 0, 0), pipeline_mode=once),
                  row, vec, mod, vec, mod, mod],
        out_specs=[row, row],
        out_shape=[jax.ShapeDtypeStruct((rows, D_MODEL), F32), jax.ShapeDtypeStruct((rows, D_MODEL), BF16)],
        compiler_params=_cparams("arbitrary"),
        name="out_proj",
    )(u3, yf, yb, d.reshape(1, -1), glu_w, glu_b.reshape(1, -1), a2, a3, w, x2d, g.reshape(1, -1), gate,
      fg.reshape(1, -1), fshift, fscale)


def _ffn_kernel(x_ref, f_ref, w1_ref, w2_ref, gpost_ref, gate_ref, o_ref, h_ref):
    j = pl.program_id(1)

    @pl.when(j == 0)
    def _():
        o_ref[...] = jnp.zeros_like(o_ref)

    h = _dot(f_ref[...], w1_ref[...].astype(BF16))
    h_ref[...] = jnp.square(jnp.maximum(h, 0.0)).astype(BF16)
    nc = 512
    for n in range(D_MODEL // nc):
        cs = slice(n * nc, (n + 1) * nc)
        o_ref[:, cs] += _dot(h_ref[...], w2_ref[:, cs].astype(BF16))

    @pl.when(j == pl.num_programs(1) - 1)
    def _():
        o_ref[...] = x_ref[...] + gate_ref[0] * _rms(o_ref[...], gpost_ref[...])


def _ffn(x2d, f2d, w1, w2, gpost, gate, layer, tm, mod_row):
    rows = x2d.shape[0]
    tf = 512
    mod = pl.BlockSpec((1, 1, D_MODEL), lambda i, j: (mod_row(i), 0, 0))
    vec = pl.BlockSpec((1, D_MODEL), lambda i, j: (0, 0))
    once = pl.Buffered(1)
    return pl.pallas_call(
        _ffn_kernel,
        grid=(rows // tm, D_FF // tf),
        in_specs=[pl.BlockSpec((tm, D_MODEL), lambda i, j: (i, 0)),
                  pl.BlockSpec((tm, D_MODEL), lambda i, j: (i, 0), pipeline_mode=once),
                  pl.BlockSpec((None, D_MODEL, tf), lambda i, j: (layer, 0, j)),
                  pl.BlockSpec((None, tf, D_MODEL), lambda i, j: (layer, j, 0)),
                  vec, mod],
        out_specs=pl.BlockSpec((tm, D_MODEL), lambda i, j: (i, 0), pipeline_mode=once),
        out_shape=jax.ShapeDtypeStruct((rows, D_MODEL), F32),
        scratch_shapes=[pltpu.VMEM((tm, tf), BF16)],
        compiler_params=_cparams("arbitrary", "arbitrary"),
        name="ffn",
    )(x2d, f2d, w1, w2, gpost.reshape(1, -1), gate)


def _rope_tables():
    t = jnp.arange(SEQ)
    row = (t // GRID_W).astype(F32)[:, None]
    col = (t % GRID_W).astype(F32)[:, None]

    def tables(rot_dim):
        quarter = rot_dim // 4
        inv_freq = ROPE_BASE ** (-jnp.arange(quarter, dtype=F32) / quarter)
        ar, ac = row * inv_freq, col * inv_freq
        cos = jnp.concatenate([jnp.cos(ar), jnp.cos(ar), jnp.cos(ac), jnp.cos(ac)], axis=1)
        sin = jnp.concatenate([-jnp.sin(ar), jnp.sin(ar), -jnp.sin(ac), jnp.sin(ac)], axis=1)
        pad = 128 - rot_dim
        if pad:
            cos = jnp.concatenate([cos, jnp.ones((SEQ, pad), F32)], axis=1)
            sin = jnp.concatenate([sin, jnp.zeros((SEQ, pad), F32)], axis=1)
        return cos, sin

    return tables(HEAD_DIM), tables(MLA_ROPE)


def kernel(x, c, ctx, c_ctx, ada_w, ada_b, norm_mix_pre, norm_mix_post, norm_ffn_pre, norm_ffn_post, w_in, w_out, s5_a_re, s5_a_im, s5_log_dt, s5_b_re, s5_b_im, s5_c_re, s5_c_im, s5_d, s5_glu_w, s5_glu_b, swa_sink, mla_q_norm, mla_w_uq, mla_kv_norm, mla_w_ukv, ffn_w1, ffn_w2):
    (cos_swa, sin_swa), (cos_mla, sin_mla) = _rope_tables()
    tabs = (cos_swa, sin_swa, cos_mla, sin_mla)

    cvec = jnp.concatenate([c, c_ctx[None, :], jnp.zeros((3, D_MODEL), F32)], axis=0)
    mods = _ada(cvec, ada_w, ada_b)

    tm_p = 512
    tm_f = 1024

    def lat_row(tm):
        return lambda i: i // (SEQ // tm)

    ctx_row = lambda i: 4

    xl = x.reshape(BATCH * SEQ, D_MODEL)
    xc = ctx.reshape(BATCH * CTX_LEN, D_MODEL)

    w_in_pad = jnp.pad(w_in, ((0, 0), (0, 0), (0, IN_COLS - IN_WIDTH))).astype(BF16)
    w_out_b = w_out.astype(BF16)
    glu_w_b = s5_glu_w.astype(BF16)
    w_uq_pad = jnp.pad(mla_w_uq.reshape(DEPTH, MLA_Q_RANK, MLA_HEADS, MLA_NOPE + MLA_ROPE),
                       ((0, 0), (0, 0), (0, 0), (0, MLA_QK_PAD - MLA_NOPE - MLA_ROPE))
                       ).reshape(DEPTH, MLA_Q_RANK, MLA_HEADS * MLA_QK_PAD).astype(BF16)
    w_ukv4 = mla_w_ukv.reshape(DEPTH, MLA_KV_RANK, MLA_HEADS, MLA_NOPE + MLA_V)
    w_ukv_perm = jnp.concatenate([w_ukv4[..., :MLA_NOPE].reshape(DEPTH, MLA_KV_RANK, -1),
                                  w_ukv4[..., MLA_NOPE:].reshape(DEPTH, MLA_KV_RANK, -1)], axis=2).astype(BF16)
    s5_w = _s5_disc(s5_a_re, s5_a_im, s5_log_dt, s5_b_re, s5_b_im, s5_c_re, s5_c_im)

    for i in range(DEPTH):
        need_ctx = i < DEPTH - 1
        mod = [mods[i, :, k * D_MODEL:(k + 1) * D_MODEL].reshape(8, 1, D_MODEL) for k in range(6)]
        g_pre = norm_mix_pre[i].reshape(1, -1)

        u_l, qk_l, vs_l, q_l, k_l, v_l = _mix_in(
            xl, mod[0], mod[1], g_pre, w_in_pad, w_uq_pad, w_ukv_perm, mla_q_norm[i], mla_kv_norm[i],
            tabs, i, tm_p, lat_row(tm_p), True, SEQ)
        u_c, qk_c, vs_c, q_c, k_c, v_c = _mix_in(
            xc, mod[0], mod[1], g_pre, w_in_pad, w_uq_pad, w_ukv_perm, mla_q_norm[i], mla_kv_norm[i],
            tabs, i, tm_p, ctx_row, False, tm_p)

        yf, yb = _s5_scan(u_c, u_l, *s5_w, i, 0 if need_ctx else S5_CTX_CHUNKS)
        u3_l = u_l.reshape(BATCH, SEQ, S5_WIDTH)
        u3_c = u_c.reshape(BATCH, CTX_LEN, S5_WIDTH)

        swa_l = _swa_latent(qk_l, vs_l, qk_c, vs_c, swa_sink[i])

        mla_l = _mla_attn(q_l, [k_c, k_l], [v_c, v_l], SEQ, 1024)

        xl, fl = _out_proj(u3_l, yf, yb, 0, s5_d[i], glu_w_b, s5_glu_b[i], swa_l, mla_l, w_out_b, xl,
                           norm_mix_post[i], mod[2], norm_ffn_pre[i], mod[3], mod[4], i, tm_p, lat_row(tm_p))
        xl = _ffn(xl, fl, ffn_w1, ffn_w2, norm_ffn_post[i], mod[5], i, tm_f, lat_row(tm_f))

        if need_ctx:
            swa_c = _swa_context(qk_c, vs_c, swa_sink[i])
            mla_c = _mla_attn(q_c, [k_c], [v_c], CTX_LEN, 256)
            xc, fc = _out_proj(u3_c, yf, yb, SEQ, s5_d[i], glu_w_b, s5_glu_b[i], swa_c, mla_c, w_out_b, xc,
                               norm_mix_post[i], mod[2], norm_ffn_pre[i], mod[3], mod[4], i, CTX_LEN, ctx_row)
            xc = _ffn(xc, fc, ffn_w1, ffn_w2, norm_ffn_post[i], mod[5], i, tm_f, ctx_row)

    return xl.reshape(BATCH, SEQ, D_MODEL)
```

```python
import functools
import math

import jax
import jax.numpy as jnp
import numpy as np
from jax import lax
from jax.experimental import pallas as pl
from jax.experimental.pallas import tpu as pltpu

F32 = jnp.float32
BF16 = jnp.bfloat16

D_MODEL = 2048
BATCH = 4
SEQ = 2048
DEPTH = 2
GRID_W = 64
CTX_LEN = 256
EPS = 1e-6
ROPE_BASE = 10000.0
NEG_INF = -1e30
BLOCK = 128
HEAD_DIM = 128
S5_WIDTH = 512
S5_GROUP = 16
S5_GROUPS = 32
S5_STATE = 64
S5_LANES = S5_GROUPS * S5_STATE
SWA_HEADS = 6
SWA_KV_HEADS = 2
SWA_REP = SWA_HEADS // SWA_KV_HEADS
MLA_HEADS = 6
MLA_Q_RANK = 768
MLA_KV_RANK = 512
MLA_NOPE = 128
MLA_ROPE = 64
MLA_V = 128
MLA_QK_PAD = 256
MLA_V_PAD = 256
D_FF = 4 * D_MODEL
IN_WIDTHS = (512, 768, 256, 256, 768, 512, 64)
IN_WIDTH = sum(IN_WIDTHS)
OFF_U, OFF_QS, OFF_KS, OFF_VS, OFF_CQ, OFF_CKV, OFF_KR = 0, 512, 1280, 1536, 1792, 2560, 3072

VMEM_LIMIT = 56 * 1024 * 1024


def _cparams(*sem):
    return pltpu.CompilerParams(dimension_semantics=sem, vmem_limit_bytes=VMEM_LIMIT)


def _dot(a, b):
    return jnp.dot(a, b, preferred_element_type=F32)


def _dot_nt(a, b):
    return lax.dot_general(a, b, (((1,), (1,)), ((), ())), preferred_element_type=F32)


def _rms(x, g):
    return x * lax.rsqrt(jnp.mean(x * x, axis=-1, keepdims=True) + EPS) * g


SUB_ROWS = 256


def _sub_tiles(rows):
    return [slice(r, r + SUB_ROWS) for r in range(0, rows, SUB_ROWS)]


def _rope(x, cos, sin, half):
    lane = lax.broadcasted_iota(jnp.int32, x.shape, 1)
    fwd = pltpu.roll(x, 128 - half, 1)
    bwd = pltpu.roll(x, half, 1)
    sw = jnp.where((lane % (2 * half)) < half, fwd, bwd)
    return x * cos + sw * sin


def _ada_kernel(c_ref, w_ref, b_ref, o_ref):
    c = c_ref[...]
    s = c * jax.nn.sigmoid(c)
    o_ref[0] = _dot(s.astype(BF16), w_ref[0].astype(BF16)) + b_ref[0]


def _ada(cvec, ada_w, ada_b):
    tn = 1024
    n = 6 * D_MODEL
    return pl.pallas_call(
        _ada_kernel,
        grid=(DEPTH, n // tn),
        in_specs=[pl.BlockSpec((8, D_MODEL), lambda l, j: (0, 0)),
                  pl.BlockSpec((1, D_MODEL, tn), lambda l, j: (l, 0, j)),
                  pl.BlockSpec((1, 1, tn), lambda l, j: (l, 0, j))],
        out_specs=pl.BlockSpec((1, 8, tn), lambda l, j: (l, 0, j)),
        out_shape=jax.ShapeDtypeStruct((DEPTH, 8, n), F32),
        compiler_params=_cparams("arbitrary", "arbitrary"),
        name="ada",
    )(cvec, ada_w, ada_b.reshape(DEPTH, 1, n))


IN_COLS = 3200
SWA_QK_W = (SWA_HEADS + SWA_KV_HEADS) * HEAD_DIM


def _mix_in_kernel(x_ref, shift_ref, scale_ref, g_ref, w_ref, wq_ref, wkv_ref, qn_ref, kvn_ref,
                   cs_ref, ss_ref, cm_ref, sm_ref,
                   u_ref, qk_ref, vs_ref, q_ref, k_ref, v_ref, h_ref, *, rope):
    for rs in _sub_tiles(x_ref.shape[0]):
        h = _rms(x_ref[rs, :], g_ref[...]) * (1.0 + scale_ref[0]) + shift_ref[0]
        h_ref[rs, :] = h.astype(BF16)

        def proj(lo, width):
            return _dot(h_ref[rs, :], w_ref[:, lo:lo + width])

        u_ref[rs, :] = proj(OFF_U, S5_WIDTH)

        swa_scale = HEAD_DIM ** -0.5
        per = 4
        for c in range(SWA_QK_W // (per * HEAD_DIM)):
            zc = proj(OFF_QS + c * per * HEAD_DIM, per * HEAD_DIM)
            for hh in range(per):
                head = c * per + hh
                xh = zc[:, hh * HEAD_DIM:(hh + 1) * HEAD_DIM]
                if rope:
                    xh = _rope(xh, cs_ref[rs, :], ss_ref[rs, :], HEAD_DIM // 4)
                if head < SWA_HEADS:
                    xh = xh * swa_scale
                qk_ref[rs, head * HEAD_DIM:(head + 1) * HEAD_DIM] = xh.astype(BF16)

        vs_ref[rs, :] = proj(OFF_VS, SWA_KV_HEADS * HEAD_DIM).astype(BF16)

        mla_scale = (MLA_NOPE + MLA_ROPE) ** -0.5
        cq = proj(OFF_CQ, MLA_Q_RANK)
        q = _dot(_rms(cq, qn_ref[...]).astype(BF16), wq_ref[...])
        for hd in range(MLA_HEADS):
            lo = hd * MLA_QK_PAD
            q_ref[rs, lo:lo + MLA_NOPE] = (q[:, lo:lo + MLA_NOPE] * mla_scale).astype(BF16)
            r = q[:, lo + MLA_NOPE:lo + MLA_QK_PAD]
            if rope:
                r = _rope(r, cm_ref[rs, :], sm_ref[rs, :], MLA_ROPE // 4)
            q_ref[rs, lo + MLA_NOPE:lo + MLA_QK_PAD] = (r * mla_scale).astype(BF16)

        ckv = proj(OFF_CKV, MLA_KV_RANK)
        kv = _dot(_rms(ckv, kvn_ref[...]).astype(BF16), wkv_ref[...])
        kr = proj(OFF_KR, 128)
        if rope:
            kr = _rope(kr, cm_ref[rs, :], sm_ref[rs, :], MLA_ROPE // 4)
        kr = kr.astype(BF16)
        for hd in range(MLA_HEADS):
            lo = hd * MLA_QK_PAD
            k_ref[rs, lo:lo + MLA_NOPE] = kv[:, hd * MLA_NOPE:(hd + 1) * MLA_NOPE].astype(BF16)
            k_ref[rs, lo + MLA_NOPE:lo + MLA_QK_PAD] = kr
            vlo = hd * MLA_V_PAD
            vh = kv[:, MLA_HEADS * MLA_NOPE + hd * MLA_V:MLA_HEADS * MLA_NOPE + (hd + 1) * MLA_V].astype(BF16)
            v_ref[rs, vlo:vlo + MLA_V] = vh
            v_ref[rs, vlo + MLA_V:vlo + MLA_V_PAD] = jnp.ones_like(vh)


def _mix_in(x2d, shift, scale, g, w_in, w_uq, w_ukv, q_norm, kv_norm, tabs, layer, tm, mod_row, rope, seq):
    rows = x2d.shape[0]
    once = pl.Buffered(1)
    tab = pl.BlockSpec((tm, 128), lambda i: (i % (seq // tm), 0))

    def res(shape):
        return pl.BlockSpec((None,) + shape, lambda i: (layer, 0, 0), pipeline_mode=once)

    def out(width):
        return pl.BlockSpec((tm, width), lambda i: (i, 0))

    widths = (S5_WIDTH, SWA_QK_W, SWA_KV_HEADS * HEAD_DIM,
              MLA_HEADS * MLA_QK_PAD, MLA_HEADS * MLA_QK_PAD, MLA_HEADS * MLA_V_PAD)
    dtypes = (F32, BF16, BF16, BF16, BF16, BF16)
    return pl.pallas_call(
        functools.partial(_mix_in_kernel, rope=rope),
        grid=(rows // tm,),
        in_specs=[pl.BlockSpec((tm, D_MODEL), lambda i: (i, 0)),
                  pl.BlockSpec((1, 1, D_MODEL), lambda i: (mod_row(i), 0, 0)),
                  pl.BlockSpec((1, 1, D_MODEL), lambda i: (mod_row(i), 0, 0)),
                  pl.BlockSpec((1, D_MODEL), lambda i: (0, 0)),
                  res((D_MODEL, IN_COLS)),
                  res((MLA_Q_RANK, MLA_HEADS * MLA_QK_PAD)),
                  res((MLA_KV_RANK, MLA_HEADS * (MLA_NOPE + MLA_V))),
                  pl.BlockSpec((1, MLA_Q_RANK), lambda i: (0, 0)),
                  pl.BlockSpec((1, MLA_KV_RANK), lambda i: (0, 0)),
                  tab, tab, tab, tab],
        out_specs=[out(w) for w in widths],
        out_shape=[jax.ShapeDtypeStruct((rows, w), dt) for w, dt in zip(widths, dtypes)],
        scratch_shapes=[pltpu.VMEM((tm, D_MODEL), BF16)],
        compiler_params=_cparams("arbitrary"),
        name="mix_in",
    )(x2d, shift, scale, g, w_in, w_uq, w_ukv, q_norm.reshape(1, -1), kv_norm.reshape(1, -1), *tabs)


S5_GH = S5_GROUPS // 2


def _zoh(are, aim, ldt):
    dt = jnp.exp(ldt)
    mag = jnp.exp(are * dt)
    lr = mag * jnp.cos(aim * dt)
    li = mag * jnp.sin(aim * dt)
    den = are * are + aim * aim
    nr = lr - 1.0
    return lr, li, (nr * are + li * aim) / den, (li * are - nr * aim) / den


def _s5_disc_kernel(are_ref, aim_ref, ldt_ref, arec_ref, aimc_ref, ldtc_ref, br_ref, bi_ref, cr_ref, ci_ref,
                    lam_ref, wbr_ref, wbi_ref, wcr_ref, wci_ref):
    lr, li, _, _ = _zoh(are_ref[0], aim_ref[0], ldt_ref[0])
    lam_ref[0, 0:1, :] = lr
    lam_ref[0, 1:2, :] = li
    gh, n, p = S5_GH, S5_STATE, S5_GROUP
    kin, kst = gh * p, gh * n

    def iota(shape, axis):
        return lax.broadcasted_iota(jnp.int32, shape, axis)

    tile_n = jnp.where(iota((n, kst), 0) == (iota((n, kst), 1) & (n - 1)), 1.0, 0.0).astype(BF16)
    tile_p = jnp.where(iota((p, kin), 0) == (iota((p, kin), 1) & (p - 1)), 1.0, 0.0).astype(BF16)
    lp, ln = p.bit_length() - 1, n.bit_length() - 1
    diag_b = (iota((kin, kst), 0) >> lp) == (iota((kin, kst), 1) >> ln)
    diag_c = (iota((kst, kin), 0) >> ln) == (iota((kst, kin), 1) >> lp)
    for h in range(2):
        _, _, fr, fi = _zoh(arec_ref[0, h], aimc_ref[0, h], ldtc_ref[0, h])
        fr = jnp.broadcast_to(fr[:, None, :], (gh, p, n)).reshape(kin, n)
        fi = jnp.broadcast_to(fi[:, None, :], (gh, p, n)).reshape(kin, n)
        bbr = (fr * br_ref[0, h] - fi * bi_ref[0, h]).astype(BF16)
        bbi = (fr * bi_ref[0, h] + fi * br_ref[0, h]).astype(BF16)
        wbr_ref[0, h] = jnp.where(diag_b, _dot(bbr, tile_n), 0.0).astype(BF16)
        wbi_ref[0, h] = jnp.where(diag_b, _dot(bbi, tile_n), 0.0).astype(BF16)
        wcr_ref[0, h] = jnp.where(diag_c, _dot(cr_ref[0, h].astype(BF16), tile_p), 0.0).astype(BF16)
        wci_ref[0, h] = jnp.where(diag_c, _dot(ci_ref[0, h].astype(BF16), tile_p), 0.0).astype(BF16)


def _s5_disc(a_re, a_im, log_dt, b_re, b_im, c_re, c_im):
    nd = DEPTH * 2
    gh, n, p = S5_GH, S5_STATE, S5_GROUP
    kin, kst = gh * p, gh * n
    are = a_re.reshape(nd, 1, S5_LANES)
    aim = a_im.reshape(nd, 1, S5_LANES)
    ldt_gn = jnp.broadcast_to(log_dt[..., None], a_re.shape)
    ldt = ldt_gn.reshape(nd, 1, S5_LANES)
    compact = lambda a: a.reshape(nd, 2, gh, n)
    bt = lambda b: b.reshape(nd, 2, gh, n, p).transpose(0, 1, 2, 4, 3).reshape(nd, 2, kin, n)
    ct = lambda c: c.reshape(nd, 2, gh, p, n).transpose(0, 1, 2, 4, 3).reshape(nd, 2, kst, p)
    vec = pl.BlockSpec((1, 1, S5_LANES), lambda d: (d, 0, 0))
    cvec = pl.BlockSpec((1, 2, gh, n), lambda d: (d, 0, 0, 0))
    bin_spec = pl.BlockSpec((1, 2, kin, n), lambda d: (d, 0, 0, 0))
    cin_spec = pl.BlockSpec((1, 2, kst, p), lambda d: (d, 0, 0, 0))
    bspec = pl.BlockSpec((1, 2, kin, kst), lambda d: (d, 0, 0, 0))
    cspec = pl.BlockSpec((1, 2, kst, kin), lambda d: (d, 0, 0, 0))
    return pl.pallas_call(
        _s5_disc_kernel,
        grid=(nd,),
        in_specs=[vec, vec, vec, cvec, cvec, cvec, bin_spec, bin_spec, cin_spec, cin_spec],
        out_specs=[pl.BlockSpec((1, 2, S5_LANES), lambda d: (d, 0, 0)), bspec, bspec, cspec, cspec],
        out_shape=[jax.ShapeDtypeStruct((nd, 2, S5_LANES), F32),
                   jax.ShapeDtypeStruct((nd, 2, kin, kst), BF16),
                   jax.ShapeDtypeStruct((nd, 2, kin, kst), BF16),
                   jax.ShapeDtypeStruct((nd, 2, kst, kin), BF16),
                   jax.ShapeDtypeStruct((nd, 2, kst, kin), BF16)],
        compiler_params=_cparams("arbitrary"),
        name="s5_disc",
    )(are, aim, ldt, compact(a_re), compact(a_im), compact(ldt_gn),
      bt(b_re), bt(b_im), ct(c_re), ct(c_im))


S5_TC = 128
S5_CTX_CHUNKS = CTX_LEN // S5_TC
S5_LAT_CHUNKS = SEQ // S5_TC


def _s5_scan_kernel(ucf_ref, ulf_ref, ucb_ref, ulb_ref, lam_ref, wbr_ref, wbi_ref, wcr_ref, wci_ref,
                    yf_ref, yb_ref, fr_ref, fi_ref, gr_ref, gi_ref, sr_ref, si_ref, ut_ref, yt_ref):
    c = pl.program_id(0)
    tc = S5_TC
    kin, kst = S5_WIDTH // 2, S5_LANES // 2
    nlb = kst // 128

    @pl.when(c == 0)
    def _():
        sr_ref[...] = jnp.zeros_like(sr_ref)
        si_ref[...] = jnp.zeros_like(si_ref)

    in_ctx = c < S5_CTX_CHUNKS
    state = [(fr_ref, fi_ref), (gr_ref, gi_ref)]
    nub = S5_WIDTH // 128

    u_dir = []
    for d, (uc_ref, ul_ref) in enumerate(((ucf_ref, ulf_ref), (ucb_ref, ulb_ref))):
        u = jnp.where(in_ctx, uc_ref[...], ul_ref[...])
        for j in range(nub):
            for b in range(BATCH):
                ut_ref[d, j, pl.ds(b, tc, stride=BATCH), :] = u[b, :, j * 128:(j + 1) * 128]
        u_dir.append(jnp.concatenate([ut_ref[d, j] for j in range(nub)], axis=1).astype(BF16))

    def project(d, h, part):
        w_ref = (wbr_ref, wbi_ref)[part]
        dst = state[d][part]
        p = _dot(u_dir[d][:, h * kin:(h + 1) * kin], w_ref[d, h])
        for j in range(nlb):
            dst[h * nlb + j] = p[:, j * 128:(j + 1) * 128]

    def scan_steps(h, k0, k1, carry):
        ls = slice(h * nlb, (h + 1) * nlb)
        fwd_rows = lax.broadcasted_iota(jnp.int32, (nlb, 8, 128), 1) < BATCH
        la_r = jnp.where(fwd_rows, lam_ref[0, 0, ls], lam_ref[1, 0, ls])
        la_i = jnp.where(fwd_rows, lam_ref[0, 1, ls], lam_ref[1, 1, ls])
        lb_r = jnp.where(fwd_rows, lam_ref[1, 0, ls], lam_ref[0, 0, ls])
        lb_i = jnp.where(fwd_rows, lam_ref[1, 1, ls], lam_ref[0, 1, ls])
        s_r, s_i = carry
        for k in range(k0, k1):
            rf = slice(k * 8, (k + 1) * 8)
            kb = tc // 2 - 1 - k
            rb = slice(kb * 8, (kb + 1) * 8)
            f_r, f_i = fr_ref[ls, rf, :], fi_ref[ls, rf, :]
            g_r, g_i = gr_ref[ls, rb, :], gi_ref[ls, rb, :]
            a_r = la_r * s_r - la_i * s_i + jnp.where(fwd_rows, f_r, g_r)
            a_i = la_r * s_i + la_i * s_r + jnp.where(fwd_rows, f_i, g_i)
            t_r = pltpu.roll(a_r, BATCH, 1)
            t_i = pltpu.roll(a_i, BATCH, 1)
            b_r = lb_r * t_r - lb_i * t_i + jnp.where(fwd_rows, g_r, f_r)
            b_i = lb_r * t_i + lb_i * t_r + jnp.where(fwd_rows, g_i, f_i)
            fr_ref[ls, rf, :] = jnp.where(fwd_rows, a_r, b_r)
            fi_ref[ls, rf, :] = jnp.where(fwd_rows, a_i, b_i)
            gr_ref[ls, rb, :] = jnp.where(fwd_rows, b_r, a_r)
            gi_ref[ls, rb, :] = jnp.where(fwd_rows, b_i, a_i)
            s_r, s_i = pltpu.roll(b_r, BATCH, 1), pltpu.roll(b_i, BATCH, 1)
        return s_r, s_i

    def half_rows(src, h):
        return jnp.concatenate([src[h * nlb + j] for j in range(nlb)], axis=1).astype(BF16)

    def readout(d, h):
        y = _dot(half_rows(state[d][0], h), wcr_ref[d, h]) - _dot(half_rows(state[d][1], h), wci_ref[d, h])
        for j in range(kin // 128):
            yt_ref[d, h * (kin // 128) + j] = y[:, j * 128:(j + 1) * 128]

    def emit(d):
        y_ref = (yf_ref, yb_ref)[d]
        for b in range(BATCH):
            y_ref[b] = jnp.concatenate([yt_ref[d, j, pl.ds(b, tc, stride=BATCH), :] for j in range(nub)], axis=1)

    quarters = 4
    per = tc // 2 // quarters
    pieces = [(d, part) for d in range(2) for part in range(2)]
    for d, part in pieces:
        project(d, 0, part)
    carry = (sr_ref[0:nlb], si_ref[0:nlb])
    for q, (d, part) in enumerate(pieces):
        carry = scan_steps(0, q * per, (q + 1) * per, carry)
        project(d, 1, part)
    sr_ref[0:nlb], si_ref[0:nlb] = carry
    carry = (sr_ref[nlb:2 * nlb], si_ref[nlb:2 * nlb])
    for q in range(quarters):
        carry = scan_steps(1, q * per, (q + 1) * per, carry)
        if q % 2 == 1:
            readout(q // 2, 0)
    sr_ref[nlb:2 * nlb], si_ref[nlb:2 * nlb] = carry
    for d in range(2):
        readout(d, 1)
        emit(d)


def _s5_scan(u_c, u_l, lam, wbr, wbi, wcr, wci, layer, skip_chunks):
    tc = S5_TC
    nchunk = S5_CTX_CHUNKS + S5_LAT_CHUNKS
    kin, kst = S5_WIDTH // 2, S5_LANES // 2
    zc3 = u_c.reshape(BATCH, CTX_LEN, S5_WIDTH)
    zl3 = u_l.reshape(BATCH, SEQ, S5_WIDTH)

    def bwd_chunk(c):
        return jnp.where(c < S5_CTX_CHUNKS, S5_CTX_CHUNKS - 1 - c, nchunk + S5_CTX_CHUNKS - 1 - c)

    def ctx_spec(chunk_of):
        return pl.BlockSpec((BATCH, tc, S5_WIDTH),
                            lambda c: (0, jnp.clip(chunk_of(c), 0, S5_CTX_CHUNKS - 1), 0))

    def lat_spec(chunk_of):
        return pl.BlockSpec((BATCH, tc, S5_WIDTH),
                            lambda c: (0, jnp.clip(chunk_of(c) - S5_CTX_CHUNKS, 0, S5_LAT_CHUNKS - 1), 0))

    def out_block(chunk):
        return jnp.where(chunk >= S5_CTX_CHUNKS, chunk - S5_CTX_CHUNKS, S5_LAT_CHUNKS + chunk)

    def out_spec(chunk_of):
        return pl.BlockSpec((BATCH, tc, S5_WIDTH),
                            lambda c: (0, out_block(chunk_of(jnp.maximum(c, skip_chunks))), 0))

    once = pl.Buffered(1)
    bspec = pl.BlockSpec((2, 2, kin, kst), lambda c: (layer, 0, 0, 0), pipeline_mode=once)
    cspec = pl.BlockSpec((2, 2, kst, kin), lambda c: (layer, 0, 0, 0), pipeline_mode=once)
    t_out = (nchunk - skip_chunks) * tc
    nblk = S5_LANES // 128
    big = pltpu.VMEM((nblk, tc * BATCH, 128), F32)
    state = pltpu.VMEM((nblk, 8, 128), F32)
    narrow = pltpu.VMEM((2, S5_WIDTH // 128, tc * BATCH, 128), F32)
    lam = lam.reshape(DEPTH * 2, 2, nblk, 1, 128)
    fwd_chunk = lambda c: c
    return pl.pallas_call(
        _s5_scan_kernel,
        grid=(nchunk,),
        in_specs=[ctx_spec(fwd_chunk), lat_spec(fwd_chunk), ctx_spec(bwd_chunk), lat_spec(bwd_chunk),
                  pl.BlockSpec((2, 2, nblk, 1, 128), lambda c: (layer, 0, 0, 0, 0)),
                  bspec, bspec, cspec, cspec],
        out_specs=[out_spec(fwd_chunk), out_spec(bwd_chunk)],
        out_shape=[jax.ShapeDtypeStruct((BATCH, t_out, S5_WIDTH), F32)] * 2,
        scratch_shapes=[big, big, big, big, state, state, narrow, narrow],
        compiler_params=_cparams("arbitrary"),
        name="s5_scan",
    )(zc3, zl3, zc3, zl3, lam, wbr, wbi, wcr, wci)


def _sink_rows(sink_ref, g, rows):
    r = lax.broadcasted_iota(jnp.int32, (rows, 1), 0) // (rows // SWA_REP)
    s0, s1, s2 = sink_ref[SWA_REP * g], sink_ref[SWA_REP * g + 1], sink_ref[SWA_REP * g + 2]
    return jnp.where(r == 0, s0, jnp.where(r == 1, s1, s2))


def _stack_heads(x, n):
    return jnp.concatenate([x[:, h * HEAD_DIM:(h + 1) * HEAD_DIM] for h in range(n)], axis=0)


def _unstack_heads(x, n):
    rows = x.shape[0] // n
    return jnp.concatenate([x[h * rows:(h + 1) * rows] for h in range(n)], axis=1)


SWA_QB = 4


def _with_ones(v):
    return jnp.concatenate([v, jnp.ones_like(v)], axis=1)


def _swa_latent_kernel(sink_ref, q_ref, kp_ref, km_ref, kn_ref, vp_ref, vm_ref, vn_ref,
                       kx_ref, vx_ref, o_ref):
    n = pl.program_id(1)
    rows = SWA_REP * BLOCK
    qi = lax.broadcasted_iota(jnp.int32, (rows, 3 * BLOCK), 0) % BLOCK
    kj = lax.broadcasted_iota(jnp.int32, (rows, 3 * BLOCK), 1)
    in_window = (kj >= qi) & (kj <= qi + 2 * BLOCK)
    for g in range(SWA_KV_HEADS):
        gs = slice(g * HEAD_DIM, (g + 1) * HEAD_DIM)
        kspan = jnp.concatenate([kp_ref[:, gs], km_ref[:, gs], kn_ref[:, gs]], axis=0)
        vspan = _with_ones(jnp.concatenate([vp_ref[:, gs], vm_ref[:, gs], vn_ref[:, gs]], axis=0))
        kx = kx_ref[:, gs]
        vx = _with_ones(vx_ref[:, gs])
        sk = _sink_rows(sink_ref, g, rows)
        for j in range(SWA_QB):
            blk = n * SWA_QB + j
            qs = slice(g * SWA_REP * HEAD_DIM, (g + 1) * SWA_REP * HEAD_DIM)
            q = _stack_heads(q_ref[j * BLOCK:(j + 1) * BLOCK, qs], SWA_REP)
            kb = kspan[j * BLOCK:(j + 3) * BLOCK]
            vb = vspan[j * BLOCK:(j + 3) * BLOCK]
            s_ctx = _dot_nt(q, kx)
            s_band = _dot_nt(q, kb)
            kpos = (blk - 1) * BLOCK + kj
            valid = in_window & (kpos >= 0) & (kpos < SEQ)
            s_band = jnp.where(valid, s_band, NEG_INF)
            m = jnp.maximum(jnp.maximum(jnp.max(s_ctx, axis=-1, keepdims=True),
                                        jnp.max(s_band, axis=-1, keepdims=True)), sk)
            p_ctx = jnp.exp(s_ctx - m).astype(BF16)
            p_band = jnp.exp(s_band - m).astype(BF16)
            o = _dot(p_ctx, vx) + _dot(p_band, vb)
            den = o[:, HEAD_DIM:] + jnp.exp(sk - m)
            o_ref[j * BLOCK:(j + 1) * BLOCK, qs] = _unstack_heads(
                o[:, :HEAD_DIM] / den, SWA_REP).astype(o_ref.dtype)


def _swa_latent(qk, vs, qk_c, vs_c, sink):
    nb = SEQ // BLOCK
    ng = nb // SWA_QB
    span = SWA_QB * BLOCK
    kvw = SWA_KV_HEADS * HEAD_DIM
    kcol = SWA_HEADS * HEAD_DIM // kvw
    vcol = 0
    kxcol = kcol

    def edge(col, blk_of):
        return pl.BlockSpec((BLOCK, kvw), lambda b, n, s: (b * nb + jnp.clip(blk_of(n), 0, nb - 1), col))

    def main(col):
        return pl.BlockSpec((span, kvw), lambda b, n, s: (b * ng + n, col))

    prev_blk = lambda n: n * SWA_QB - 1
    next_blk = lambda n: (n + 1) * SWA_QB
    grid_spec = pltpu.PrefetchScalarGridSpec(
        num_scalar_prefetch=1,
        grid=(BATCH, ng),
        in_specs=[pl.BlockSpec((span, SWA_HEADS * HEAD_DIM), lambda b, n, s: (b * ng + n, 0)),
                  edge(kcol, prev_blk), main(kcol), edge(kcol, next_blk),
                  edge(vcol, prev_blk), main(vcol), edge(vcol, next_blk),
                  pl.BlockSpec((CTX_LEN, kvw), lambda b, n, s: (b, kxcol)),
                  pl.BlockSpec((CTX_LEN, kvw), lambda b, n, s: (b, vcol))],
        out_specs=pl.BlockSpec((span, SWA_HEADS * HEAD_DIM), lambda b, n, s: (b * ng + n, 0)),
    )
    return pl.pallas_call(
        _swa_latent_kernel,
        grid_spec=grid_spec,
        out_shape=jax.ShapeDtypeStruct((BATCH * SEQ, SWA_HEADS * HEAD_DIM), BF16),
        compiler_params=_cparams("arbitrary", "arbitrary"),
        name="swa_latent",
    )(sink, qk, qk, qk, qk, vs, vs, vs, qk_c, vs_c)


def _swa_context_kernel(sink_ref, q_ref, k_ref, v_ref, o_ref):
    g = pl.program_id(1)
    q = _stack_heads(q_ref[...], SWA_REP)
    s = _dot_nt(q, k_ref[...])
    sk = _sink_rows(sink_ref, g, SWA_REP * CTX_LEN)
    m = jnp.maximum(jnp.max(s, axis=-1, keepdims=True), sk)
    p = jnp.exp(s - m)
    den = jnp.sum(p, axis=-1, keepdims=True) + jnp.exp(sk - m)
    o = _dot(p.astype(BF16), v_ref[...]) / den
    o_ref[...] = _unstack_heads(o, SWA_REP).astype(o_ref.dtype)


def _swa_context(qk_c, vs_c, sink):
    qw = SWA_REP * HEAD_DIM
    grid_spec = pltpu.PrefetchScalarGridSpec(
        num_scalar_prefetch=1,
        grid=(BATCH, SWA_KV_HEADS),
        in_specs=[pl.BlockSpec((CTX_LEN, qw), lambda b, g, s: (b, g)),
                  pl.BlockSpec((CTX_LEN, HEAD_DIM), lambda b, g, s: (b, SWA_HEADS + g)),
                  pl.BlockSpec((CTX_LEN, HEAD_DIM), lambda b, g, s: (b, g))],
        out_specs=pl.BlockSpec((CTX_LEN, qw), lambda b, g, s: (b, g)),
    )
    return pl.pallas_call(
        _swa_context_kernel,
        grid_spec=grid_spec,
        out_shape=jax.ShapeDtypeStruct((BATCH * CTX_LEN, SWA_HEADS * HEAD_DIM), BF16),
        compiler_params=_cparams("arbitrary", "arbitrary"),
        name="swa_context",
    )(sink, qk_c, qk_c, vs_c)


def _mla_attn_kernel(*refs, nseg):
    q_ref = refs[0]
    k_refs = refs[1:1 + nseg]
    v_refs = refs[1 + nseg:1 + 2 * nseg]
    o_ref = refs[1 + 2 * nseg]
    for h in range(MLA_HEADS):
        q = q_ref[:, h * MLA_QK_PAD:(h + 1) * MLA_QK_PAD]
        s = [_dot_nt(q, k[:, h * MLA_QK_PAD:(h + 1) * MLA_QK_PAD]) for k in k_refs]
        m = functools.reduce(jnp.maximum, [jnp.max(x, axis=-1, keepdims=True) for x in s])
        p = [jnp.exp(x - m).astype(BF16) for x in s]
        o = sum(_dot(x, v[:, h * MLA_V_PAD:(h + 1) * MLA_V_PAD]) for x, v in zip(p, v_refs))
        o_ref[:, h * MLA_V:(h + 1) * MLA_V] = (o[:, :MLA_V] / o[:, MLA_V:]).astype(o_ref.dtype)


def _mla_attn(q, ks, vs, n_q, tq):
    nseg = len(ks)
    nq = n_q // tq
    lens = [k.shape[0] // BATCH for k in ks]
    kw, vw = MLA_HEADS * MLA_QK_PAD, MLA_HEADS * MLA_V
    in_specs = [pl.BlockSpec((tq, kw), lambda b, i: (b * nq + i, 0))]
    in_specs += [pl.BlockSpec((n, kw), lambda b, i: (b, 0)) for n in lens]
    in_specs += [pl.BlockSpec((n, MLA_HEADS * MLA_V_PAD), lambda b, i: (b, 0)) for n in lens]
    return pl.pallas_call(
        functools.partial(_mla_attn_kernel, nseg=nseg),
        grid=(BATCH, nq),
        in_specs=in_specs,
        out_specs=pl.BlockSpec((tq, vw), lambda b, i: (b * nq + i, 0)),
        out_shape=jax.ShapeDtypeStruct((BATCH * n_q, vw), BF16),
        compiler_params=_cparams("arbitrary", "arbitrary"),
        name="mla_attn",
    )(q, *ks, *vs)


def _out_proj_kernel(u_ref, yf_ref, yb_ref, d_ref, gw_ref, gb_ref, a2_ref, a3_ref, w_ref, x_ref, g_ref, gate_ref,
                     fg_ref, fshift_ref, fscale_ref, o_ref, f_ref):
    nc = 512
    k1 = S5_WIDTH
    k2 = k1 + a2_ref.shape[1]
    for rs in _sub_tiles(o_ref.shape[0]):
        y = d_ref[...] * u_ref[0, rs, :] + yf_ref[0, rs, :] + yb_ref[0, rs, :]
        gl = jax.nn.gelu(y)
        a1 = (gl * jax.nn.sigmoid(_dot(gl.astype(BF16), gw_ref[...]) + gb_ref[...])).astype(BF16)
        for n in range(D_MODEL // nc):
            cs = slice(n * nc, (n + 1) * nc)
            o_ref[rs, cs] = (_dot(a2_ref[rs, :], w_ref[k1:k2, cs]) + _dot(a3_ref[rs, :], w_ref[k2:D_MODEL, cs])
                             + _dot(a1, w_ref[0:k1, cs]))
        xn = x_ref[rs, :] + gate_ref[0] * _rms(o_ref[rs, :], g_ref[...])
        o_ref[rs, :] = xn
        f_ref[rs, :] = (_rms(xn, fg_ref[...]) * (1.0 + fscale_ref[0]) + fshift_ref[0]).astype(BF16)


def _out_proj(u3, yf, yb, y_off, d, glu_w, glu_b, a2, a3, w, x2d, g, gate, fg, fshift, fscale, layer, tm, mod_row):
    rows = a2.shape[0]
    per_b = u3.shape[1] // tm
    once = pl.Buffered(1)
    uspec = pl.BlockSpec((1, tm, S5_WIDTH), lambda i: (i // per_b, i % per_b, 0))
    yspec = pl.BlockSpec((1, tm, S5_WIDTH), lambda i: (i // per_b, i % per_b + y_off // tm, 0))
    vec5 = pl.BlockSpec((1, S5_WIDTH), lambda i: (0, 0))
    vec = pl.BlockSpec((1, D_MODEL), lambda i: (0, 0))
    mod = pl.BlockSpec((1, 1, D_MODEL), lambda i: (mod_row(i), 0, 0))
    row = pl.BlockSpec((tm, D_MODEL), lambda i: (i, 0))
    return pl.pallas_call(
        _out_proj_kernel,
        grid=(rows // tm,),
        in_specs=[uspec, yspec, yspec, vec5,
                  pl.BlockSpec((None, S5_WIDTH, S5_WIDTH), lambda i: (layer, 0, 0), pipeline_mode=once), vec5,
                  pl.BlockSpec((tm, a2.shape[1]), lambda i: (i, 0)),
                  pl.BlockSpec((tm, a3.shape[1]), lambda i: (i, 0)),
                  pl.BlockSpec((None, D_MODEL, D_MODEL), lambda i: (layer, 0, 0), pipeline_mode=once),
                  row, vec, mod, vec, mod, mod],
        out_specs=[row, row],
        out_shape=[jax.ShapeDtypeStruct((rows, D_MODEL), F32), jax.ShapeDtypeStruct((rows, D_MODEL), BF16)],
        compiler_params=_cparams("arbitrary"),
        name="out_proj",
    )(u3, yf, yb, d.reshape(1, -1), glu_w, glu_b.reshape(1, -1), a2, a3, w, x2d, g.reshape(1, -1), gate,
      fg.reshape(1, -1), fshift, fscale)


def _ffn_kernel(x_ref, f_ref, w1_ref, w2_ref, gpost_ref, gate_ref, o_ref, h_ref):
    j = pl.program_id(1)

    @pl.when(j == 0)
    def _():
        o_ref[...] = jnp.zeros_like(o_ref)

    h = _dot(f_ref[...], w1_ref[...].astype(BF16))
    h_ref[...] = jnp.square(jnp.maximum(h, 0.0)).astype(BF16)
    nc = 512
    for n in range(D_MODEL // nc):
        cs = slice(n * nc, (n + 1) * nc)
        o_ref[:, cs] += _dot(h_ref[...], w2_ref[:, cs].astype(BF16))

    @pl.when(j == pl.num_programs(1) - 1)
    def _():
        o_ref[...] = x_ref[...] + gate_ref[0] * _rms(o_ref[...], gpost_ref[...])


def _ffn(x2d, f2d, w1, w2, gpost, gate, layer, tm, mod_row):
    rows = x2d.shape[0]
    tf = 512
    mod = pl.BlockSpec((1, 1, D_MODEL), lambda i, j: (mod_row(i), 0, 0))
    vec = pl.BlockSpec((1, D_MODEL), lambda i, j: (0, 0))
    once = pl.Buffered(1)
    return pl.pallas_call(
        _ffn_kernel,
        grid=(rows // tm, D_FF // tf),
        in_specs=[pl.BlockSpec((tm, D_MODEL), lambda i, j: (i, 0)),
                  pl.BlockSpec((tm, D_MODEL), lambda i, j: (i, 0), pipeline_mode=once),
                  pl.BlockSpec((None, D_MODEL, tf), lambda i, j: (layer, 0, j)),
                  pl.BlockSpec((None, tf, D_MODEL), lambda i, j: (layer, j, 0)),
                  vec, mod],
        out_specs=pl.BlockSpec((tm, D_MODEL), lambda i, j: (i, 0), pipeline_mode=once),
        out_shape=jax.ShapeDtypeStruct((rows, D_MODEL), F32),
        scratch_shapes=[pltpu.VMEM((tm, tf), BF16)],
        compiler_params=_cparams("arbitrary", "arbitrary"),
        name="ffn",
    )(x2d, f2d, w1, w2, gpost.reshape(1, -1), gate)


def _rope_tables():
    t = np.arange(SEQ)
    row = (t // GRID_W).astype(np.float32)[:, None]
    col = (t % GRID_W).astype(np.float32)[:, None]

    def tables(rot_dim):
        quarter = rot_dim // 4
        inv_freq = np.float32(ROPE_BASE) ** (-np.arange(quarter, dtype=np.float32) / np.float32(quarter))
        ar, ac = row * inv_freq, col * inv_freq
        cos = np.concatenate([np.cos(ar), np.cos(ar), np.cos(ac), np.cos(ac)], axis=1)
        sin = np.concatenate([-np.sin(ar), np.sin(ar), -np.sin(ac), np.sin(ac)], axis=1)
        pad = 128 - rot_dim
        if pad:
            cos = np.concatenate([cos, np.ones((SEQ, pad), np.float32)], axis=1)
            sin = np.concatenate([sin, np.zeros((SEQ, pad), np.float32)], axis=1)
        return jnp.asarray(cos, F32), jnp.asarray(sin, F32)

    return tables(HEAD_DIM), tables(MLA_ROPE)


def kernel(x, c, ctx, c_ctx, ada_w, ada_b, norm_mix_pre, norm_mix_post, norm_ffn_pre, norm_ffn_post, w_in, w_out, s5_a_re, s5_a_im, s5_log_dt, s5_b_re, s5_b_im, s5_c_re, s5_c_im, s5_d, s5_glu_w, s5_glu_b, swa_sink, mla_q_norm, mla_w_uq, mla_kv_norm, mla_w_ukv, ffn_w1, ffn_w2):
    (cos_swa, sin_swa), (cos_mla, sin_mla) = _rope_tables()
    tabs = (cos_swa, sin_swa, cos_mla, sin_mla)

    cvec = jnp.concatenate([c, c_ctx[None, :], jnp.zeros((3, D_MODEL), F32)], axis=0)
    mods = _ada(cvec, ada_w, ada_b)

    tm_p = 512
    tm_f = 1024

    def lat_row(tm):
        return lambda i: i // (SEQ // tm)

    ctx_row = lambda i: 4

    xl = x.reshape(BATCH * SEQ, D_MODEL)
    xc = ctx.reshape(BATCH * CTX_LEN, D_MODEL)

    w_in_pad = jnp.pad(w_in, ((0, 0), (0, 0), (0, IN_COLS - IN_WIDTH))).astype(BF16)
    w_out_b = w_out.astype(BF16)
    glu_w_b = s5_glu_w.astype(BF16)
    w_uq_pad = jnp.pad(mla_w_uq.reshape(DEPTH, MLA_Q_RANK, MLA_HEADS, MLA_NOPE + MLA_ROPE),
                       ((0, 0), (0, 0), (0, 0), (0, MLA_QK_PAD - MLA_NOPE - MLA_ROPE))
                       ).reshape(DEPTH, MLA_Q_RANK, MLA_HEADS * MLA_QK_PAD).astype(BF16)
    w_ukv4 = mla_w_ukv.reshape(DEPTH, MLA_KV_RANK, MLA_HEADS, MLA_NOPE + MLA_V)
    w_ukv_perm = jnp.concatenate([w_ukv4[..., :MLA_NOPE].reshape(DEPTH, MLA_KV_RANK, -1),
                                  w_ukv4[..., MLA_NOPE:].reshape(DEPTH, MLA_KV_RANK, -1)], axis=2).astype(BF16)
    s5_w = _s5_disc(s5_a_re, s5_a_im, s5_log_dt, s5_b_re, s5_b_im, s5_c_re, s5_c_im)

    for i in range(DEPTH):
        need_ctx = i < DEPTH - 1
        mod = [mods[i, :, k * D_MODEL:(k + 1) * D_MODEL].reshape(8, 1, D_MODEL) for k in range(6)]
        g_pre = norm_mix_pre[i].reshape(1, -1)

        u_l, qk_l, vs_l, q_l, k_l, v_l = _mix_in(
            xl, mod[0], mod[1], g_pre, w_in_pad, w_uq_pad, w_ukv_perm, mla_q_norm[i], mla_kv_norm[i],
            tabs, i, tm_p, lat_row(tm_p), True, SEQ)
        u_c, qk_c, vs_c, q_c, k_c, v_c = _mix_in(
            xc, mod[0], mod[1], g_pre, w_in_pad, w_uq_pad, w_ukv_perm, mla_q_norm[i], mla_kv_norm[i],
            tabs, i, tm_p, ctx_row, False, tm_p)

        yf, yb = _s5_scan(u_c, u_l, *s5_w, i, 0 if need_ctx else S5_CTX_CHUNKS)
        u3_l = u_l.reshape(BATCH, SEQ, S5_WIDTH)
        u3_c = u_c.reshape(BATCH, CTX_LEN, S5_WIDTH)

        swa_l = _swa_latent(qk_l, vs_l, qk_c, vs_c, swa_sink[i])

        mla_l = _mla_attn(q_l, [k_c, k_l], [v_c, v_l], SEQ, 512)

        xl, fl = _out_proj(u3_l, yf, yb, 0, s5_d[i], glu_w_b, s5_glu_b[i], swa_l, mla_l, w_out_b, xl,
                           norm_mix_post[i], mod[2], norm_ffn_pre[i], mod[3], mod[4], i, tm_p, lat_row(tm_p))
        xl = _ffn(xl, fl, ffn_w1, ffn_w2, norm_ffn_post[i], mod[5], i, tm_f, lat_row(tm_f))

        if need_ctx:
            swa_c = _swa_context(qk_c, vs_c, swa_sink[i])
            mla_c = _mla_attn(q_c, [k_c], [v_c], CTX_LEN, 256)
            xc, fc = _out_proj(u3_c, yf, yb, SEQ, s5_d[i], glu_w_b, s5_glu_b[i], swa_c, mla_c, w_out_b, xc,
                               norm_mix_post[i], mod[2], norm_ffn_pre[i], mod[3], mod[4], i, CTX_LEN, ctx_row)
            xc = _ffn(xc, fc, ffn_w1, ffn_w2, norm_ffn_post[i], mod[5], i, tm_f, ctx_row)

    return xl.reshape(BATCH, SEQ, D_MODEL)
```

```python
import functools
import math

import jax
import jax.numpy as jnp
import numpy as np
from jax import lax
from jax.experimental import pallas as pl
from jax.experimental.pallas import tpu as pltpu

F32 = jnp.float32
BF16 = jnp.bfloat16

D_MODEL = 2048
BATCH = 4
SEQ = 2048
DEPTH = 2
GRID_W = 64
CTX_LEN = 256
EPS = 1e-6
ROPE_BASE = 10000.0
NEG_INF = -1e30
BLOCK = 128
HEAD_DIM = 128
S5_WIDTH = 512
S5_GROUP = 16
S5_GROUPS = 32
S5_STATE = 64
S5_LANES = S5_GROUPS * S5_STATE
SWA_HEADS = 6
SWA_KV_HEADS = 2
SWA_REP = SWA_HEADS // SWA_KV_HEADS
MLA_HEADS = 6
MLA_Q_RANK = 768
MLA_KV_RANK = 512
MLA_NOPE = 128
MLA_ROPE = 64
MLA_V = 128
MLA_QK_PAD = 256
MLA_V_PAD = 256
D_FF = 4 * D_MODEL
IN_WIDTHS = (512, 768, 256, 256, 768, 512, 64)
IN_WIDTH = sum(IN_WIDTHS)
OFF_U, OFF_QS, OFF_KS, OFF_VS, OFF_CQ, OFF_CKV, OFF_KR = 0, 512, 1280, 1536, 1792, 2560, 3072

VMEM_LIMIT = 56 * 1024 * 1024


def _cparams(*sem):
    return pltpu.CompilerParams(dimension_semantics=sem, vmem_limit_bytes=VMEM_LIMIT)


def _dot(a, b):
    return jnp.dot(a, b, preferred_element_type=F32)


def _dot_nt(a, b):
    return lax.dot_general(a, b, (((1,), (1,)), ((), ())), preferred_element_type=F32)


def _rms(x, g):
    return x * lax.rsqrt(jnp.mean(x * x, axis=-1, keepdims=True) + EPS) * g


SUB_ROWS = 256


def _sub_tiles(rows):
    return [slice(r, r + SUB_ROWS) for r in range(0, rows, SUB_ROWS)]


def _rope(x, cos, sin, half):
    lane = lax.broadcasted_iota(jnp.int32, x.shape, 1)
    fwd = pltpu.roll(x, 128 - half, 1)
    bwd = pltpu.roll(x, half, 1)
    sw = jnp.where((lane % (2 * half)) < half, fwd, bwd)
    return x * cos + sw * sin


def _ada_kernel(c_ref, w_ref, b_ref, o_ref):
    c = c_ref[...]
    s = c * jax.nn.sigmoid(c)
    o_ref[0] = _dot(s.astype(BF16), w_ref[0].astype(BF16)) + b_ref[0]


def _ada(cvec, ada_w, ada_b):
    tn = 1024
    n = 6 * D_MODEL
    return pl.pallas_call(
        _ada_kernel,
        grid=(DEPTH, n // tn),
        in_specs=[pl.BlockSpec((8, D_MODEL), lambda l, j: (0, 0)),
                  pl.BlockSpec((1, D_MODEL, tn), lambda l, j: (l, 0, j)),
                  pl.BlockSpec((1, 1, tn), lambda l, j: (l, 0, j))],
        out_specs=pl.BlockSpec((1, 8, tn), lambda l, j: (l, 0, j)),
        out_shape=jax.ShapeDtypeStruct((DEPTH, 8, n), F32),
        compiler_params=_cparams("arbitrary", "arbitrary"),
        name="ada",
    )(cvec, ada_w, ada_b.reshape(DEPTH, 1, n))


IN_COLS = 3200
SWA_QK_W = (SWA_HEADS + SWA_KV_HEADS) * HEAD_DIM


def _mix_in_kernel(x_ref, shift_ref, scale_ref, g_ref, w_ref, wq_ref, wkv_ref, qn_ref, kvn_ref,
                   cs_ref, ss_ref, cm_ref, sm_ref,
                   u_ref, qk_ref, vs_ref, q_ref, k_ref, v_ref, h_ref, *, rope):
    for rs in _sub_tiles(x_ref.shape[0]):
        h = _rms(x_ref[rs, :], g_ref[...]) * (1.0 + scale_ref[0]) + shift_ref[0]
        h_ref[rs, :] = h.astype(BF16)

        def proj(lo, width):
            return _dot(h_ref[rs, :], w_ref[:, lo:lo + width])

        u_ref[rs, :] = proj(OFF_U, S5_WIDTH)

        swa_scale = HEAD_DIM ** -0.5
        per = 4
        for c in range(SWA_QK_W // (per * HEAD_DIM)):
            zc = proj(OFF_QS + c * per * HEAD_DIM, per * HEAD_DIM)
            for hh in range(per):
                head = c * per + hh
                xh = zc[:, hh * HEAD_DIM:(hh + 1) * HEAD_DIM]
                if rope:
                    xh = _rope(xh, cs_ref[rs, :], ss_ref[rs, :], HEAD_DIM // 4)
                if head < SWA_HEADS:
                    xh = xh * swa_scale
                qk_ref[rs, head * HEAD_DIM:(head + 1) * HEAD_DIM] = xh.astype(BF16)

        vs_ref[rs, :] = proj(OFF_VS, SWA_KV_HEADS * HEAD_DIM).astype(BF16)

        mla_scale = (MLA_NOPE + MLA_ROPE) ** -0.5
        cq = proj(OFF_CQ, MLA_Q_RANK)
        q = _dot(_rms(cq, qn_ref[...]).astype(BF16), wq_ref[...])
        for hd in range(MLA_HEADS):
            lo = hd * MLA_QK_PAD
            q_ref[rs, lo:lo + MLA_NOPE] = (q[:, lo:lo + MLA_NOPE] * mla_scale).astype(BF16)
            r = q[:, lo + MLA_NOPE:lo + MLA_QK_PAD]
            if rope:
                r = _rope(r, cm_ref[rs, :], sm_ref[rs, :], MLA_ROPE // 4)
            q_ref[rs, lo + MLA_NOPE:lo + MLA_QK_PAD] = (r * mla_scale).astype(BF16)

        ckv = proj(OFF_CKV, MLA_KV_RANK)
        kv = _dot(_rms(ckv, kvn_ref[...]).astype(BF16), wkv_ref[...])
        kr = proj(OFF_KR, 128)
        if rope:
            kr = _rope(kr, cm_ref[rs, :], sm_ref[rs, :], MLA_ROPE // 4)
        kr = kr.astype(BF16)
        for hd in range(MLA_HEADS):
            lo = hd * MLA_QK_PAD
            k_ref[rs, lo:lo + MLA_NOPE] = kv[:, hd * MLA_NOPE:(hd + 1) * MLA_NOPE].astype(BF16)
            k_ref[rs, lo + MLA_NOPE:lo + MLA_QK_PAD] = kr
            vlo = hd * MLA_V_PAD
            vh = kv[:, MLA_HEADS * MLA_NOPE + hd * MLA_V:MLA_HEADS * MLA_NOPE + (hd + 1) * MLA_V].astype(BF16)
            v_ref[rs, vlo:vlo + MLA_V] = vh
            v_ref[rs, vlo + MLA_V:vlo + MLA_V_PAD] = jnp.ones_like(vh)


def _mix_in(x2d, shift, scale, g, w_in, w_uq, w_ukv, q_norm, kv_norm, tabs, layer, tm, mod_row, rope, seq):
    rows = x2d.shape[0]
    once = pl.Buffered(1)
    tab = pl.BlockSpec((tm, 128), lambda i: (i % (seq // tm), 0))

    def res(shape):
        return pl.BlockSpec((None,) + shape, lambda i: (layer, 0, 0), pipeline_mode=once)

    def out(width):
        return pl.BlockSpec((tm, width), lambda i: (i, 0))

    widths = (S5_WIDTH, SWA_QK_W, SWA_KV_HEADS * HEAD_DIM,
              MLA_HEADS * MLA_QK_PAD, MLA_HEADS * MLA_QK_PAD, MLA_HEADS * MLA_V_PAD)
    dtypes = (F32, BF16, BF16, BF16, BF16, BF16)
    return pl.pallas_call(
        functools.partial(_mix_in_kernel, rope=rope),
        grid=(rows // tm,),
        in_specs=[pl.BlockSpec((tm, D_MODEL), lambda i: (i, 0)),
                  pl.BlockSpec((1, 1, D_MODEL), lambda i: (mod_row(i), 0, 0)),
                  pl.BlockSpec((1, 1, D_MODEL), lambda i: (mod_row(i), 0, 0)),
                  pl.BlockSpec((1, D_MODEL), lambda i: (0, 0)),
                  res((D_MODEL, IN_COLS)),
                  res((MLA_Q_RANK, MLA_HEADS * MLA_QK_PAD)),
                  res((MLA_KV_RANK, MLA_HEADS * (MLA_NOPE + MLA_V))),
                  pl.BlockSpec((1, MLA_Q_RANK), lambda i: (0, 0)),
                  pl.BlockSpec((1, MLA_KV_RANK), lambda i: (0, 0)),
                  tab, tab, tab, tab],
        out_specs=[out(w) for w in widths],
        out_shape=[jax.ShapeDtypeStruct((rows, w), dt) for w, dt in zip(widths, dtypes)],
        scratch_shapes=[pltpu.VMEM((tm, D_MODEL), BF16)],
        compiler_params=_cparams("arbitrary"),
        name="mix_in",
    )(x2d, shift, scale, g, w_in, w_uq, w_ukv, q_norm.reshape(1, -1), kv_norm.reshape(1, -1), *tabs)


S5_GH = S5_GROUPS // 2


def _zoh(are, aim, ldt):
    dt = jnp.exp(ldt)
    mag = jnp.exp(are * dt)
    lr = mag * jnp.cos(aim * dt)
    li = mag * jnp.sin(aim * dt)
    den = are * are + aim * aim
    nr = lr - 1.0
    return lr, li, (nr * are + li * aim) / den, (li * are - nr * aim) / den


def _s5_disc_kernel(are_ref, aim_ref, ldt_ref, arec_ref, aimc_ref, ldtc_ref, br_ref, bi_ref, cr_ref, ci_ref,
                    lam_ref, wbr_ref, wbi_ref, wcr_ref, wci_ref):
    lr, li, _, _ = _zoh(are_ref[0], aim_ref[0], ldt_ref[0])
    lam_ref[0, 0:1, :] = lr
    lam_ref[0, 1:2, :] = li
    gh, n, p = S5_GH, S5_STATE, S5_GROUP
    kin, kst = gh * p, gh * n

    def iota(shape, axis):
        return lax.broadcasted_iota(jnp.int32, shape, axis)

    tile_n = jnp.where(iota((n, kst), 0) == (iota((n, kst), 1) & (n - 1)), 1.0, 0.0).astype(BF16)
    tile_p = jnp.where(iota((p, kin), 0) == (iota((p, kin), 1) & (p - 1)), 1.0, 0.0).astype(BF16)
    lp, ln = p.bit_length() - 1, n.bit_length() - 1
    diag_b = (iota((kin, kst), 0) >> lp) == (iota((kin, kst), 1) >> ln)
    diag_c = (iota((kst, kin), 0) >> ln) == (iota((kst, kin), 1) >> lp)
    for h in range(2):
        _, _, fr, fi = _zoh(arec_ref[0, h], aimc_ref[0, h], ldtc_ref[0, h])
        fr = jnp.broadcast_to(fr[:, None, :], (gh, p, n)).reshape(kin, n)
        fi = jnp.broadcast_to(fi[:, None, :], (gh, p, n)).reshape(kin, n)
        bbr = (fr * br_ref[0, h] - fi * bi_ref[0, h]).astype(BF16)
        bbi = (fr * bi_ref[0, h] + fi * br_ref[0, h]).astype(BF16)
        wbr_ref[0, h] = jnp.where(diag_b, _dot(bbr, tile_n), 0.0).astype(BF16)
        wbi_ref[0, h] = jnp.where(diag_b, _dot(bbi, tile_n), 0.0).astype(BF16)
        wcr_ref[0, h] = jnp.where(diag_c, _dot(cr_ref[0, h].astype(BF16), tile_p), 0.0).astype(BF16)
        wci_ref[0, h] = jnp.where(diag_c, _dot(ci_ref[0, h].astype(BF16), tile_p), 0.0).astype(BF16)


def _s5_disc(a_re, a_im, log_dt, b_re, b_im, c_re, c_im):
    nd = DEPTH * 2
    gh, n, p = S5_GH, S5_STATE, S5_GROUP
    kin, kst = gh * p, gh * n
    are = a_re.reshape(nd, 1, S5_LANES)
    aim = a_im.reshape(nd, 1, S5_LANES)
    ldt_gn = jnp.broadcast_to(log_dt[..., None], a_re.shape)
    ldt = ldt_gn.reshape(nd, 1, S5_LANES)
    compact = lambda a: a.reshape(nd, 2, gh, n)
    bt = lambda b: b.reshape(nd, 2, gh, n, p).transpose(0, 1, 2, 4, 3).reshape(nd, 2, kin, n)
    ct = lambda c: c.reshape(nd, 2, gh, p, n).transpose(0, 1, 2, 4, 3).reshape(nd, 2, kst, p)
    vec = pl.BlockSpec((1, 1, S5_LANES), lambda d: (d, 0, 0))
    cvec = pl.BlockSpec((1, 2, gh, n), lambda d: (d, 0, 0, 0))
    bin_spec = pl.BlockSpec((1, 2, kin, n), lambda d: (d, 0, 0, 0))
    cin_spec = pl.BlockSpec((1, 2, kst, p), lambda d: (d, 0, 0, 0))
    bspec = pl.BlockSpec((1, 2, kin, kst), lambda d: (d, 0, 0, 0))
    cspec = pl.BlockSpec((1, 2, kst, kin), lambda d: (d, 0, 0, 0))
    return pl.pallas_call(
        _s5_disc_kernel,
        grid=(nd,),
        in_specs=[vec, vec, vec, cvec, cvec, cvec, bin_spec, bin_spec, cin_spec, cin_spec],
        out_specs=[pl.BlockSpec((1, 2, S5_LANES), lambda d: (d, 0, 0)), bspec, bspec, cspec, cspec],
        out_shape=[jax.ShapeDtypeStruct((nd, 2, S5_LANES), F32),
                   jax.ShapeDtypeStruct((nd, 2, kin, kst), BF16),
                   jax.ShapeDtypeStruct((nd, 2, kin, kst), BF16),
                   jax.ShapeDtypeStruct((nd, 2, kst, kin), BF16),
                   jax.ShapeDtypeStruct((nd, 2, kst, kin), BF16)],
        compiler_params=_cparams("arbitrary"),
        name="s5_disc",
    )(are, aim, ldt, compact(a_re), compact(a_im), compact(ldt_gn),
      bt(b_re), bt(b_im), ct(c_re), ct(c_im))


S5_TC = 128
S5_CTX_CHUNKS = CTX_LEN // S5_TC
S5_LAT_CHUNKS = SEQ // S5_TC


def _s5_scan_kernel(ucf_ref, ulf_ref, ucb_ref, ulb_ref, lam_ref, wbr_ref, wbi_ref, wcr_ref, wci_ref,
                    yf_ref, yb_ref, fr_ref, fi_ref, gr_ref, gi_ref, sr_ref, si_ref, ut_ref, yt_ref):
    c = pl.program_id(0)
    tc = S5_TC
    kin, kst = S5_WIDTH // 2, S5_LANES // 2
    nlb = kst // 128

    @pl.when(c == 0)
    def _():
        sr_ref[...] = jnp.zeros_like(sr_ref)
        si_ref[...] = jnp.zeros_like(si_ref)

    in_ctx = c < S5_CTX_CHUNKS
    state = [(fr_ref, fi_ref), (gr_ref, gi_ref)]
    nub = S5_WIDTH // 128

    u_dir = []
    for d, (uc_ref, ul_ref) in enumerate(((ucf_ref, ulf_ref), (ucb_ref, ulb_ref))):
        u = jnp.where(in_ctx, uc_ref[...], ul_ref[...])
        for j in range(nub):
            for b in range(BATCH):
                ut_ref[d, j, pl.ds(b, tc, stride=BATCH), :] = u[b, :, j * 128:(j + 1) * 128]
        u_dir.append(jnp.concatenate([ut_ref[d, j] for j in range(nub)], axis=1).astype(BF16))

    def project(d, h, part):
        w_ref = (wbr_ref, wbi_ref)[part]
        dst = state[d][part]
        p = _dot(u_dir[d][:, h * kin:(h + 1) * kin], w_ref[d, h])
        for j in range(nlb):
            dst[h * nlb + j] = p[:, j * 128:(j + 1) * 128]

    def scan_steps(h, k0, k1, carry):
        ls = slice(h * nlb, (h + 1) * nlb)
        fwd_rows = lax.broadcasted_iota(jnp.int32, (nlb, 8, 128), 1) < BATCH
        la_r = jnp.where(fwd_rows, lam_ref[0, 0, ls], lam_ref[1, 0, ls])
        la_i = jnp.where(fwd_rows, lam_ref[0, 1, ls], lam_ref[1, 1, ls])
        lb_r = jnp.where(fwd_rows, lam_ref[1, 0, ls], lam_ref[0, 0, ls])
        lb_i = jnp.where(fwd_rows, lam_ref[1, 1, ls], lam_ref[0, 1, ls])
        s_r, s_i = carry
        for k in range(k0, k1):
            rf = slice(k * 8, (k + 1) * 8)
            kb = tc // 2 - 1 - k
            rb = slice(kb * 8, (kb + 1) * 8)
            f_r, f_i = fr_ref[ls, rf, :], fi_ref[ls, rf, :]
            g_r, g_i = gr_ref[ls, rb, :], gi_ref[ls, rb, :]
            a_r = la_r * s_r - la_i * s_i + jnp.where(fwd_rows, f_r, g_r)
            a_i = la_r * s_i + la_i * s_r + jnp.where(fwd_rows, f_i, g_i)
            t_r = pltpu.roll(a_r, BATCH, 1)
            t_i = pltpu.roll(a_i, BATCH, 1)
            b_r = lb_r * t_r - lb_i * t_i + jnp.where(fwd_rows, g_r, f_r)
            b_i = lb_r * t_i + lb_i * t_r + jnp.where(fwd_rows, g_i, f_i)
            fr_ref[ls, rf, :] = jnp.where(fwd_rows, a_r, b_r)
            fi_ref[ls, rf, :] = jnp.where(fwd_rows, a_i, b_i)
            gr_ref[ls, rb, :] = jnp.where(fwd_rows, b_r, a_r)
            gi_ref[ls, rb, :] = jnp.where(fwd_rows, b_i, a_i)
            s_r, s_i = pltpu.roll(b_r, BATCH, 1), pltpu.roll(b_i, BATCH, 1)
        return s_r, s_i

    def half_rows(src, h):
        return jnp.concatenate([src[h * nlb + j] for j in range(nlb)], axis=1).astype(BF16)

    def readout(d, h):
        y = _dot(half_rows(state[d][0], h), wcr_ref[d, h]) - _dot(half_rows(state[d][1], h), wci_ref[d, h])
        for j in range(kin // 128):
            yt_ref[d, h * (kin // 128) + j] = y[:, j * 128:(j + 1) * 128]

    def emit(d):
        y_ref = (yf_ref, yb_ref)[d]
        for b in range(BATCH):
            y_ref[b] = jnp.concatenate([yt_ref[d, j, pl.ds(b, tc, stride=BATCH), :] for j in range(nub)], axis=1)

    quarters = 4
    per = tc // 2 // quarters
    pieces = [(d, part) for d in range(2) for part in range(2)]
    for d, part in pieces:
        project(d, 0, part)
    carry = (sr_ref[0:nlb], si_ref[0:nlb])
    for q, (d, part) in enumerate(pieces):
        carry = scan_steps(0, q * per, (q + 1) * per, carry)
        project(d, 1, part)
    sr_ref[0:nlb], si_ref[0:nlb] = carry
    carry = (sr_ref[nlb:2 * nlb], si_ref[nlb:2 * nlb])
    for q in range(quarters):
        carry = scan_steps(1, q * per, (q + 1) * per, carry)
        if q % 2 == 1:
            readout(q // 2, 0)
    sr_ref[nlb:2 * nlb], si_ref[nlb:2 * nlb] = carry
    for d in range(2):
        readout(d, 1)
        emit(d)


def _s5_scan(u_c, u_l, lam, wbr, wbi, wcr, wci, layer, skip_chunks):
    tc = S5_TC
    nchunk = S5_CTX_CHUNKS + S5_LAT_CHUNKS
    kin, kst = S5_WIDTH // 2, S5_LANES // 2
    zc3 = u_c.reshape(BATCH, CTX_LEN, S5_WIDTH)
    zl3 = u_l.reshape(BATCH, SEQ, S5_WIDTH)

    def bwd_chunk(c):
        return jnp.where(c < S5_CTX_CHUNKS, S5_CTX_CHUNKS - 1 - c, nchunk + S5_CTX_CHUNKS - 1 - c)

    def ctx_spec(chunk_of):
        return pl.BlockSpec((BATCH, tc, S5_WIDTH),
                            lambda c: (0, jnp.clip(chunk_of(c), 0, S5_CTX_CHUNKS - 1), 0))

    def lat_spec(chunk_of):
        return pl.BlockSpec((BATCH, tc, S5_WIDTH),
                            lambda c: (0, jnp.clip(chunk_of(c) - S5_CTX_CHUNKS, 0, S5_LAT_CHUNKS - 1), 0))

    def out_block(chunk):
        return jnp.where(chunk >= S5_CTX_CHUNKS, chunk - S5_CTX_CHUNKS, S5_LAT_CHUNKS + chunk)

    def out_spec(chunk_of):
        return pl.BlockSpec((BATCH, tc, S5_WIDTH),
                            lambda c: (0, out_block(chunk_of(jnp.maximum(c, skip_chunks))), 0))

    once = pl.Buffered(1)
    bspec = pl.BlockSpec((2, 2, kin, kst), lambda c: (layer, 0, 0, 0), pipeline_mode=once)
    cspec = pl.BlockSpec((2, 2, kst, kin), lambda c: (layer, 0, 0, 0), pipeline_mode=once)
    t_out = (nchunk - skip_chunks) * tc
    nblk = S5_LANES // 128
    big = pltpu.VMEM((nblk, tc * BATCH, 128), F32)
    state = pltpu.VMEM((nblk, 8, 128), F32)
    narrow = pltpu.VMEM((2, S5_WIDTH // 128, tc * BATCH, 128), F32)
    lam = lam.reshape(DEPTH * 2, 2, nblk, 1, 128)
    fwd_chunk = lambda c: c
    return pl.pallas_call(
        _s5_scan_kernel,
        grid=(nchunk,),
        in_specs=[ctx_spec(fwd_chunk), lat_spec(fwd_chunk), ctx_spec(bwd_chunk), lat_spec(bwd_chunk),
                  pl.BlockSpec((2, 2, nblk, 1, 128), lambda c: (layer, 0, 0, 0, 0)),
                  bspec, bspec, cspec, cspec],
        out_specs=[out_spec(fwd_chunk), out_spec(bwd_chunk)],
        out_shape=[jax.ShapeDtypeStruct((BATCH, t_out, S5_WIDTH), F32)] * 2,
        scratch_shapes=[big, big, big, big, state, state, narrow, narrow],
        compiler_params=_cparams("arbitrary"),
        name="s5_scan",
    )(zc3, zl3, zc3, zl3, lam, wbr, wbi, wcr, wci)


def _sink_rows(sink_ref, g, rows):
    r = lax.broadcasted_iota(jnp.int32, (rows, 1), 0) // (rows // SWA_REP)
    s0, s1, s2 = sink_ref[SWA_REP * g], sink_ref[SWA_REP * g + 1], sink_ref[SWA_REP * g + 2]
    return jnp.where(r == 0, s0, jnp.where(r == 1, s1, s2))


def _stack_heads(x, n):
    return jnp.concatenate([x[:, h * HEAD_DIM:(h + 1) * HEAD_DIM] for h in range(n)], axis=0)


def _unstack_heads(x, n):
    rows = x.shape[0] // n
    return jnp.concatenate([x[h * rows:(h + 1) * rows] for h in range(n)], axis=1)


SWA_QB = 4


def _with_ones(v):
    return jnp.concatenate([v, jnp.ones_like(v)], axis=1)


def _swa_latent_kernel(sink_ref, q_ref, kp_ref, km_ref, kn_ref, vp_ref, vm_ref, vn_ref,
                       kx_ref, vx_ref, o_ref):
    n = pl.program_id(1)
    rows = SWA_REP * BLOCK
    qi = lax.broadcasted_iota(jnp.int32, (rows, 3 * BLOCK), 0) % BLOCK
    kj = lax.broadcasted_iota(jnp.int32, (rows, 3 * BLOCK), 1)
    in_window = (kj >= qi) & (kj <= qi + 2 * BLOCK)
    kv = []
    for g in range(SWA_KV_HEADS):
        gs = slice(g * HEAD_DIM, (g + 1) * HEAD_DIM)
        kspan = jnp.concatenate([kp_ref[:, gs], km_ref[:, gs], kn_ref[:, gs]], axis=0)
        vspan = _with_ones(jnp.concatenate([vp_ref[:, gs], vm_ref[:, gs], vn_ref[:, gs]], axis=0))
        kv.append((kspan, vspan, kx_ref[:, gs], _with_ones(vx_ref[:, gs]), _sink_rows(sink_ref, g, rows)))

    def scores(g, j):
        kspan, _, kx, _, _ = kv[g]
        qs = slice(g * SWA_REP * HEAD_DIM, (g + 1) * SWA_REP * HEAD_DIM)
        q = _stack_heads(q_ref[j * BLOCK:(j + 1) * BLOCK, qs], SWA_REP)
        return _dot_nt(q, kx), _dot_nt(q, kspan[j * BLOCK:(j + 3) * BLOCK])

    work = [(g, j) for g in range(SWA_KV_HEADS) for j in range(SWA_QB)]
    s_next = scores(*work[0])
    for idx, (g, j) in enumerate(work):
        s_ctx, s_band = s_next
        if idx + 1 < len(work):
            s_next = scores(*work[idx + 1])
        _, vspan, _, vx, sk = kv[g]
        qs = slice(g * SWA_REP * HEAD_DIM, (g + 1) * SWA_REP * HEAD_DIM)
        kpos = (n * SWA_QB + j - 1) * BLOCK + kj
        valid = in_window & (kpos >= 0) & (kpos < SEQ)
        s_band = jnp.where(valid, s_band, NEG_INF)
        m = jnp.maximum(jnp.maximum(jnp.max(s_ctx, axis=-1, keepdims=True),
                                    jnp.max(s_band, axis=-1, keepdims=True)), sk)
        p_ctx = jnp.exp(s_ctx - m).astype(BF16)
        p_band = jnp.exp(s_band - m).astype(BF16)
        o = _dot(p_ctx, vx) + _dot(p_band, vspan[j * BLOCK:(j + 3) * BLOCK])
        den = o[:, HEAD_DIM:] + jnp.exp(sk - m)
        o_ref[j * BLOCK:(j + 1) * BLOCK, qs] = _unstack_heads(
            o[:, :HEAD_DIM] / den, SWA_REP).astype(o_ref.dtype)


def _swa_latent(qk, vs, qk_c, vs_c, sink):
    nb = SEQ // BLOCK
    ng = nb // SWA_QB
    span = SWA_QB * BLOCK
    kvw = SWA_KV_HEADS * HEAD_DIM
    kcol = SWA_HEADS * HEAD_DIM // kvw
    vcol = 0
    kxcol = kcol

    def edge(col, blk_of):
        return pl.BlockSpec((BLOCK, kvw), lambda b, n, s: (b * nb + jnp.clip(blk_of(n), 0, nb - 1), col))

    def main(col):
        return pl.BlockSpec((span, kvw), lambda b, n, s: (b * ng + n, col))

    prev_blk = lambda n: n * SWA_QB - 1
    next_blk = lambda n: (n + 1) * SWA_QB
    grid_spec = pltpu.PrefetchScalarGridSpec(
        num_scalar_prefetch=1,
        grid=(BATCH, ng),
        in_specs=[pl.BlockSpec((span, SWA_HEADS * HEAD_DIM), lambda b, n, s: (b * ng + n, 0)),
                  edge(kcol, prev_blk), main(kcol), edge(kcol, next_blk),
                  edge(vcol, prev_blk), main(vcol), edge(vcol, next_blk),
                  pl.BlockSpec((CTX_LEN, kvw), lambda b, n, s: (b, kxcol)),
                  pl.BlockSpec((CTX_LEN, kvw), lambda b, n, s: (b, vcol))],
        out_specs=pl.BlockSpec((span, SWA_HEADS * HEAD_DIM), lambda b, n, s: (b * ng + n, 0)),
    )
    return pl.pallas_call(
        _swa_latent_kernel,
        grid_spec=grid_spec,
        out_shape=jax.ShapeDtypeStruct((BATCH * SEQ, SWA_HEADS * HEAD_DIM), BF16),
        compiler_params=_cparams("arbitrary", "arbitrary"),
        name="swa_latent",
    )(sink, qk, qk, qk, qk, vs, vs, vs, qk_c, vs_c)


def _swa_context_kernel(sink_ref, q_ref, k_ref, v_ref, o_ref):
    g = pl.program_id(1)
    q = _stack_heads(q_ref[...], SWA_REP)
    s = _dot_nt(q, k_ref[...])
    sk = _sink_rows(sink_ref, g, SWA_REP * CTX_LEN)
    m = jnp.maximum(jnp.max(s, axis=-1, keepdims=True), sk)
    p = jnp.exp(s - m)
    den = jnp.sum(p, axis=-1, keepdims=True) + jnp.exp(sk - m)
    o = _dot(p.astype(BF16), v_ref[...]) / den
    o_ref[...] = _unstack_heads(o, SWA_REP).astype(o_ref.dtype)


def _swa_context(qk_c, vs_c, sink):
    qw = SWA_REP * HEAD_DIM
    grid_spec = pltpu.PrefetchScalarGridSpec(
        num_scalar_prefetch=1,
        grid=(BATCH, SWA_KV_HEADS),
        in_specs=[pl.BlockSpec((CTX_LEN, qw), lambda b, g, s: (b, g)),
                  pl.BlockSpec((CTX_LEN, HEAD_DIM), lambda b, g, s: (b, SWA_HEADS + g)),
                  pl.BlockSpec((CTX_LEN, HEAD_DIM), lambda b, g, s: (b, g))],
        out_specs=pl.BlockSpec((CTX_LEN, qw), lambda b, g, s: (b, g)),
    )
    return pl.pallas_call(
        _swa_context_kernel,
        grid_spec=grid_spec,
        out_shape=jax.ShapeDtypeStruct((BATCH * CTX_LEN, SWA_HEADS * HEAD_DIM), BF16),
        compiler_params=_cparams("arbitrary", "arbitrary"),
        name="swa_context",
    )(sink, qk_c, qk_c, vs_c)


def _mla_attn_kernel(*refs, nseg):
    q_ref = refs[0]
    k_refs = refs[1:1 + nseg]
    v_refs = refs[1 + nseg:1 + 2 * nseg]
    o_ref = refs[1 + 2 * nseg]
    def scores(h):
        q = q_ref[:, h * MLA_QK_PAD:(h + 1) * MLA_QK_PAD]
        return [_dot_nt(q, k[:, h * MLA_QK_PAD:(h + 1) * MLA_QK_PAD]) for k in k_refs]

    s_next = scores(0)
    for h in range(MLA_HEADS):
        s = s_next
        if h + 1 < MLA_HEADS:
            s_next = scores(h + 1)
        m = functools.reduce(jnp.maximum, [jnp.max(x, axis=-1, keepdims=True) for x in s])
        p = [jnp.exp(x - m).astype(BF16) for x in s]
        o = sum(_dot(x, v[:, h * MLA_V_PAD:(h + 1) * MLA_V_PAD]) for x, v in zip(p, v_refs))
        o_ref[:, h * MLA_V:(h + 1) * MLA_V] = (o[:, :MLA_V] / o[:, MLA_V:]).astype(o_ref.dtype)


def _mla_attn(q, ks, vs, n_q, tq):
    nseg = len(ks)
    nq = n_q // tq
    lens = [k.shape[0] // BATCH for k in ks]
    kw, vw = MLA_HEADS * MLA_QK_PAD, MLA_HEADS * MLA_V
    in_specs = [pl.BlockSpec((tq, kw), lambda b, i: (b * nq + i, 0))]
    in_specs += [pl.BlockSpec((n, kw), lambda b, i: (b, 0)) for n in lens]
    in_specs += [pl.BlockSpec((n, MLA_HEADS * MLA_V_PAD), lambda b, i: (b, 0)) for n in lens]
    return pl.pallas_call(
        functools.partial(_mla_attn_kernel, nseg=nseg),
        grid=(BATCH, nq),
        in_specs=in_specs,
        out_specs=pl.BlockSpec((tq, vw), lambda b, i: (b * nq + i, 0)),
        out_shape=jax.ShapeDtypeStruct((BATCH * n_q, vw), BF16),
        compiler_params=_cparams("arbitrary", "arbitrary"),
        name="mla_attn",
    )(q, *ks, *vs)


def _out_proj_kernel(u_ref, yf_ref, yb_ref, d_ref, gw_ref, gb_ref, a2_ref, a3_ref, w_ref, x_ref, g_ref, gate_ref,
                     fg_ref, fshift_ref, fscale_ref, o_ref, f_ref):
    nc = 512
    k1 = S5_WIDTH
    k2 = k1 + a2_ref.shape[1]
    for rs in _sub_tiles(o_ref.shape[0]):
        y = d_ref[...] * u_ref[0, rs, :] + yf_ref[0, rs, :] + yb_ref[0, rs, :]
        gl = jax.nn.gelu(y)
        a1 = (gl * jax.nn.sigmoid(_dot(gl.astype(BF16), gw_ref[...]) + gb_ref[...])).astype(BF16)
        for n in range(D_MODEL // nc):
            cs = slice(n * nc, (n + 1) * nc)
            o_ref[rs, cs] = (_dot(a2_ref[rs, :], w_ref[k1:k2, cs]) + _dot(a3_ref[rs, :], w_ref[k2:D_MODEL, cs])
                             + _dot(a1, w_ref[0:k1, cs]))
        xn = x_ref[rs, :] + gate_ref[0] * _rms(o_ref[rs, :], g_ref[...])
        o_ref[rs, :] = xn
        f_ref[rs, :] = (_rms(xn, fg_ref[...]) * (1.0 + fscale_ref[0]) + fshift_ref[0]).astype(BF16)


def _out_proj(u3, yf, yb, y_off, d, glu_w, glu_b, a2, a3, w, x2d, g, gate, fg, fshift, fscale, layer, tm, mod_row):
    rows = a2.shape[0]
    per_b = u3.shape[1] // tm
    once = pl.Buffered(1)
    uspec = pl.BlockSpec((1, tm, S5_WIDTH), lambda i: (i // per_b, i % per_b, 0))
    yspec = pl.BlockSpec((1, tm, S5_WIDTH), lambda i: (i // per_b, i % per_b + y_off // tm, 0))
    vec5 = pl.BlockSpec((1, S5_WIDTH), lambda i: (0, 0))
    vec = pl.BlockSpec((1, D_MODEL), lambda i: (0, 0))
    mod = pl.BlockSpec((1, 1, D_MODEL), lambda i: (mod_row(i), 0, 0))
    row = pl.BlockSpec((tm, D_MODEL), lambda i: (i, 0))
    return pl.pallas_call(
        _out_proj_kernel,
        grid=(rows // tm,),
        in_specs=[uspec, yspec, yspec, vec5,
                  pl.BlockSpec((None, S5_WIDTH, S5_WIDTH), lambda i: (layer, 0, 0), pipeline_mode=once), vec5,
                  pl.BlockSpec((tm, a2.shape[1]), lambda i: (i, 0)),
                  pl.BlockSpec((tm, a3.shape[1]), lambda i: (i, 0)),
                  pl.BlockSpec((None, D_MODEL, D_MODEL), lambda i: (layer, 0, 0), pipeline_mode=once),
                  row, vec, mod, vec, mod, mod],
        out_specs=[row, row],
        out_shape=[jax.ShapeDtypeStruct((rows, D_MODEL), F32), jax.ShapeDtypeStruct((rows, D_MODEL), BF16)],
        compiler_params=_cparams("arbitrary"),
        name="out_proj",
    )(u3, yf, yb, d.reshape(1, -1), glu_w, glu_b.reshape(1, -1), a2, a3, w, x2d, g.reshape(1, -1), gate,
      fg.reshape(1, -1), fshift, fscale)


def _ffn_kernel(x_ref, f_ref, w1_ref, w2_ref, gpost_ref, gate_ref, o_ref, h_ref):
    j = pl.program_id(1)

    @pl.when(j == 0)
    def _():
        o_ref[...] = jnp.zeros_like(o_ref)

    h = _dot(f_ref[...], w1_ref[...].astype(BF16))
    h_ref[...] = jnp.square(jnp.maximum(h, 0.0)).astype(BF16)
    nc = 512
    for n in range(D_MODEL // nc):
        cs = slice(n * nc, (n + 1) * nc)
        o_ref[:, cs] += _dot(h_ref[...], w2_ref[:, cs].astype(BF16))

    @pl.when(j == pl.num_programs(1) - 1)
    def _():
        o_ref[...] = x_ref[...] + gate_ref[0] * _rms(o_ref[...], gpost_ref[...])


def _ffn(x2d, f2d, w1, w2, gpost, gate, layer, tm, mod_row):
    rows = x2d.shape[0]
    tf = 512
    mod = pl.BlockSpec((1, 1, D_MODEL), lambda i, j: (mod_row(i), 0, 0))
    vec = pl.BlockSpec((1, D_MODEL), lambda i, j: (0, 0))
    once = pl.Buffered(1)
    return pl.pallas_call(
        _ffn_kernel,
        grid=(rows // tm, D_FF // tf),
        in_specs=[pl.BlockSpec((tm, D_MODEL), lambda i, j: (i, 0)),
                  pl.BlockSpec((tm, D_MODEL), lambda i, j: (i, 0), pipeline_mode=once),
                  pl.BlockSpec((None, D_MODEL, tf), lambda i, j: (layer, 0, j)),
                  pl.BlockSpec((None, tf, D_MODEL), lambda i, j: (layer, j, 0)),
                  vec, mod],
        out_specs=pl.BlockSpec((tm, D_MODEL), lambda i, j: (i, 0), pipeline_mode=once),
        out_shape=jax.ShapeDtypeStruct((rows, D_MODEL), F32),
        scratch_shapes=[pltpu.VMEM((tm, tf), BF16)],
        compiler_params=_cparams("arbitrary", "arbitrary"),
        name="ffn",
    )(x2d, f2d, w1, w2, gpost.reshape(1, -1), gate)


def _rope_tables():
    t = np.arange(SEQ)
    row = (t // GRID_W).astype(np.float32)[:, None]
    col = (t % GRID_W).astype(np.float32)[:, None]

    def tables(rot_dim):
        quarter = rot_dim // 4
        inv_freq = np.float32(ROPE_BASE) ** (-np.arange(quarter, dtype=np.float32) / np.float32(quarter))
        ar, ac = row * inv_freq, col * inv_freq
        cos = np.concatenate([np.cos(ar), np.cos(ar), np.cos(ac), np.cos(ac)], axis=1)
        sin = np.concatenate([-np.sin(ar), np.sin(ar), -np.sin(ac), np.sin(ac)], axis=1)
        pad = 128 - rot_dim
        if pad:
            cos = np.concatenate([cos, np.ones((SEQ, pad), np.float32)], axis=1)
            sin = np.concatenate([sin, np.zeros((SEQ, pad), np.float32)], axis=1)
        return jnp.asarray(cos, F32), jnp.asarray(sin, F32)

    return tables(HEAD_DIM), tables(MLA_ROPE)


def kernel(x, c, ctx, c_ctx, ada_w, ada_b, norm_mix_pre, norm_mix_post, norm_ffn_pre, norm_ffn_post, w_in, w_out, s5_a_re, s5_a_im, s5_log_dt, s5_b_re, s5_b_im, s5_c_re, s5_c_im, s5_d, s5_glu_w, s5_glu_b, swa_sink, mla_q_norm, mla_w_uq, mla_kv_norm, mla_w_ukv, ffn_w1, ffn_w2):
    (cos_swa, sin_swa), (cos_mla, sin_mla) = _rope_tables()
    tabs = (cos_swa, sin_swa, cos_mla, sin_mla)

    cvec = jnp.concatenate([c, c_ctx[None, :], jnp.zeros((3, D_MODEL), F32)], axis=0)
    mods = _ada(cvec, ada_w, ada_b)

    tm_p = 512
    tm_f = 1024

    def lat_row(tm):
        return lambda i: i // (SEQ // tm)

    ctx_row = lambda i: 4

    xl = x.reshape(BATCH * SEQ, D_MODEL)
    xc = ctx.reshape(BATCH * CTX_LEN, D_MODEL)

    w_in_pad = jnp.pad(w_in, ((0, 0), (0, 0), (0, IN_COLS - IN_WIDTH))).astype(BF16)
    w_out_b = w_out.astype(BF16)
    glu_w_b = s5_glu_w.astype(BF16)
    w_uq_pad = jnp.pad(mla_w_uq.reshape(DEPTH, MLA_Q_RANK, MLA_HEADS, MLA_NOPE + MLA_ROPE),
                       ((0, 0), (0, 0), (0, 0), (0, MLA_QK_PAD - MLA_NOPE - MLA_ROPE))
                       ).reshape(DEPTH, MLA_Q_RANK, MLA_HEADS * MLA_QK_PAD).astype(BF16)
    w_ukv4 = mla_w_ukv.reshape(DEPTH, MLA_KV_RANK, MLA_HEADS, MLA_NOPE + MLA_V)
    w_ukv_perm = jnp.concatenate([w_ukv4[..., :MLA_NOPE].reshape(DEPTH, MLA_KV_RANK, -1),
                                  w_ukv4[..., MLA_NOPE:].reshape(DEPTH, MLA_KV_RANK, -1)], axis=2).astype(BF16)
    s5_w = _s5_disc(s5_a_re, s5_a_im, s5_log_dt, s5_b_re, s5_b_im, s5_c_re, s5_c_im)

    for i in range(DEPTH):
        need_ctx = i < DEPTH - 1
        mod = [mods[i, :, k * D_MODEL:(k + 1) * D_MODEL].reshape(8, 1, D_MODEL) for k in range(6)]
        g_pre = norm_mix_pre[i].reshape(1, -1)

        u_l, qk_l, vs_l, q_l, k_l, v_l = _mix_in(
            xl, mod[0], mod[1], g_pre, w_in_pad, w_uq_pad, w_ukv_perm, mla_q_norm[i], mla_kv_norm[i],
            tabs, i, tm_p, lat_row(tm_p), True, SEQ)
        u_c, qk_c, vs_c, q_c, k_c, v_c = _mix_in(
            xc, mod[0], mod[1], g_pre, w_in_pad, w_uq_pad, w_ukv_perm, mla_q_norm[i], mla_kv_norm[i],
            tabs, i, tm_p, ctx_row, False, tm_p)

        yf, yb = _s5_scan(u_c, u_l, *s5_w, i, 0 if need_ctx else S5_CTX_CHUNKS)
        u3_l = u_l.reshape(BATCH, SEQ, S5_WIDTH)
        u3_c = u_c.reshape(BATCH, CTX_LEN, S5_WIDTH)

        swa_l = _swa_latent(qk_l, vs_l, qk_c, vs_c, swa_sink[i])

        mla_l = _mla_attn(q_l, [k_c, k_l], [v_c, v_l], SEQ, 512)

        xl, fl = _out_proj(u3_l, yf, yb, 0, s5_d[i], glu_w_b, s5_glu_b[i], swa_l, mla_l, w_out_b, xl,
                           norm_mix_post[i], mod[2], norm_ffn_pre[i], mod[3], mod[4], i, tm_p, lat_row(tm_p))
        xl = _ffn(xl, fl, ffn_w1, ffn_w2, norm_ffn_post[i], mod[5], i, tm_f, lat_row(tm_f))

        if need_ctx:
            swa_c = _swa_context(qk_c, vs_c, swa_sink[i])
            mla_c = _mla_attn(q_c, [k_c], [v_c], CTX_LEN, 256)
            xc, fc = _out_proj(u3_c, yf, yb, SEQ, s5_d[i], glu_w_b, s5_glu_b[i], swa_c, mla_c, w_out_b, xc,
                               norm_mix_post[i], mod[2], norm_ffn_pre[i], mod[3], mod[4], i, CTX_LEN, ctx_row)
            xc = _ffn(xc, fc, ffn_w1, ffn_w2, norm_ffn_post[i], mod[5], i, tm_f, ctx_row)

    return xl.reshape(BATCH, SEQ, D_MODEL)
```

```python
import functools

import jax
import jax.numpy as jnp
import numpy as np
from jax import lax
from jax.experimental import pallas as pl
from jax.experimental.pallas import tpu as pltpu

F32 = jnp.float32
BF16 = jnp.bfloat16

D_MODEL = 2048
BATCH = 4
SEQ = 2048
DEPTH = 2
GRID_W = 64
CTX_LEN = 256
EPS = 1e-6
ROPE_BASE = 10000.0
NEG_INF = -1e30
BLOCK = 128
HEAD_DIM = 128
S5_WIDTH = 512
S5_GROUP = 16
S5_GROUPS = 32
S5_STATE = 64
S5_LANES = S5_GROUPS * S5_STATE
SWA_HEADS = 6
SWA_KV_HEADS = 2
SWA_REP = SWA_HEADS // SWA_KV_HEADS
MLA_HEADS = 6
MLA_Q_RANK = 768
MLA_KV_RANK = 512
MLA_NOPE = 128
MLA_ROPE = 64
MLA_V = 128
MLA_QK_PAD = 256
MLA_V_PAD = 256
D_FF = 4 * D_MODEL
IN_WIDTHS = (512, 768, 256, 256, 768, 512, 64)
IN_WIDTH = sum(IN_WIDTHS)
OFF_U, OFF_QS, OFF_KS, OFF_VS, OFF_CQ, OFF_CKV, OFF_KR = 0, 512, 1280, 1536, 1792, 2560, 3072

LANE = 128
SUBLANE = 8
V7X_VMEM_BYTES = 64 * 1024 * 1024
VMEM_LIMIT = V7X_VMEM_BYTES * 7 // 8

TM_PROJ = 512
SUB_ROWS = 256
TM_FFN = 1024
TF_FFN = 512
TN_ADA = 1024
TQ_MLA = 512
TQ_MLA_CTX = CTX_LEN
S5_TC = 128
SWA_QB = 4


def _cparams(*sem):
    return pltpu.CompilerParams(dimension_semantics=sem, vmem_limit_bytes=VMEM_LIMIT)


def _dot(a, b):
    return jnp.dot(a, b, preferred_element_type=F32)


def _dot_nt(a, b):
    return lax.dot_general(a, b, (((1,), (1,)), ((), ())), preferred_element_type=F32)


def _rms(x, g):
    return x * lax.rsqrt(jnp.mean(x * x, axis=-1, keepdims=True) + EPS) * g


def _sub_tiles(rows):
    return [slice(r, r + SUB_ROWS) for r in range(0, rows, SUB_ROWS)]


def _rope(x, cos, sin, half):
    lane = lax.broadcasted_iota(jnp.int32, x.shape, 1)
    fwd = pltpu.roll(x, LANE - half, 1)
    bwd = pltpu.roll(x, half, 1)
    sw = jnp.where((lane % (2 * half)) < half, fwd, bwd)
    return x * cos + sw * sin


def _ada_kernel(c_ref, w_ref, b_ref, o_ref):
    c = c_ref[...]
    s = c * jax.nn.sigmoid(c)
    o_ref[0] = _dot(s.astype(BF16), w_ref[0].astype(BF16)) + b_ref[0]


def _ada(cvec, ada_w, ada_b):
    tn = TN_ADA
    n = 6 * D_MODEL
    return pl.pallas_call(
        _ada_kernel,
        grid=(DEPTH, n // tn),
        in_specs=[pl.BlockSpec((8, D_MODEL), lambda l, j: (0, 0)),
                  pl.BlockSpec((1, D_MODEL, tn), lambda l, j: (l, 0, j)),
                  pl.BlockSpec((1, 1, tn), lambda l, j: (l, 0, j))],
        out_specs=pl.BlockSpec((1, 8, tn), lambda l, j: (l, 0, j)),
        out_shape=jax.ShapeDtypeStruct((DEPTH, 8, n), F32),
        compiler_params=_cparams("arbitrary", "arbitrary"),
        name="ada",
    )(cvec, ada_w, ada_b.reshape(DEPTH, 1, n))


IN_COLS = 3200
SWA_QK_W = (SWA_HEADS + SWA_KV_HEADS) * HEAD_DIM


def _mix_in_kernel(x_ref, shift_ref, scale_ref, g_ref, w_ref, wq_ref, wkv_ref, qn_ref, kvn_ref,
                   cs_ref, ss_ref, cm_ref, sm_ref,
                   u_ref, qk_ref, vs_ref, q_ref, k_ref, v_ref, h_ref, *, rope):
    for rs in _sub_tiles(x_ref.shape[0]):
        h = _rms(x_ref[rs, :], g_ref[...]) * (1.0 + scale_ref[0]) + shift_ref[0]
        h_ref[rs, :] = h.astype(BF16)

        def proj(lo, width):
            return _dot(h_ref[rs, :], w_ref[:, lo:lo + width])

        u_ref[rs, :] = proj(OFF_U, S5_WIDTH)

        swa_scale = HEAD_DIM ** -0.5
        per = 4
        for c in range(SWA_QK_W // (per * HEAD_DIM)):
            zc = proj(OFF_QS + c * per * HEAD_DIM, per * HEAD_DIM)
            for hh in range(per):
                head = c * per + hh
                xh = zc[:, hh * HEAD_DIM:(hh + 1) * HEAD_DIM]
                if rope:
                    xh = _rope(xh, cs_ref[rs, :], ss_ref[rs, :], HEAD_DIM // 4)
                if head < SWA_HEADS:
                    xh = xh * swa_scale
                qk_ref[rs, head * HEAD_DIM:(head + 1) * HEAD_DIM] = xh.astype(BF16)

        vs_ref[rs, :] = proj(OFF_VS, SWA_KV_HEADS * HEAD_DIM).astype(BF16)

        mla_scale = (MLA_NOPE + MLA_ROPE) ** -0.5
        cq = proj(OFF_CQ, MLA_Q_RANK)
        q = _dot(_rms(cq, qn_ref[...]).astype(BF16), wq_ref[...])
        for hd in range(MLA_HEADS):
            lo = hd * MLA_QK_PAD
            q_ref[rs, lo:lo + MLA_NOPE] = (q[:, lo:lo + MLA_NOPE] * mla_scale).astype(BF16)
            r = q[:, lo + MLA_NOPE:lo + MLA_QK_PAD]
            if rope:
                r = _rope(r, cm_ref[rs, :], sm_ref[rs, :], MLA_ROPE // 4)
            q_ref[rs, lo + MLA_NOPE:lo + MLA_QK_PAD] = (r * mla_scale).astype(BF16)

        ckv = proj(OFF_CKV, MLA_KV_RANK)
        kv = _dot(_rms(ckv, kvn_ref[...]).astype(BF16), wkv_ref[...])
        kr = proj(OFF_KR, LANE)
        if rope:
            kr = _rope(kr, cm_ref[rs, :], sm_ref[rs, :], MLA_ROPE // 4)
        kr = kr.astype(BF16)
        for hd in range(MLA_HEADS):
            lo = hd * MLA_QK_PAD
            k_ref[rs, lo:lo + MLA_NOPE] = kv[:, hd * MLA_NOPE:(hd + 1) * MLA_NOPE].astype(BF16)
            k_ref[rs, lo + MLA_NOPE:lo + MLA_QK_PAD] = kr
            vlo = hd * MLA_V_PAD
            vh = kv[:, MLA_HEADS * MLA_NOPE + hd * MLA_V:MLA_HEADS * MLA_NOPE + (hd + 1) * MLA_V].astype(BF16)
            v_ref[rs, vlo:vlo + MLA_V] = vh
            v_ref[rs, vlo + MLA_V:vlo + MLA_V_PAD] = jnp.ones_like(vh)


def _mix_in(x2d, shift, scale, g, w_in, w_uq, w_ukv, q_norm, kv_norm, tabs, layer, tm, mod_row, rope, seq):
    rows = x2d.shape[0]
    once = pl.Buffered(1)
    tab = pl.BlockSpec((tm, LANE), lambda i: (i % (seq // tm), 0))

    def res(shape):
        return pl.BlockSpec((None,) + shape, lambda i: (layer, 0, 0), pipeline_mode=once)

    def out(width):
        return pl.BlockSpec((tm, width), lambda i: (i, 0))

    widths = (S5_WIDTH, SWA_QK_W, SWA_KV_HEADS * HEAD_DIM,
              MLA_HEADS * MLA_QK_PAD, MLA_HEADS * MLA_QK_PAD, MLA_HEADS * MLA_V_PAD)
    dtypes = (F32, BF16, BF16, BF16, BF16, BF16)
    return pl.pallas_call(
        functools.partial(_mix_in_kernel, rope=rope),
        grid=(rows // tm,),
        in_specs=[pl.BlockSpec((tm, D_MODEL), lambda i: (i, 0)),
                  pl.BlockSpec((1, 1, D_MODEL), lambda i: (mod_row(i), 0, 0)),
                  pl.BlockSpec((1, 1, D_MODEL), lambda i: (mod_row(i), 0, 0)),
                  pl.BlockSpec((1, D_MODEL), lambda i: (0, 0)),
                  res((D_MODEL, IN_COLS)),
                  res((MLA_Q_RANK, MLA_HEADS * MLA_QK_PAD)),
                  res((MLA_KV_RANK, MLA_HEADS * (MLA_NOPE + MLA_V))),
                  pl.BlockSpec((1, MLA_Q_RANK), lambda i: (0, 0)),
                  pl.BlockSpec((1, MLA_KV_RANK), lambda i: (0, 0)),
                  tab, tab, tab, tab],
        out_specs=[out(w) for w in widths],
        out_shape=[jax.ShapeDtypeStruct((rows, w), dt) for w, dt in zip(widths, dtypes)],
        scratch_shapes=[pltpu.VMEM((tm, D_MODEL), BF16)],
        compiler_params=_cparams("arbitrary"),
        name="mix_in",
    )(x2d, shift, scale, g, w_in, w_uq, w_ukv, q_norm.reshape(1, -1), kv_norm.reshape(1, -1), *tabs)


S5_GH = S5_GROUPS // 2


def _zoh(are, aim, ldt):
    dt = jnp.exp(ldt)
    mag = jnp.exp(are * dt)
    lr = mag * jnp.cos(aim * dt)
    li = mag * jnp.sin(aim * dt)
    den = are * are + aim * aim
    nr = lr - 1.0
    return lr, li, (nr * are + li * aim) / den, (li * are - nr * aim) / den


def _s5_disc_kernel(are_ref, aim_ref, ldt_ref, arec_ref, aimc_ref, ldtc_ref, br_ref, bi_ref, cr_ref, ci_ref,
                    lam_ref, wbr_ref, wbi_ref, wcr_ref, wci_ref):
    lr, li, _, _ = _zoh(are_ref[0], aim_ref[0], ldt_ref[0])
    lam_ref[0, 0:1, :] = lr
    lam_ref[0, 1:2, :] = li
    gh, n, p = S5_GH, S5_STATE, S5_GROUP
    kin, kst = gh * p, gh * n

    def iota(shape, axis):
        return lax.broadcasted_iota(jnp.int32, shape, axis)

    tile_n = jnp.where(iota((n, kst), 0) == (iota((n, kst), 1) & (n - 1)), 1.0, 0.0).astype(BF16)
    tile_p = jnp.where(iota((p, kin), 0) == (iota((p, kin), 1) & (p - 1)), 1.0, 0.0).astype(BF16)
    lp, ln = p.bit_length() - 1, n.bit_length() - 1
    diag_b = (iota((kin, kst), 0) >> lp) == (iota((kin, kst), 1) >> ln)
    diag_c = (iota((kst, kin), 0) >> ln) == (iota((kst, kin), 1) >> lp)
    for h in range(2):
        _, _, fr, fi = _zoh(arec_ref[0, h], aimc_ref[0, h], ldtc_ref[0, h])
        fr = jnp.broadcast_to(fr[:, None, :], (gh, p, n)).reshape(kin, n)
        fi = jnp.broadcast_to(fi[:, None, :], (gh, p, n)).reshape(kin, n)
        bbr = (fr * br_ref[0, h] - fi * bi_ref[0, h]).astype(BF16)
        bbi = (fr * bi_ref[0, h] + fi * br_ref[0, h]).astype(BF16)
        wbr_ref[0, h] = jnp.where(diag_b, _dot(bbr, tile_n), 0.0).astype(BF16)
        wbi_ref[0, h] = jnp.where(diag_b, _dot(bbi, tile_n), 0.0).astype(BF16)
        wcr_ref[0, h] = jnp.where(diag_c, _dot(cr_ref[0, h].astype(BF16), tile_p), 0.0).astype(BF16)
        wci_ref[0, h] = jnp.where(diag_c, _dot(ci_ref[0, h].astype(BF16), tile_p), 0.0).astype(BF16)


def _s5_disc(a_re, a_im, log_dt, b_re, b_im, c_re, c_im):
    nd = DEPTH * 2
    gh, n, p = S5_GH, S5_STATE, S5_GROUP
    kin, kst = gh * p, gh * n
    are = a_re.reshape(nd, 1, S5_LANES)
    aim = a_im.reshape(nd, 1, S5_LANES)
    ldt_gn = jnp.broadcast_to(log_dt[..., None], a_re.shape)
    ldt = ldt_gn.reshape(nd, 1, S5_LANES)
    compact = lambda a: a.reshape(nd, 2, gh, n)
    bt = lambda b: b.reshape(nd, 2, gh, n, p).transpose(0, 1, 2, 4, 3).reshape(nd, 2, kin, n)
    ct = lambda c: c.reshape(nd, 2, gh, p, n).transpose(0, 1, 2, 4, 3).reshape(nd, 2, kst, p)
    vec = pl.BlockSpec((1, 1, S5_LANES), lambda d: (d, 0, 0))
    cvec = pl.BlockSpec((1, 2, gh, n), lambda d: (d, 0, 0, 0))
    bin_spec = pl.BlockSpec((1, 2, kin, n), lambda d: (d, 0, 0, 0))
    cin_spec = pl.BlockSpec((1, 2, kst, p), lambda d: (d, 0, 0, 0))
    bspec = pl.BlockSpec((1, 2, kin, kst), lambda d: (d, 0, 0, 0))
    cspec = pl.BlockSpec((1, 2, kst, kin), lambda d: (d, 0, 0, 0))
    return pl.pallas_call(
        _s5_disc_kernel,
        grid=(nd,),
        in_specs=[vec, vec, vec, cvec, cvec, cvec, bin_spec, bin_spec, cin_spec, cin_spec],
        out_specs=[pl.BlockSpec((1, 2, S5_LANES), lambda d: (d, 0, 0)), bspec, bspec, cspec, cspec],
        out_shape=[jax.ShapeDtypeStruct((nd, 2, S5_LANES), F32),
                   jax.ShapeDtypeStruct((nd, 2, kin, kst), BF16),
                   jax.ShapeDtypeStruct((nd, 2, kin, kst), BF16),
                   jax.ShapeDtypeStruct((nd, 2, kst, kin), BF16),
                   jax.ShapeDtypeStruct((nd, 2, kst, kin), BF16)],
        compiler_params=_cparams("arbitrary"),
        name="s5_disc",
    )(are, aim, ldt, compact(a_re), compact(a_im), compact(ldt_gn),
      bt(b_re), bt(b_im), ct(c_re), ct(c_im))


S5_CTX_CHUNKS = CTX_LEN // S5_TC
S5_LAT_CHUNKS = SEQ // S5_TC


def _s5_scan_kernel(ucf_ref, ulf_ref, ucb_ref, ulb_ref, lam_ref, wbr_ref, wbi_ref, wcr_ref, wci_ref,
                    yf_ref, yb_ref, fr_ref, fi_ref, gr_ref, gi_ref, sr_ref, si_ref, ut_ref, yt_ref):
    c = pl.program_id(0)
    tc = S5_TC
    kin, kst = S5_WIDTH // 2, S5_LANES // 2
    nlb = kst // LANE

    @pl.when(c == 0)
    def _():
        sr_ref[...] = jnp.zeros_like(sr_ref)
        si_ref[...] = jnp.zeros_like(si_ref)

    in_ctx = c < S5_CTX_CHUNKS
    state = [(fr_ref, fi_ref), (gr_ref, gi_ref)]
    nub = S5_WIDTH // LANE

    u_dir = []
    for d, (uc_ref, ul_ref) in enumerate(((ucf_ref, ulf_ref), (ucb_ref, ulb_ref))):
        u = jnp.where(in_ctx, uc_ref[...], ul_ref[...])
        for j in range(nub):
            for b in range(BATCH):
                ut_ref[d, j, pl.ds(b, tc, stride=BATCH), :] = u[b, :, j * LANE:(j + 1) * LANE]
        u_dir.append(jnp.concatenate([ut_ref[d, j] for j in range(nub)], axis=1).astype(BF16))

    def project(d, h, part):
        w_ref = (wbr_ref, wbi_ref)[part]
        dst = state[d][part]
        p = _dot(u_dir[d][:, h * kin:(h + 1) * kin], w_ref[d, h])
        for j in range(nlb):
            dst[h * nlb + j] = p[:, j * LANE:(j + 1) * LANE]

    def scan_steps(h, k0, k1, carry):
        ls = slice(h * nlb, (h + 1) * nlb)
        fwd_rows = lax.broadcasted_iota(jnp.int32, (nlb, SUBLANE, LANE), 1) < BATCH
        la_r = jnp.where(fwd_rows, lam_ref[0, 0, ls], lam_ref[1, 0, ls])
        la_i = jnp.where(fwd_rows, lam_ref[0, 1, ls], lam_ref[1, 1, ls])
        lb_r = jnp.where(fwd_rows, lam_ref[1, 0, ls], lam_ref[0, 0, ls])
        lb_i = jnp.where(fwd_rows, lam_ref[1, 1, ls], lam_ref[0, 1, ls])
        s_r, s_i = carry
        for k in range(k0, k1):
            rf = slice(k * SUBLANE, (k + 1) * SUBLANE)
            kb = tc // 2 - 1 - k
            rb = slice(kb * SUBLANE, (kb + 1) * SUBLANE)
            f_r, f_i = fr_ref[ls, rf, :], fi_ref[ls, rf, :]
            g_r, g_i = gr_ref[ls, rb, :], gi_ref[ls, rb, :]
            a_r = la_r * s_r - la_i * s_i + jnp.where(fwd_rows, f_r, g_r)
            a_i = la_r * s_i + la_i * s_r + jnp.where(fwd_rows, f_i, g_i)
            t_r = pltpu.roll(a_r, BATCH, 1)
            t_i = pltpu.roll(a_i, BATCH, 1)
            b_r = lb_r * t_r - lb_i * t_i + jnp.where(fwd_rows, g_r, f_r)
            b_i = lb_r * t_i + lb_i * t_r + jnp.where(fwd_rows, g_i, f_i)
            fr_ref[ls, rf, :] = jnp.where(fwd_rows, a_r, b_r)
            fi_ref[ls, rf, :] = jnp.where(fwd_rows, a_i, b_i)
            gr_ref[ls, rb, :] = jnp.where(fwd_rows, b_r, a_r)
            gi_ref[ls, rb, :] = jnp.where(fwd_rows, b_i, a_i)
            s_r, s_i = pltpu.roll(b_r, BATCH, 1), pltpu.roll(b_i, BATCH, 1)
        return s_r, s_i

    def half_rows(src, h):
        return jnp.concatenate([src[h * nlb + j] for j in range(nlb)], axis=1).astype(BF16)

    def readout(d, h):
        y = _dot(half_rows(state[d][0], h), wcr_ref[d, h]) - _dot(half_rows(state[d][1], h), wci_ref[d, h])
        for j in range(kin // LANE):
            yt_ref[d, h * (kin // LANE) + j] = y[:, j * LANE:(j + 1) * LANE]

    def emit(d):
        y_ref = (yf_ref, yb_ref)[d]
        for b in range(BATCH):
            y_ref[b] = jnp.concatenate([yt_ref[d, j, pl.ds(b, tc, stride=BATCH), :] for j in range(nub)], axis=1)

    quarters = 4
    per = tc // 2 // quarters
    pieces = [(d, part) for d in range(2) for part in range(2)]
    for d, part in pieces:
        project(d, 0, part)
    carry = (sr_ref[0:nlb], si_ref[0:nlb])
    for q, (d, part) in enumerate(pieces):
        carry = scan_steps(0, q * per, (q + 1) * per, carry)
        project(d, 1, part)
    sr_ref[0:nlb], si_ref[0:nlb] = carry
    carry = (sr_ref[nlb:2 * nlb], si_ref[nlb:2 * nlb])
    for q in range(quarters):
        carry = scan_steps(1, q * per, (q + 1) * per, carry)
        if q % 2 == 1:
            readout(q // 2, 0)
    sr_ref[nlb:2 * nlb], si_ref[nlb:2 * nlb] = carry
    for d in range(2):
        readout(d, 1)
        emit(d)


def _s5_scan(u_c, u_l, lam, wbr, wbi, wcr, wci, layer, skip_chunks):
    tc = S5_TC
    nchunk = S5_CTX_CHUNKS + S5_LAT_CHUNKS
    kin, kst = S5_WIDTH // 2, S5_LANES // 2
    zc3 = u_c.reshape(BATCH, CTX_LEN, S5_WIDTH)
    zl3 = u_l.reshape(BATCH, SEQ, S5_WIDTH)

    def bwd_chunk(c):
        return jnp.where(c < S5_CTX_CHUNKS, S5_CTX_CHUNKS - 1 - c, nchunk + S5_CTX_CHUNKS - 1 - c)

    def ctx_spec(chunk_of):
        return pl.BlockSpec((BATCH, tc, S5_WIDTH),
                            lambda c: (0, jnp.clip(chunk_of(c), 0, S5_CTX_CHUNKS - 1), 0))

    def lat_spec(chunk_of):
        return pl.BlockSpec((BATCH, tc, S5_WIDTH),
                            lambda c: (0, jnp.clip(chunk_of(c) - S5_CTX_CHUNKS, 0, S5_LAT_CHUNKS - 1), 0))

    def out_block(chunk):
        return jnp.where(chunk >= S5_CTX_CHUNKS, chunk - S5_CTX_CHUNKS, S5_LAT_CHUNKS + chunk)

    def out_spec(chunk_of):
        return pl.BlockSpec((BATCH, tc, S5_WIDTH),
                            lambda c: (0, out_block(chunk_of(jnp.maximum(c, skip_chunks))), 0))

    once = pl.Buffered(1)
    bspec = pl.BlockSpec((2, 2, kin, kst), lambda c: (layer, 0, 0, 0), pipeline_mode=once)
    cspec = pl.BlockSpec((2, 2, kst, kin), lambda c: (layer, 0, 0, 0), pipeline_mode=once)
    t_out = (nchunk - skip_chunks) * tc
    nblk = S5_LANES // LANE
    big = pltpu.VMEM((nblk, tc * BATCH, LANE), F32)
    state = pltpu.VMEM((nblk, SUBLANE, LANE), F32)
    narrow = pltpu.VMEM((2, S5_WIDTH // LANE, tc * BATCH, LANE), F32)
    lam = lam.reshape(DEPTH * 2, 2, nblk, 1, LANE)
    fwd_chunk = lambda c: c
    return pl.pallas_call(
        _s5_scan_kernel,
        grid=(nchunk,),
        in_specs=[ctx_spec(fwd_chunk), lat_spec(fwd_chunk), ctx_spec(bwd_chunk), lat_spec(bwd_chunk),
                  pl.BlockSpec((2, 2, nblk, 1, LANE), lambda c: (layer, 0, 0, 0, 0)),
                  bspec, bspec, cspec, cspec],
        out_specs=[out_spec(fwd_chunk), out_spec(bwd_chunk)],
        out_shape=[jax.ShapeDtypeStruct((BATCH, t_out, S5_WIDTH), F32)] * 2,
        scratch_shapes=[big, big, big, big, state, state, narrow, narrow],
        compiler_params=_cparams("arbitrary"),
        name="s5_scan",
    )(zc3, zl3, zc3, zl3, lam, wbr, wbi, wcr, wci)


def _sink_rows(sink_ref, g, rows):
    r = lax.broadcasted_iota(jnp.int32, (rows, 1), 0) // (rows // SWA_REP)
    s0, s1, s2 = sink_ref[SWA_REP * g], sink_ref[SWA_REP * g + 1], sink_ref[SWA_REP * g + 2]
    return jnp.where(r == 0, s0, jnp.where(r == 1, s1, s2))


def _stack_heads(x, n):
    return jnp.concatenate([x[:, h * HEAD_DIM:(h + 1) * HEAD_DIM] for h in range(n)], axis=0)


def _unstack_heads(x, n):
    rows = x.shape[0] // n
    return jnp.concatenate([x[h * rows:(h + 1) * rows] for h in range(n)], axis=1)


def _with_ones(v):
    return jnp.concatenate([v, jnp.ones_like(v)], axis=1)


def _swa_latent_kernel(sink_ref, q_ref, kp_ref, km_ref, kn_ref, vp_ref, vm_ref, vn_ref,
                       kx_ref, vx_ref, o_ref):
    n = pl.program_id(1)
    rows = SWA_REP * BLOCK
    qi = lax.broadcasted_iota(jnp.int32, (rows, 3 * BLOCK), 0) % BLOCK
    kj = lax.broadcasted_iota(jnp.int32, (rows, 3 * BLOCK), 1)
    in_window = (kj >= qi) & (kj <= qi + 2 * BLOCK)
    kv = []
    for g in range(SWA_KV_HEADS):
        gs = slice(g * HEAD_DIM, (g + 1) * HEAD_DIM)
        kspan = jnp.concatenate([kp_ref[:, gs], km_ref[:, gs], kn_ref[:, gs]], axis=0)
        vspan = _with_ones(jnp.concatenate([vp_ref[:, gs], vm_ref[:, gs], vn_ref[:, gs]], axis=0))
        kv.append((kspan, vspan, kx_ref[:, gs], _with_ones(vx_ref[:, gs]), _sink_rows(sink_ref, g, rows)))

    def scores(g, j):
        kspan, _, kx, _, _ = kv[g]
        qs = slice(g * SWA_REP * HEAD_DIM, (g + 1) * SWA_REP * HEAD_DIM)
        q = _stack_heads(q_ref[j * BLOCK:(j + 1) * BLOCK, qs], SWA_REP)
        return _dot_nt(q, kx), _dot_nt(q, kspan[j * BLOCK:(j + 3) * BLOCK])

    work = [(g, j) for g in range(SWA_KV_HEADS) for j in range(SWA_QB)]
    s_next = scores(*work[0])
    for idx, (g, j) in enumerate(work):
        s_ctx, s_band = s_next
        if idx + 1 < len(work):
            s_next = scores(*work[idx + 1])
        _, vspan, _, vx, sk = kv[g]
        qs = slice(g * SWA_REP * HEAD_DIM, (g + 1) * SWA_REP * HEAD_DIM)
        kpos = (n * SWA_QB + j - 1) * BLOCK + kj
        valid = in_window & (kpos >= 0) & (kpos < SEQ)
        s_band = jnp.where(valid, s_band, NEG_INF)
        m = jnp.maximum(jnp.maximum(jnp.max(s_ctx, axis=-1, keepdims=True),
                                    jnp.max(s_band, axis=-1, keepdims=True)), sk)
        p_ctx = jnp.exp(s_ctx - m).astype(BF16)
        p_band = jnp.exp(s_band - m).astype(BF16)
        o = _dot(p_ctx, vx) + _dot(p_band, vspan[j * BLOCK:(j + 3) * BLOCK])
        den = o[:, HEAD_DIM:] + jnp.exp(sk - m)
        o_ref[j * BLOCK:(j + 1) * BLOCK, qs] = _unstack_heads(
            o[:, :HEAD_DIM] / den, SWA_REP).astype(o_ref.dtype)


def _swa_latent(qk, vs, qk_c, vs_c, sink):
    nb = SEQ // BLOCK
    ng = nb // SWA_QB
    span = SWA_QB * BLOCK
    kvw = SWA_KV_HEADS * HEAD_DIM
    kcol = SWA_HEADS * HEAD_DIM // kvw
    vcol = 0
    kxcol = kcol

    def edge(col, blk_of):
        return pl.BlockSpec((BLOCK, kvw), lambda b, n, s: (b * nb + jnp.clip(blk_of(n), 0, nb - 1), col))

    def main(col):
        return pl.BlockSpec((span, kvw), lambda b, n, s: (b * ng + n, col))

    prev_blk = lambda n: n * SWA_QB - 1
    next_blk = lambda n: (n + 1) * SWA_QB
    grid_spec = pltpu.PrefetchScalarGridSpec(
        num_scalar_prefetch=1,
        grid=(BATCH, ng),
        in_specs=[pl.BlockSpec((span, SWA_HEADS * HEAD_DIM), lambda b, n, s: (b * ng + n, 0)),
                  edge(kcol, prev_blk), main(kcol), edge(kcol, next_blk),
                  edge(vcol, prev_blk), main(vcol), edge(vcol, next_blk),
                  pl.BlockSpec((CTX_LEN, kvw), lambda b, n, s: (b, kxcol)),
                  pl.BlockSpec((CTX_LEN, kvw), lambda b, n, s: (b, vcol))],
        out_specs=pl.BlockSpec((span, SWA_HEADS * HEAD_DIM), lambda b, n, s: (b * ng + n, 0)),
    )
    return pl.pallas_call(
        _swa_latent_kernel,
        grid_spec=grid_spec,
        out_shape=jax.ShapeDtypeStruct((BATCH * SEQ, SWA_HEADS * HEAD_DIM), BF16),
        compiler_params=_cparams("arbitrary", "arbitrary"),
        name="swa_latent",
    )(sink, qk, qk, qk, qk, vs, vs, vs, qk_c, vs_c)


def _swa_context_kernel(sink_ref, q_ref, k_ref, v_ref, o_ref):
    g = pl.program_id(1)
    q = _stack_heads(q_ref[...], SWA_REP)
    s = _dot_nt(q, k_ref[...])
    sk = _sink_rows(sink_ref, g, SWA_REP * CTX_LEN)
    m = jnp.maximum(jnp.max(s, axis=-1, keepdims=True), sk)
    p = jnp.exp(s - m)
    den = jnp.sum(p, axis=-1, keepdims=True) + jnp.exp(sk - m)
    o = _dot(p.astype(BF16), v_ref[...]) / den
    o_ref[...] = _unstack_heads(o, SWA_REP).astype(o_ref.dtype)


def _swa_context(qk_c, vs_c, sink):
    qw = SWA_REP * HEAD_DIM
    grid_spec = pltpu.PrefetchScalarGridSpec(
        num_scalar_prefetch=1,
        grid=(BATCH, SWA_KV_HEADS),
        in_specs=[pl.BlockSpec((CTX_LEN, qw), lambda b, g, s: (b, g)),
                  pl.BlockSpec((CTX_LEN, HEAD_DIM), lambda b, g, s: (b, SWA_HEADS + g)),
                  pl.BlockSpec((CTX_LEN, HEAD_DIM), lambda b, g, s: (b, g))],
        out_specs=pl.BlockSpec((CTX_LEN, qw), lambda b, g, s: (b, g)),
    )
    return pl.pallas_call(
        _swa_context_kernel,
        grid_spec=grid_spec,
        out_shape=jax.ShapeDtypeStruct((BATCH * CTX_LEN, SWA_HEADS * HEAD_DIM), BF16),
        compiler_params=_cparams("arbitrary", "arbitrary"),
        name="swa_context",
    )(sink, qk_c, qk_c, vs_c)


def _mla_attn_kernel(*refs, nseg):
    q_ref = refs[0]
    k_refs = refs[1:1 + nseg]
    v_refs = refs[1 + nseg:1 + 2 * nseg]
    o_ref = refs[1 + 2 * nseg]
    def scores(h):
        q = q_ref[:, h * MLA_QK_PAD:(h + 1) * MLA_QK_PAD]
        return [_dot_nt(q, k[:, h * MLA_QK_PAD:(h + 1) * MLA_QK_PAD]) for k in k_refs]

    s_next = scores(0)
    for h in range(MLA_HEADS):
        s = s_next
        if h + 1 < MLA_HEADS:
            s_next = scores(h + 1)
        m = functools.reduce(jnp.maximum, [jnp.max(x, axis=-1, keepdims=True) for x in s])
        p = [jnp.exp(x - m).astype(BF16) for x in s]
        o = sum(_dot(x, v[:, h * MLA_V_PAD:(h + 1) * MLA_V_PAD]) for x, v in zip(p, v_refs))
        o_ref[:, h * MLA_V:(h + 1) * MLA_V] = (o[:, :MLA_V] / o[:, MLA_V:]).astype(o_ref.dtype)


def _mla_attn(q, ks, vs, n_q, tq):
    nseg = len(ks)
    nq = n_q // tq
    lens = [k.shape[0] // BATCH for k in ks]
    kw, vw = MLA_HEADS * MLA_QK_PAD, MLA_HEADS * MLA_V
    in_specs = [pl.BlockSpec((tq, kw), lambda b, i: (b * nq + i, 0))]
    in_specs += [pl.BlockSpec((n, kw), lambda b, i: (b, 0)) for n in lens]
    in_specs += [pl.BlockSpec((n, MLA_HEADS * MLA_V_PAD), lambda b, i: (b, 0)) for n in lens]
    return pl.pallas_call(
        functools.partial(_mla_attn_kernel, nseg=nseg),
        grid=(BATCH, nq),
        in_specs=in_specs,
        out_specs=pl.BlockSpec((tq, vw), lambda b, i: (b * nq + i, 0)),
        out_shape=jax.ShapeDtypeStruct((BATCH * n_q, vw), BF16),
        compiler_params=_cparams("arbitrary", "arbitrary"),
        name="mla_attn",
    )(q, *ks, *vs)


def _out_proj_kernel(u_ref, yf_ref, yb_ref, d_ref, gw_ref, gb_ref, a2_ref, a3_ref, w_ref, x_ref, g_ref, gate_ref,
                     fg_ref, fshift_ref, fscale_ref, o_ref, f_ref):
    nc = 512
    k1 = S5_WIDTH
    k2 = k1 + a2_ref.shape[1]
    for rs in _sub_tiles(o_ref.shape[0]):
        y = d_ref[...] * u_ref[0, rs, :] + yf_ref[0, rs, :] + yb_ref[0, rs, :]
        gl = jax.nn.gelu(y)
        a1 = (gl * jax.nn.sigmoid(_dot(gl.astype(BF16), gw_ref[...]) + gb_ref[...])).astype(BF16)
        for n in range(D_MODEL // nc):
            cs = slice(n * nc, (n + 1) * nc)
            o_ref[rs, cs] = (_dot(a2_ref[rs, :], w_ref[k1:k2, cs]) + _dot(a3_ref[rs, :], w_ref[k2:D_MODEL, cs])
                             + _dot(a1, w_ref[0:k1, cs]))
        xn = x_ref[rs, :] + gate_ref[0] * _rms(o_ref[rs, :], g_ref[...])
        o_ref[rs, :] = xn
        f_ref[rs, :] = (_rms(xn, fg_ref[...]) * (1.0 + fscale_ref[0]) + fshift_ref[0]).astype(BF16)


def _out_proj(u3, yf, yb, y_off, d, glu_w, glu_b, a2, a3, w, x2d, g, gate, fg, fshift, fscale, layer, tm, mod_row):
    rows = a2.shape[0]
    per_b = u3.shape[1] // tm
    once = pl.Buffered(1)
    uspec = pl.BlockSpec((1, tm, S5_WIDTH), lambda i: (i // per_b, i % per_b, 0))
    yspec = pl.BlockSpec((1, tm, S5_WIDTH), lambda i: (i // per_b, i % per_b + y_off // tm, 0))
    vec5 = pl.BlockSpec((1, S5_WIDTH), lambda i: (0, 0))
    vec = pl.BlockSpec((1, D_MODEL), lambda i: (0, 0))
    mod = pl.BlockSpec((1, 1, D_MODEL), lambda i: (mod_row(i), 0, 0))
    row = pl.BlockSpec((tm, D_MODEL), lambda i: (i, 0))
    return pl.pallas_call(
        _out_proj_kernel,
        grid=(rows // tm,),
        in_specs=[uspec, yspec, yspec, vec5,
                  pl.BlockSpec((None, S5_WIDTH, S5_WIDTH), lambda i: (layer, 0, 0), pipeline_mode=once), vec5,
                  pl.BlockSpec((tm, a2.shape[1]), lambda i: (i, 0)),
                  pl.BlockSpec((tm, a3.shape[1]), lambda i: (i, 0)),
                  pl.BlockSpec((None, D_MODEL, D_MODEL), lambda i: (layer, 0, 0), pipeline_mode=once),
                  row, vec, mod, vec, mod, mod],
        out_specs=[row, row],
        out_shape=[jax.ShapeDtypeStruct((rows, D_MODEL), F32), jax.ShapeDtypeStruct((rows, D_MODEL), BF16)],
        compiler_params=_cparams("arbitrary"),
        name="out_proj",
    )(u3, yf, yb, d.reshape(1, -1), glu_w, glu_b.reshape(1, -1), a2, a3, w, x2d, g.reshape(1, -1), gate,
      fg.reshape(1, -1), fshift, fscale)


def _ffn_kernel(x_ref, f_ref, w1_ref, w2_ref, gpost_ref, gate_ref, o_ref, h_ref):
    j = pl.program_id(1)

    @pl.when(j == 0)
    def _():
        o_ref[...] = jnp.zeros_like(o_ref)

    h = _dot(f_ref[...], w1_ref[...].astype(BF16))
    h_ref[...] = jnp.square(jnp.maximum(h, 0.0)).astype(BF16)
    nc = 512
    for n in range(D_MODEL // nc):
        cs = slice(n * nc, (n + 1) * nc)
        o_ref[:, cs] += _dot(h_ref[...], w2_ref[:, cs].astype(BF16))

    @pl.when(j == pl.num_programs(1) - 1)
    def _():
        o_ref[...] = x_ref[...] + gate_ref[0] * _rms(o_ref[...], gpost_ref[...])


def _ffn(x2d, f2d, w1, w2, gpost, gate, layer, tm, mod_row):
    rows = x2d.shape[0]
    tf = TF_FFN
    mod = pl.BlockSpec((1, 1, D_MODEL), lambda i, j: (mod_row(i), 0, 0))
    vec = pl.BlockSpec((1, D_MODEL), lambda i, j: (0, 0))
    once = pl.Buffered(1)
    return pl.pallas_call(
        _ffn_kernel,
        grid=(rows // tm, D_FF // tf),
        in_specs=[pl.BlockSpec((tm, D_MODEL), lambda i, j: (i, 0)),
                  pl.BlockSpec((tm, D_MODEL), lambda i, j: (i, 0), pipeline_mode=once),
                  pl.BlockSpec((None, D_MODEL, tf), lambda i, j: (layer, 0, j)),
                  pl.BlockSpec((None, tf, D_MODEL), lambda i, j: (layer, j, 0)),
                  vec, mod],
        out_specs=pl.BlockSpec((tm, D_MODEL), lambda i, j: (i, 0), pipeline_mode=once),
        out_shape=jax.ShapeDtypeStruct((rows, D_MODEL), F32),
        scratch_shapes=[pltpu.VMEM((tm, tf), BF16)],
        compiler_params=_cparams("arbitrary", "arbitrary"),
        name="ffn",
    )(x2d, f2d, w1, w2, gpost.reshape(1, -1), gate)


def _rope_tables():
    t = np.arange(SEQ)
    row = (t // GRID_W).astype(np.float32)[:, None]
    col = (t % GRID_W).astype(np.float32)[:, None]

    def tables(rot_dim):
        quarter = rot_dim // 4
        inv_freq = np.float32(ROPE_BASE) ** (-np.arange(quarter, dtype=np.float32) / np.float32(quarter))
        ar, ac = row * inv_freq, col * inv_freq
        cos = np.concatenate([np.cos(ar), np.cos(ar), np.cos(ac), np.cos(ac)], axis=1)
        sin = np.concatenate([-np.sin(ar), np.sin(ar), -np.sin(ac), np.sin(ac)], axis=1)
        pad = LANE - rot_dim
        if pad:
            cos = np.concatenate([cos, np.ones((SEQ, pad), np.float32)], axis=1)
            sin = np.concatenate([sin, np.zeros((SEQ, pad), np.float32)], axis=1)
        return jnp.asarray(cos, F32), jnp.asarray(sin, F32)

    return tables(HEAD_DIM), tables(MLA_ROPE)


def kernel(x, c, ctx, c_ctx, ada_w, ada_b, norm_mix_pre, norm_mix_post, norm_ffn_pre, norm_ffn_post, w_in, w_out, s5_a_re, s5_a_im, s5_log_dt, s5_b_re, s5_b_im, s5_c_re, s5_c_im, s5_d, s5_glu_w, s5_glu_b, swa_sink, mla_q_norm, mla_w_uq, mla_kv_norm, mla_w_ukv, ffn_w1, ffn_w2):
    (cos_swa, sin_swa), (cos_mla, sin_mla) = _rope_tables()
    tabs = (cos_swa, sin_swa, cos_mla, sin_mla)

    cvec = jnp.concatenate([c, c_ctx[None, :], jnp.zeros((3, D_MODEL), F32)], axis=0)
    mods = _ada(cvec, ada_w, ada_b)

    tm_p, tm_f = TM_PROJ, TM_FFN

    def lat_row(tm):
        return lambda i: i // (SEQ // tm)

    ctx_row = lambda i: 4

    xl = x.reshape(BATCH * SEQ, D_MODEL)
    xc = ctx.reshape(BATCH * CTX_LEN, D_MODEL)

    w_in_pad = jnp.pad(w_in, ((0, 0), (0, 0), (0, IN_COLS - IN_WIDTH))).astype(BF16)
    w_out_b = w_out.astype(BF16)
    glu_w_b = s5_glu_w.astype(BF16)
    w_uq_pad = jnp.pad(mla_w_uq.reshape(DEPTH, MLA_Q_RANK, MLA_HEADS, MLA_NOPE + MLA_ROPE),
                       ((0, 0), (0, 0), (0, 0), (0, MLA_QK_PAD - MLA_NOPE - MLA_ROPE))
                       ).reshape(DEPTH, MLA_Q_RANK, MLA_HEADS * MLA_QK_PAD).astype(BF16)
    w_ukv4 = mla_w_ukv.reshape(DEPTH, MLA_KV_RANK, MLA_HEADS, MLA_NOPE + MLA_V)
    w_ukv_perm = jnp.concatenate([w_ukv4[..., :MLA_NOPE].reshape(DEPTH, MLA_KV_RANK, -1),
                                  w_ukv4[..., MLA_NOPE:].reshape(DEPTH, MLA_KV_RANK, -1)], axis=2).astype(BF16)
    s5_w = _s5_disc(s5_a_re, s5_a_im, s5_log_dt, s5_b_re, s5_b_im, s5_c_re, s5_c_im)

    for i in range(DEPTH):
        need_ctx = i < DEPTH - 1
        mod = [mods[i, :, k * D_MODEL:(k + 1) * D_MODEL].reshape(8, 1, D_MODEL) for k in range(6)]
        g_pre = norm_mix_pre[i].reshape(1, -1)

        u_l, qk_l, vs_l, q_l, k_l, v_l = _mix_in(
            xl, mod[0], mod[1], g_pre, w_in_pad, w_uq_pad, w_ukv_perm, mla_q_norm[i], mla_kv_norm[i],
            tabs, i, tm_p, lat_row(tm_p), True, SEQ)
        u_c, qk_c, vs_c, q_c, k_c, v_c = _mix_in(
            xc, mod[0], mod[1], g_pre, w_in_pad, w_uq_pad, w_ukv_perm, mla_q_norm[i], mla_kv_norm[i],
            tabs, i, tm_p, ctx_row, False, tm_p)

        yf, yb = _s5_scan(u_c, u_l, *s5_w, i, 0 if need_ctx else S5_CTX_CHUNKS)
        u3_l = u_l.reshape(BATCH, SEQ, S5_WIDTH)
        u3_c = u_c.reshape(BATCH, CTX_LEN, S5_WIDTH)

        swa_l = _swa_latent(qk_l, vs_l, qk_c, vs_c, swa_sink[i])

        mla_l = _mla_attn(q_l, [k_c, k_l], [v_c, v_l], SEQ, TQ_MLA)

        xl, fl = _out_proj(u3_l, yf, yb, 0, s5_d[i], glu_w_b, s5_glu_b[i], swa_l, mla_l, w_out_b, xl,
                           norm_mix_post[i], mod[2], norm_ffn_pre[i], mod[3], mod[4], i, tm_p, lat_row(tm_p))
        xl = _ffn(xl, fl, ffn_w1, ffn_w2, norm_ffn_post[i], mod[5], i, tm_f, lat_row(tm_f))

        if need_ctx:
            swa_c = _swa_context(qk_c, vs_c, swa_sink[i])
            mla_c = _mla_attn(q_c, [k_c], [v_c], CTX_LEN, TQ_MLA_CTX)
            xc, fc = _out_proj(u3_c, yf, yb, SEQ, s5_d[i], glu_w_b, s5_glu_b[i], swa_c, mla_c, w_out_b, xc,
                               norm_mix_post[i], mod[2], norm_ffn_pre[i], mod[3], mod[4], i, CTX_LEN, ctx_row)
            xc = _ffn(xc, fc, ffn_w1, ffn_w2, norm_ffn_post[i], mod[5], i, tm_f, ctx_row)

    return xl.reshape(BATCH, SEQ, D_MODEL)
```

```python
import functools

import jax
import jax.numpy as jnp
import numpy as np
from jax import lax
from jax.experimental import pallas as pl
from jax.experimental.pallas import tpu as pltpu

F32 = jnp.float32
BF16 = jnp.bfloat16

D_MODEL = 2048
BATCH = 4
SEQ = 2048
DEPTH = 2
GRID_W = 64
CTX_LEN = 256
EPS = 1e-6
ROPE_BASE = 10000.0
NEG_INF = -1e30
BLOCK = 128
HEAD_DIM = 128
S5_WIDTH = 512
S5_GROUP = 16
S5_GROUPS = 32
S5_STATE = 64
S5_LANES = S5_GROUPS * S5_STATE
SWA_HEADS = 6
SWA_KV_HEADS = 2
SWA_REP = SWA_HEADS // SWA_KV_HEADS
MLA_HEADS = 6
MLA_Q_RANK = 768
MLA_KV_RANK = 512
MLA_NOPE = 128
MLA_ROPE = 64
MLA_V = 128
MLA_QK_PAD = 256
MLA_V_PAD = 256
D_FF = 4 * D_MODEL
IN_WIDTHS = (512, 768, 256, 256, 768, 512, 64)
IN_WIDTH = sum(IN_WIDTHS)
OFF_U, OFF_QS, OFF_KS, OFF_VS, OFF_CQ, OFF_CKV, OFF_KR = 0, 512, 1280, 1536, 1792, 2560, 3072

LANE = 128
SUBLANE = 8
V7X_VMEM_BYTES = 64 * 1024 * 1024
VMEM_LIMIT = V7X_VMEM_BYTES * 7 // 8

TM_PROJ = 512
SUB_ROWS = 256
TM_FFN = 1024
TF_FFN = 512
TN_ADA = 1024
TQ_MLA = 512
TQ_MLA_CTX = CTX_LEN
S5_TC = 128
SWA_QB = 4


def _cparams(*sem):
    return pltpu.CompilerParams(dimension_semantics=sem, vmem_limit_bytes=VMEM_LIMIT)


def _dot(a, b):
    return jnp.dot(a, b, preferred_element_type=F32)


def _dot_nt(a, b):
    return lax.dot_general(a, b, (((1,), (1,)), ((), ())), preferred_element_type=F32)


def _rms(x, g):
    return x * lax.rsqrt(jnp.mean(x * x, axis=-1, keepdims=True) + EPS) * g


def _sub_tiles(rows):
    return [slice(r, r + SUB_ROWS) for r in range(0, rows, SUB_ROWS)]


def _rope(x, cos, sin, half):
    lane = lax.broadcasted_iota(jnp.int32, x.shape, 1)
    fwd = pltpu.roll(x, LANE - half, 1)
    bwd = pltpu.roll(x, half, 1)
    sw = jnp.where((lane % (2 * half)) < half, fwd, bwd)
    return x * cos + sw * sin


def _ada_kernel(c_ref, w_ref, b_ref, o_ref):
    c = c_ref[...]
    s = c * jax.nn.sigmoid(c)
    o_ref[0] = _dot(s.astype(BF16), w_ref[0].astype(BF16)) + b_ref[0]


def _ada(cvec, ada_w, ada_b):
    tn = TN_ADA
    n = 6 * D_MODEL
    return pl.pallas_call(
        _ada_kernel,
        grid=(DEPTH, n // tn),
        in_specs=[pl.BlockSpec((8, D_MODEL), lambda l, j: (0, 0)),
                  pl.BlockSpec((1, D_MODEL, tn), lambda l, j: (l, 0, j)),
                  pl.BlockSpec((1, 1, tn), lambda l, j: (l, 0, j))],
        out_specs=pl.BlockSpec((1, 8, tn), lambda l, j: (l, 0, j)),
        out_shape=jax.ShapeDtypeStruct((DEPTH, 8, n), F32),
        compiler_params=_cparams("arbitrary", "arbitrary"),
        name="ada",
    )(cvec, ada_w, ada_b.reshape(DEPTH, 1, n))


IN_COLS = 3200
SWA_QK_W = (SWA_HEADS + SWA_KV_HEADS) * HEAD_DIM


def _mix_in_kernel(x_ref, shift_ref, scale_ref, g_ref, w_ref, wq_ref, wkv_ref, qn_ref, kvn_ref,
                   cs_ref, ss_ref, cm_ref, sm_ref,
                   u_ref, qk_ref, vs_ref, q_ref, k_ref, v_ref, h_ref, *, rope):
    for rs in _sub_tiles(x_ref.shape[0]):
        h = _rms(x_ref[rs, :], g_ref[...]) * (1.0 + scale_ref[0]) + shift_ref[0]
        h_ref[rs, :] = h.astype(BF16)

        def proj(lo, width):
            return _dot(h_ref[rs, :], w_ref[:, lo:lo + width])

        u_ref[rs, :] = proj(OFF_U, S5_WIDTH)

        swa_scale = HEAD_DIM ** -0.5
        per = 4
        for c in range(SWA_QK_W // (per * HEAD_DIM)):
            zc = proj(OFF_QS + c * per * HEAD_DIM, per * HEAD_DIM)
            for hh in range(per):
                head = c * per + hh
                xh = zc[:, hh * HEAD_DIM:(hh + 1) * HEAD_DIM]
                if rope:
                    xh = _rope(xh, cs_ref[rs, :], ss_ref[rs, :], HEAD_DIM // 4)
                if head < SWA_HEADS:
                    xh = xh * swa_scale
                qk_ref[rs, head * HEAD_DIM:(head + 1) * HEAD_DIM] = xh.astype(BF16)

        vs_ref[rs, :] = proj(OFF_VS, SWA_KV_HEADS * HEAD_DIM).astype(BF16)

        mla_scale = (MLA_NOPE + MLA_ROPE) ** -0.5
        cq = proj(OFF_CQ, MLA_Q_RANK)
        q = _dot(_rms(cq, qn_ref[...]).astype(BF16), wq_ref[...])
        for hd in range(MLA_HEADS):
            lo = hd * MLA_QK_PAD
            q_ref[rs, lo:lo + MLA_NOPE] = (q[:, lo:lo + MLA_NOPE] * mla_scale).astype(BF16)
            r = q[:, lo + MLA_NOPE:lo + MLA_QK_PAD]
            if rope:
                r = _rope(r, cm_ref[rs, :], sm_ref[rs, :], MLA_ROPE // 4)
            q_ref[rs, lo + MLA_NOPE:lo + MLA_QK_PAD] = (r * mla_scale).astype(BF16)

        ckv = proj(OFF_CKV, MLA_KV_RANK)
        kv = _dot(_rms(ckv, kvn_ref[...]).astype(BF16), wkv_ref[...])
        kr = proj(OFF_KR, LANE)
        if rope:
            kr = _rope(kr, cm_ref[rs, :], sm_ref[rs, :], MLA_ROPE // 4)
        kr = kr.astype(BF16)
        for hd in range(MLA_HEADS):
            lo = hd * MLA_QK_PAD
            k_ref[rs, lo:lo + MLA_NOPE] = kv[:, hd * MLA_NOPE:(hd + 1) * MLA_NOPE].astype(BF16)
            k_ref[rs, lo + MLA_NOPE:lo + MLA_QK_PAD] = kr
            vlo = hd * MLA_V_PAD
            vh = kv[:, MLA_HEADS * MLA_NOPE + hd * MLA_V:MLA_HEADS * MLA_NOPE + (hd + 1) * MLA_V].astype(BF16)
            v_ref[rs, vlo:vlo + MLA_V] = vh
            v_ref[rs, vlo + MLA_V:vlo + MLA_V_PAD] = jnp.ones_like(vh)


def _mix_in(x2d, shift, scale, g, w_in, w_uq, w_ukv, q_norm, kv_norm, tabs, layer, tm, mod_row, rope, seq):
    rows = x2d.shape[0]
    once = pl.Buffered(1)
    tab = pl.BlockSpec((tm, LANE), lambda i: (i % (seq // tm), 0))

    def res(shape):
        return pl.BlockSpec((None,) + shape, lambda i: (layer, 0, 0), pipeline_mode=once)

    def out(width):
        return pl.BlockSpec((tm, width), lambda i: (i, 0))

    widths = (S5_WIDTH, SWA_QK_W, SWA_KV_HEADS * HEAD_DIM,
              MLA_HEADS * MLA_QK_PAD, MLA_HEADS * MLA_QK_PAD, MLA_HEADS * MLA_V_PAD)
    dtypes = (F32, BF16, BF16, BF16, BF16, BF16)
    return pl.pallas_call(
        functools.partial(_mix_in_kernel, rope=rope),
        grid=(rows // tm,),
        in_specs=[pl.BlockSpec((tm, D_MODEL), lambda i: (i, 0)),
                  pl.BlockSpec((1, 1, D_MODEL), lambda i: (mod_row(i), 0, 0)),
                  pl.BlockSpec((1, 1, D_MODEL), lambda i: (mod_row(i), 0, 0)),
                  pl.BlockSpec((1, D_MODEL), lambda i: (0, 0)),
                  res((D_MODEL, IN_COLS)),
                  res((MLA_Q_RANK, MLA_HEADS * MLA_QK_PAD)),
                  res((MLA_KV_RANK, MLA_HEADS * (MLA_NOPE + MLA_V))),
                  pl.BlockSpec((1, MLA_Q_RANK), lambda i: (0, 0)),
                  pl.BlockSpec((1, MLA_KV_RANK), lambda i: (0, 0)),
                  tab, tab, tab, tab],
        out_specs=[out(w) for w in widths],
        out_shape=[jax.ShapeDtypeStruct((rows, w), dt) for w, dt in zip(widths, dtypes)],
        scratch_shapes=[pltpu.VMEM((tm, D_MODEL), BF16)],
        compiler_params=_cparams("arbitrary"),
        name="mix_in",
    )(x2d, shift, scale, g, w_in, w_uq, w_ukv, q_norm.reshape(1, -1), kv_norm.reshape(1, -1), *tabs)


S5_GH = S5_GROUPS // 2


def _zoh(are, aim, ldt):
    dt = jnp.exp(ldt)
    mag = jnp.exp(are * dt)
    lr = mag * jnp.cos(aim * dt)
    li = mag * jnp.sin(aim * dt)
    den = are * are + aim * aim
    nr = lr - 1.0
    return lr, li, (nr * are + li * aim) / den, (li * are - nr * aim) / den


def _s5_disc_kernel(are_ref, aim_ref, ldt_ref, arec_ref, aimc_ref, ldtc_ref, br_ref, bi_ref, cr_ref, ci_ref,
                    lam_ref, wbr_ref, wbi_ref, wcr_ref, wci_ref):
    lr, li, _, _ = _zoh(are_ref[0], aim_ref[0], ldt_ref[0])
    lam_ref[0, 0:1, :] = lr
    lam_ref[0, 1:2, :] = li
    gh, n, p = S5_GH, S5_STATE, S5_GROUP
    kin, kst = gh * p, gh * n

    def iota(shape, axis):
        return lax.broadcasted_iota(jnp.int32, shape, axis)

    tile_n = jnp.where(iota((n, kst), 0) == (iota((n, kst), 1) & (n - 1)), 1.0, 0.0).astype(BF16)
    tile_p = jnp.where(iota((p, kin), 0) == (iota((p, kin), 1) & (p - 1)), 1.0, 0.0).astype(BF16)
    lp, ln = p.bit_length() - 1, n.bit_length() - 1
    diag_b = (iota((kin, kst), 0) >> lp) == (iota((kin, kst), 1) >> ln)
    diag_c = (iota((kst, kin), 0) >> ln) == (iota((kst, kin), 1) >> lp)
    for h in range(2):
        _, _, fr, fi = _zoh(arec_ref[0, h], aimc_ref[0, h], ldtc_ref[0, h])
        fr = jnp.broadcast_to(fr[:, None, :], (gh, p, n)).reshape(kin, n)
        fi = jnp.broadcast_to(fi[:, None, :], (gh, p, n)).reshape(kin, n)
        bbr = (fr * br_ref[0, h] - fi * bi_ref[0, h]).astype(BF16)
        bbi = (fr * bi_ref[0, h] + fi * br_ref[0, h]).astype(BF16)
        wbr_ref[0, h] = jnp.where(diag_b, _dot(bbr, tile_n), 0.0).astype(BF16)
        wbi_ref[0, h] = jnp.where(diag_b, _dot(bbi, tile_n), 0.0).astype(BF16)
        wcr_ref[0, h] = jnp.where(diag_c, _dot(cr_ref[0, h].astype(BF16), tile_p), 0.0).astype(BF16)
        wci_ref[0, h] = jnp.where(diag_c, _dot(ci_ref[0, h].astype(BF16), tile_p), 0.0).astype(BF16)


def _s5_disc(a_re, a_im, log_dt, b_re, b_im, c_re, c_im):
    nd = DEPTH * 2
    gh, n, p = S5_GH, S5_STATE, S5_GROUP
    kin, kst = gh * p, gh * n
    are = a_re.reshape(nd, 1, S5_LANES)
    aim = a_im.reshape(nd, 1, S5_LANES)
    ldt_gn = jnp.broadcast_to(log_dt[..., None], a_re.shape)
    ldt = ldt_gn.reshape(nd, 1, S5_LANES)
    compact = lambda a: a.reshape(nd, 2, gh, n)
    bt = lambda b: b.reshape(nd, 2, gh, n, p).transpose(0, 1, 2, 4, 3).reshape(nd, 2, kin, n)
    ct = lambda c: c.reshape(nd, 2, gh, p, n).transpose(0, 1, 2, 4, 3).reshape(nd, 2, kst, p)
    vec = pl.BlockSpec((1, 1, S5_LANES), lambda d: (d, 0, 0))
    cvec = pl.BlockSpec((1, 2, gh, n), lambda d: (d, 0, 0, 0))
    bin_spec = pl.BlockSpec((1, 2, kin, n), lambda d: (d, 0, 0, 0))
    cin_spec = pl.BlockSpec((1, 2, kst, p), lambda d: (d, 0, 0, 0))
    bspec = pl.BlockSpec((1, 2, kin, kst), lambda d: (d, 0, 0, 0))
    cspec = pl.BlockSpec((1, 2, kst, kin), lambda d: (d, 0, 0, 0))
    return pl.pallas_call(
        _s5_disc_kernel,
        grid=(nd,),
        in_specs=[vec, vec, vec, cvec, cvec, cvec, bin_spec, bin_spec, cin_spec, cin_spec],
        out_specs=[pl.BlockSpec((1, 2, S5_LANES), lambda d: (d, 0, 0)), bspec, bspec, cspec, cspec],
        out_shape=[jax.ShapeDtypeStruct((nd, 2, S5_LANES), F32),
                   jax.ShapeDtypeStruct((nd, 2, kin, kst), BF16),
                   jax.ShapeDtypeStruct((nd, 2, kin, kst), BF16),
                   jax.ShapeDtypeStruct((nd, 2, kst, kin), BF16),
                   jax.ShapeDtypeStruct((nd, 2, kst, kin), BF16)],
        compiler_params=_cparams("arbitrary"),
        name="s5_disc",
    )(are, aim, ldt, compact(a_re), compact(a_im), compact(ldt_gn),
      bt(b_re), bt(b_im), ct(c_re), ct(c_im))


S5_CTX_CHUNKS = CTX_LEN // S5_TC
S5_LAT_CHUNKS = SEQ // S5_TC


def _s5_scan_kernel(ucf_ref, ulf_ref, ucb_ref, ulb_ref, lam_ref, wbr_ref, wbi_ref, wcr_ref, wci_ref,
                    yf_ref, yb_ref, fr_ref, fi_ref, gr_ref, gi_ref, sr_ref, si_ref, ut_ref, yt_ref):
    c = pl.program_id(0)
    tc = S5_TC
    kin, kst = S5_WIDTH // 2, S5_LANES // 2
    nlb = kst // LANE

    @pl.when(c == 0)
    def _():
        sr_ref[...] = jnp.zeros_like(sr_ref)
        si_ref[...] = jnp.zeros_like(si_ref)

    in_ctx = c < S5_CTX_CHUNKS
    state = [(fr_ref, fi_ref), (gr_ref, gi_ref)]
    nub = S5_WIDTH // LANE

    u_dir = []
    for d, (uc_ref, ul_ref) in enumerate(((ucf_ref, ulf_ref), (ucb_ref, ulb_ref))):
        u = jnp.where(in_ctx, uc_ref[...], ul_ref[...])
        for j in range(nub):
            for b in range(BATCH):
                ut_ref[d, j, pl.ds(b, tc, stride=BATCH), :] = u[b, :, j * LANE:(j + 1) * LANE]
        u_dir.append(jnp.concatenate([ut_ref[d, j] for j in range(nub)], axis=1).astype(BF16))

    def project(d, h, part):
        w_ref = (wbr_ref, wbi_ref)[part]
        dst = state[d][part]
        p = _dot(u_dir[d][:, h * kin:(h + 1) * kin], w_ref[d, h])
        for j in range(nlb):
            dst[h * nlb + j] = p[:, j * LANE:(j + 1) * LANE]

    def scan_steps(h, k0, k1, carry):
        ls = slice(h * nlb, (h + 1) * nlb)
        fwd_rows = lax.broadcasted_iota(jnp.int32, (nlb, SUBLANE, LANE), 1) < BATCH
        la_r = jnp.where(fwd_rows, lam_ref[0, 0, ls], lam_ref[1, 0, ls])
        la_i = jnp.where(fwd_rows, lam_ref[0, 1, ls], lam_ref[1, 1, ls])
        lb_r = jnp.where(fwd_rows, lam_ref[1, 0, ls], lam_ref[0, 0, ls])
        lb_i = jnp.where(fwd_rows, lam_ref[1, 1, ls], lam_ref[0, 1, ls])
        s_r, s_i = carry
        for k in range(k0, k1):
            rf = slice(k * SUBLANE, (k + 1) * SUBLANE)
            kb = tc // 2 - 1 - k
            rb = slice(kb * SUBLANE, (kb + 1) * SUBLANE)
            f_r, f_i = fr_ref[ls, rf, :], fi_ref[ls, rf, :]
            g_r, g_i = gr_ref[ls, rb, :], gi_ref[ls, rb, :]
            a_r = la_r * s_r - la_i * s_i + jnp.where(fwd_rows, f_r, g_r)
            a_i = la_r * s_i + la_i * s_r + jnp.where(fwd_rows, f_i, g_i)
            t_r = pltpu.roll(a_r, BATCH, 1)
            t_i = pltpu.roll(a_i, BATCH, 1)
            b_r = lb_r * t_r - lb_i * t_i + jnp.where(fwd_rows, g_r, f_r)
            b_i = lb_r * t_i + lb_i * t_r + jnp.where(fwd_rows, g_i, f_i)
            fr_ref[ls, rf, :] = jnp.where(fwd_rows, a_r, b_r)
            fi_ref[ls, rf, :] = jnp.where(fwd_rows, a_i, b_i)
            gr_ref[ls, rb, :] = jnp.where(fwd_rows, b_r, a_r)
            gi_ref[ls, rb, :] = jnp.where(fwd_rows, b_i, a_i)
            s_r, s_i = pltpu.roll(b_r, BATCH, 1), pltpu.roll(b_i, BATCH, 1)
        return s_r, s_i

    def half_rows(src, h):
        return jnp.concatenate([src[h * nlb + j] for j in range(nlb)], axis=1).astype(BF16)

    def readout(d, h):
        y = _dot(half_rows(state[d][0], h), wcr_ref[d, h]) - _dot(half_rows(state[d][1], h), wci_ref[d, h])
        for j in range(kin // LANE):
            yt_ref[d, h * (kin // LANE) + j] = y[:, j * LANE:(j + 1) * LANE]

    def emit(d):
        y_ref = (yf_ref, yb_ref)[d]
        for b in range(BATCH):
            y_ref[b] = jnp.concatenate([yt_ref[d, j, pl.ds(b, tc, stride=BATCH), :] for j in range(nub)], axis=1)

    quarters = 4
    per = tc // 2 // quarters
    pieces = [(d, part) for d in range(2) for part in range(2)]
    for d, part in pieces:
        project(d, 0, part)
    carry = (sr_ref[0:nlb], si_ref[0:nlb])
    for q, (d, part) in enumerate(pieces):
        carry = scan_steps(0, q * per, (q + 1) * per, carry)
        project(d, 1, part)
    sr_ref[0:nlb], si_ref[0:nlb] = carry
    carry = (sr_ref[nlb:2 * nlb], si_ref[nlb:2 * nlb])
    for q in range(quarters):
        carry = scan_steps(1, q * per, (q + 1) * per, carry)
        if q % 2 == 1:
            readout(q // 2, 0)
    sr_ref[nlb:2 * nlb], si_ref[nlb:2 * nlb] = carry
    for d in range(2):
        readout(d, 1)
        emit(d)


def _s5_scan(u_c, u_l, lam, wbr, wbi, wcr, wci, layer, skip_chunks):
    tc = S5_TC
    nchunk = S5_CTX_CHUNKS + S5_LAT_CHUNKS
    kin, kst = S5_WIDTH // 2, S5_LANES // 2
    zc3 = u_c.reshape(BATCH, CTX_LEN, S5_WIDTH)
    zl3 = u_l.reshape(BATCH, SEQ, S5_WIDTH)

    def bwd_chunk(c):
        return jnp.where(c < S5_CTX_CHUNKS, S5_CTX_CHUNKS - 1 - c, nchunk + S5_CTX_CHUNKS - 1 - c)

    def ctx_spec(chunk_of):
        return pl.BlockSpec((BATCH, tc, S5_WIDTH),
                            lambda c: (0, jnp.clip(chunk_of(c), 0, S5_CTX_CHUNKS - 1), 0))

    def lat_spec(chunk_of):
        return pl.BlockSpec((BATCH, tc, S5_WIDTH),
                            lambda c: (0, jnp.clip(chunk_of(c) - S5_CTX_CHUNKS, 0, S5_LAT_CHUNKS - 1), 0))

    def out_block(chunk):
        return jnp.where(chunk >= S5_CTX_CHUNKS, chunk - S5_CTX_CHUNKS, S5_LAT_CHUNKS + chunk)

    def out_spec(chunk_of):
        return pl.BlockSpec((BATCH, tc, S5_WIDTH),
                            lambda c: (0, out_block(chunk_of(jnp.maximum(c, skip_chunks))), 0))

    once = pl.Buffered(1)
    bspec = pl.BlockSpec((2, 2, kin, kst), lambda c: (layer, 0, 0, 0), pipeline_mode=once)
    cspec = pl.BlockSpec((2, 2, kst, kin), lambda c: (layer, 0, 0, 0), pipeline_mode=once)
    t_out = (nchunk - skip_chunks) * tc
    nblk = S5_LANES // LANE
    big = pltpu.VMEM((nblk, tc * BATCH, LANE), F32)
    state = pltpu.VMEM((nblk, SUBLANE, LANE), F32)
    narrow = pltpu.VMEM((2, S5_WIDTH // LANE, tc * BATCH, LANE), F32)
    lam = lam.reshape(DEPTH * 2, 2, nblk, 1, LANE)
    fwd_chunk = lambda c: c
    return pl.pallas_call(
        _s5_scan_kernel,
        grid=(nchunk,),
        in_specs=[ctx_spec(fwd_chunk), lat_spec(fwd_chunk), ctx_spec(bwd_chunk), lat_spec(bwd_chunk),
                  pl.BlockSpec((2, 2, nblk, 1, LANE), lambda c: (layer, 0, 0, 0, 0)),
                  bspec, bspec, cspec, cspec],
        out_specs=[out_spec(fwd_chunk), out_spec(bwd_chunk)],
        out_shape=[jax.ShapeDtypeStruct((BATCH, t_out, S5_WIDTH), F32)] * 2,
        scratch_shapes=[big, big, big, big, state, state, narrow, narrow],
        compiler_params=_cparams("arbitrary"),
        name="s5_scan",
    )(zc3, zl3, zc3, zl3, lam, wbr, wbi, wcr, wci)


def _sink_rows(sink_ref, g, rows):
    r = lax.broadcasted_iota(jnp.int32, (rows, 1), 0) // (rows // SWA_REP)
    s0, s1, s2 = sink_ref[SWA_REP * g], sink_ref[SWA_REP * g + 1], sink_ref[SWA_REP * g + 2]
    return jnp.where(r == 0, s0, jnp.where(r == 1, s1, s2))


def _stack_heads(x, n):
    return jnp.concatenate([x[:, h * HEAD_DIM:(h + 1) * HEAD_DIM] for h in range(n)], axis=0)


def _unstack_heads(x, n):
    rows = x.shape[0] // n
    return jnp.concatenate([x[h * rows:(h + 1) * rows] for h in range(n)], axis=1)


def _with_ones(v):
    return jnp.concatenate([v, jnp.ones_like(v)], axis=1)


def _swa_latent_kernel(sink_ref, q_ref, kp_ref, km_ref, kn_ref, vp_ref, vm_ref, vn_ref,
                       kx_ref, vx_ref, o_ref):
    n = pl.program_id(1)
    rows = SWA_REP * BLOCK
    qi = lax.broadcasted_iota(jnp.int32, (rows, 3 * BLOCK), 0) % BLOCK
    kj = lax.broadcasted_iota(jnp.int32, (rows, 3 * BLOCK), 1)
    in_window = (kj >= qi) & (kj <= qi + 2 * BLOCK)
    kv = []
    for g in range(SWA_KV_HEADS):
        gs = slice(g * HEAD_DIM, (g + 1) * HEAD_DIM)
        kspan = jnp.concatenate([kp_ref[:, gs], km_ref[:, gs], kn_ref[:, gs]], axis=0)
        vspan = _with_ones(jnp.concatenate([vp_ref[:, gs], vm_ref[:, gs], vn_ref[:, gs]], axis=0))
        kv.append((kspan, vspan, kx_ref[:, gs], _with_ones(vx_ref[:, gs]), _sink_rows(sink_ref, g, rows)))

    def scores(g, j):
        kspan, _, kx, _, _ = kv[g]
        qs = slice(g * SWA_REP * HEAD_DIM, (g + 1) * SWA_REP * HEAD_DIM)
        q = _stack_heads(q_ref[j * BLOCK:(j + 1) * BLOCK, qs], SWA_REP)
        return _dot_nt(q, kx), _dot_nt(q, kspan[j * BLOCK:(j + 3) * BLOCK])

    work = [(g, j) for g in range(SWA_KV_HEADS) for j in range(SWA_QB)]
    s_next = scores(*work[0])
    for idx, (g, j) in enumerate(work):
        s_ctx, s_band = s_next
        if idx + 1 < len(work):
            s_next = scores(*work[idx + 1])
        _, vspan, _, vx, sk = kv[g]
        qs = slice(g * SWA_REP * HEAD_DIM, (g + 1) * SWA_REP * HEAD_DIM)
        kpos = (n * SWA_QB + j - 1) * BLOCK + kj
        valid = in_window & (kpos >= 0) & (kpos < SEQ)
        s_band = jnp.where(valid, s_band, NEG_INF)
        m = jnp.maximum(jnp.maximum(jnp.max(s_ctx, axis=-1, keepdims=True),
                                    jnp.max(s_band, axis=-1, keepdims=True)), sk)
        p_ctx = jnp.exp(s_ctx - m).astype(BF16)
        p_band = jnp.exp(s_band - m).astype(BF16)
        o = _dot(p_ctx, vx) + _dot(p_band, vspan[j * BLOCK:(j + 3) * BLOCK])
        den = o[:, HEAD_DIM:] + jnp.exp(sk - m)
        o_ref[j * BLOCK:(j + 1) * BLOCK, qs] = _unstack_heads(
            o[:, :HEAD_DIM] / den, SWA_REP).astype(o_ref.dtype)


def _swa_latent(qk, vs, qk_c, vs_c, sink):
    nb = SEQ // BLOCK
    ng = nb // SWA_QB
    span = SWA_QB * BLOCK
    kvw = SWA_KV_HEADS * HEAD_DIM
    kcol = SWA_HEADS * HEAD_DIM // kvw
    vcol = 0
    kxcol = kcol

    def edge(col, blk_of):
        return pl.BlockSpec((BLOCK, kvw), lambda b, n, s: (b * nb + jnp.clip(blk_of(n), 0, nb - 1), col))

    def main(col):
        return pl.BlockSpec((span, kvw), lambda b, n, s: (b * ng + n, col))

    prev_blk = lambda n: n * SWA_QB - 1
    next_blk = lambda n: (n + 1) * SWA_QB
    grid_spec = pltpu.PrefetchScalarGridSpec(
        num_scalar_prefetch=1,
        grid=(BATCH, ng),
        in_specs=[pl.BlockSpec((span, SWA_HEADS * HEAD_DIM), lambda b, n, s: (b * ng + n, 0)),
                  edge(kcol, prev_blk), main(kcol), edge(kcol, next_blk),
                  edge(vcol, prev_blk), main(vcol), edge(vcol, next_blk),
                  pl.BlockSpec((CTX_LEN, kvw), lambda b, n, s: (b, kxcol)),
                  pl.BlockSpec((CTX_LEN, kvw), lambda b, n, s: (b, vcol))],
        out_specs=pl.BlockSpec((span, SWA_HEADS * HEAD_DIM), lambda b, n, s: (b * ng + n, 0)),
    )
    return pl.pallas_call(
        _swa_latent_kernel,
        grid_spec=grid_spec,
        out_shape=jax.ShapeDtypeStruct((BATCH * SEQ, SWA_HEADS * HEAD_DIM), BF16),
        compiler_params=_cparams("arbitrary", "arbitrary"),
        name="swa_latent",
    )(sink, qk, qk, qk, qk, vs, vs, vs, qk_c, vs_c)


def _swa_context_kernel(sink_ref, q_ref, k_ref, v_ref, o_ref):
    g = pl.program_id(1)
    q = _stack_heads(q_ref[...], SWA_REP)
    s = _dot_nt(q, k_ref[...])
    sk = _sink_rows(sink_ref, g, SWA_REP * CTX_LEN)
    m = jnp.maximum(jnp.max(s, axis=-1, keepdims=True), sk)
    p = jnp.exp(s - m)
    den = jnp.sum(p, axis=-1, keepdims=True) + jnp.exp(sk - m)
    o = _dot(p.astype(BF16), v_ref[...]) / den
    o_ref[...] = _unstack_heads(o, SWA_REP).astype(o_ref.dtype)


def _swa_context(qk_c, vs_c, sink):
    qw = SWA_REP * HEAD_DIM
    grid_spec = pltpu.PrefetchScalarGridSpec(
        num_scalar_prefetch=1,
        grid=(BATCH, SWA_KV_HEADS),
        in_specs=[pl.BlockSpec((CTX_LEN, qw), lambda b, g, s: (b, g)),
                  pl.BlockSpec((CTX_LEN, HEAD_DIM), lambda b, g, s: (b, SWA_HEADS + g)),
                  pl.BlockSpec((CTX_LEN, HEAD_DIM), lambda b, g, s: (b, g))],
        out_specs=pl.BlockSpec((CTX_LEN, qw), lambda b, g, s: (b, g)),
    )
    return pl.pallas_call(
        _swa_context_kernel,
        grid_spec=grid_spec,
        out_shape=jax.ShapeDtypeStruct((BATCH * CTX_LEN, SWA_HEADS * HEAD_DIM), BF16),
        compiler_params=_cparams("arbitrary", "arbitrary"),
        name="swa_context",
    )(sink, qk_c, qk_c, vs_c)


def _mla_attn_kernel(*refs, nseg):
    q_ref = refs[0]
    k_refs = refs[1:1 + nseg]
    v_refs = refs[1 + nseg:1 + 2 * nseg]
    o_ref = refs[1 + 2 * nseg]
    def scores(h):
        q = q_ref[:, h * MLA_QK_PAD:(h + 1) * MLA_QK_PAD]
        return [_dot_nt(q, k[:, h * MLA_QK_PAD:(h + 1) * MLA_QK_PAD]) for k in k_refs]

    s_next = scores(0)
    for h in range(MLA_HEADS):
        s = s_next
        if h + 1 < MLA_HEADS:
            s_next = scores(h + 1)
        m = functools.reduce(jnp.maximum, [jnp.max(x, axis=-1, keepdims=True) for x in s])
        p = [jnp.exp(x - m).astype(BF16) for x in s]
        o = sum(_dot(x, v[:, h * MLA_V_PAD:(h + 1) * MLA_V_PAD]) for x, v in zip(p, v_refs))
        o_ref[:, h * MLA_V:(h + 1) * MLA_V] = (o[:, :MLA_V] / o[:, MLA_V:]).astype(o_ref.dtype)


def _mla_attn(q, ks, vs, n_q, tq):
    nseg = len(ks)
    nq = n_q // tq
    lens = [k.shape[0] // BATCH for k in ks]
    kw, vw = MLA_HEADS * MLA_QK_PAD, MLA_HEADS * MLA_V
    in_specs = [pl.BlockSpec((tq, kw), lambda b, i: (b * nq + i, 0))]
    in_specs += [pl.BlockSpec((n, kw), lambda b, i: (b, 0)) for n in lens]
    in_specs += [pl.BlockSpec((n, MLA_HEADS * MLA_V_PAD), lambda b, i: (b, 0)) for n in lens]
    return pl.pallas_call(
        functools.partial(_mla_attn_kernel, nseg=nseg),
        grid=(BATCH, nq),
        in_specs=in_specs,
        out_specs=pl.BlockSpec((tq, vw), lambda b, i: (b * nq + i, 0)),
        out_shape=jax.ShapeDtypeStruct((BATCH * n_q, vw), BF16),
        compiler_params=_cparams("arbitrary", "arbitrary"),
        name="mla_attn",
    )(q, *ks, *vs)


def _out_proj_kernel(u_ref, yf_ref, yb_ref, d_ref, gw_ref, gb_ref, a2_ref, a3_ref, w_ref, x_ref, g_ref, gate_ref,
                     fg_ref, fshift_ref, fscale_ref, o_ref, f_ref):
    nc = 512
    k1 = S5_WIDTH
    k2 = k1 + a2_ref.shape[1]
    for rs in _sub_tiles(o_ref.shape[0]):
        y = d_ref[...] * u_ref[0, rs, :] + yf_ref[0, rs, :] + yb_ref[0, rs, :]
        gl = jax.nn.gelu(y)
        a1 = (gl * jax.nn.sigmoid(_dot(gl.astype(BF16), gw_ref[...]) + gb_ref[...])).astype(BF16)
        for n in range(D_MODEL // nc):
            cs = slice(n * nc, (n + 1) * nc)
            o_ref[rs, cs] = (_dot(a2_ref[rs, :], w_ref[k1:k2, cs]) + _dot(a3_ref[rs, :], w_ref[k2:D_MODEL, cs])
                             + _dot(a1, w_ref[0:k1, cs]))
        xn = x_ref[rs, :] + gate_ref[0] * _rms(o_ref[rs, :], g_ref[...])
        o_ref[rs, :] = xn
        f_ref[rs, :] = (_rms(xn, fg_ref[...]) * (1.0 + fscale_ref[0]) + fshift_ref[0]).astype(BF16)


def _out_proj(u3, yf, yb, y_off, d, glu_w, glu_b, a2, a3, w, x2d, g, gate, fg, fshift, fscale, layer, tm, mod_row):
    rows = a2.shape[0]
    per_b = u3.shape[1] // tm
    once = pl.Buffered(1)
    uspec = pl.BlockSpec((1, tm, S5_WIDTH), lambda i: (i // per_b, i % per_b, 0))
    yspec = pl.BlockSpec((1, tm, S5_WIDTH), lambda i: (i // per_b, i % per_b + y_off // tm, 0))
    vec5 = pl.BlockSpec((1, S5_WIDTH), lambda i: (0, 0))
    vec = pl.BlockSpec((1, D_MODEL), lambda i: (0, 0))
    mod = pl.BlockSpec((1, 1, D_MODEL), lambda i: (mod_row(i), 0, 0))
    row = pl.BlockSpec((tm, D_MODEL), lambda i: (i, 0))
    return pl.pallas_call(
        _out_proj_kernel,
        grid=(rows // tm,),
        in_specs=[uspec, yspec, yspec, vec5,
                  pl.BlockSpec((None, S5_WIDTH, S5_WIDTH), lambda i: (layer, 0, 0), pipeline_mode=once), vec5,
                  pl.BlockSpec((tm, a2.shape[1]), lambda i: (i, 0)),
                  pl.BlockSpec((tm, a3.shape[1]), lambda i: (i, 0)),
                  pl.BlockSpec((None, D_MODEL, D_MODEL), lambda i: (layer, 0, 0), pipeline_mode=once),
                  row, vec, mod, vec, mod, mod],
        out_specs=[row, row],
        out_shape=[jax.ShapeDtypeStruct((rows, D_MODEL), F32), jax.ShapeDtypeStruct((rows, D_MODEL), BF16)],
        compiler_params=_cparams("arbitrary"),
        name="out_proj",
    )(u3, yf, yb, d.reshape(1, -1), glu_w, glu_b.reshape(1, -1), a2, a3, w, x2d, g.reshape(1, -1), gate,
      fg.reshape(1, -1), fshift, fscale)


def _ffn_kernel(x_ref, f_ref, w1_ref, w2_ref, gpost_ref, gate_ref, o_ref, h_ref):
    j = pl.program_id(1)

    @pl.when(j == 0)
    def _():
        o_ref[...] = jnp.zeros_like(o_ref)

    nc = 512
    last = j == pl.num_programs(1) - 1

    def accumulate(rs):
        h = _dot(f_ref[rs, :], w1_ref[...].astype(BF16))
        h_ref[rs, :] = jnp.square(jnp.maximum(h, 0.0)).astype(BF16)
        for n in range(D_MODEL // nc):
            cs = slice(n * nc, (n + 1) * nc)
            o_ref[rs, cs] += _dot(h_ref[rs, :], w2_ref[:, cs].astype(BF16))

    @pl.when(jnp.logical_not(last))
    def _():
        accumulate(slice(None))

    @pl.when(last)
    def _():
        half = o_ref.shape[0] // 2
        for rs in (slice(0, half), slice(half, 2 * half)):
            accumulate(rs)
            o_ref[rs, :] = x_ref[rs, :] + gate_ref[0] * _rms(o_ref[rs, :], gpost_ref[...])


def _ffn(x2d, f2d, w1, w2, gpost, gate, layer, tm, mod_row):
    rows = x2d.shape[0]
    tf = TF_FFN
    mod = pl.BlockSpec((1, 1, D_MODEL), lambda i, j: (mod_row(i), 0, 0))
    vec = pl.BlockSpec((1, D_MODEL), lambda i, j: (0, 0))
    once = pl.Buffered(1)
    return pl.pallas_call(
        _ffn_kernel,
        grid=(rows // tm, D_FF // tf),
        in_specs=[pl.BlockSpec((tm, D_MODEL), lambda i, j: (i, 0)),
                  pl.BlockSpec((tm, D_MODEL), lambda i, j: (i, 0), pipeline_mode=once),
                  pl.BlockSpec((None, D_MODEL, tf), lambda i, j: (layer, 0, j)),
                  pl.BlockSpec((None, tf, D_MODEL), lambda i, j: (layer, j, 0)),
                  vec, mod],
        out_specs=pl.BlockSpec((tm, D_MODEL), lambda i, j: (i, 0), pipeline_mode=once),
        out_shape=jax.ShapeDtypeStruct((rows, D_MODEL), F32),
        scratch_shapes=[pltpu.VMEM((tm, tf), BF16)],
        compiler_params=_cparams("arbitrary", "arbitrary"),
        name="ffn",
    )(x2d, f2d, w1, w2, gpost.reshape(1, -1), gate)


def _rope_tables():
    t = np.arange(SEQ)
    row = (t // GRID_W).astype(np.float32)[:, None]
    col = (t % GRID_W).astype(np.float32)[:, None]

    def tables(rot_dim):
        quarter = rot_dim // 4
        inv_freq = np.float32(ROPE_BASE) ** (-np.arange(quarter, dtype=np.float32) / np.float32(quarter))
        ar, ac = row * inv_freq, col * inv_freq
        cos = np.concatenate([np.cos(ar), np.cos(ar), np.cos(ac), np.cos(ac)], axis=1)
        sin = np.concatenate([-np.sin(ar), np.sin(ar), -np.sin(ac), np.sin(ac)], axis=1)
        pad = LANE - rot_dim
        if pad:
            cos = np.concatenate([cos, np.ones((SEQ, pad), np.float32)], axis=1)
            sin = np.concatenate([sin, np.zeros((SEQ, pad), np.float32)], axis=1)
        return jnp.asarray(cos, F32), jnp.asarray(sin, F32)

    return tables(HEAD_DIM), tables(MLA_ROPE)


def kernel(x, c, ctx, c_ctx, ada_w, ada_b, norm_mix_pre, norm_mix_post, norm_ffn_pre, norm_ffn_post, w_in, w_out, s5_a_re, s5_a_im, s5_log_dt, s5_b_re, s5_b_im, s5_c_re, s5_c_im, s5_d, s5_glu_w, s5_glu_b, swa_sink, mla_q_norm, mla_w_uq, mla_kv_norm, mla_w_ukv, ffn_w1, ffn_w2):
    (cos_swa, sin_swa), (cos_mla, sin_mla) = _rope_tables()
    tabs = (cos_swa, sin_swa, cos_mla, sin_mla)

    cvec = jnp.concatenate([c, c_ctx[None, :], jnp.zeros((3, D_MODEL), F32)], axis=0)
    mods = _ada(cvec, ada_w, ada_b)

    tm_p, tm_f = TM_PROJ, TM_FFN

    def lat_row(tm):
        return lambda i: i // (SEQ // tm)

    ctx_row = lambda i: 4

    xl = x.reshape(BATCH * SEQ, D_MODEL)
    xc = ctx.reshape(BATCH * CTX_LEN, D_MODEL)

    w_in_pad = jnp.pad(w_in, ((0, 0), (0, 0), (0, IN_COLS - IN_WIDTH))).astype(BF16)
    w_out_b = w_out.astype(BF16)
    glu_w_b = s5_glu_w.astype(BF16)
    w_uq_pad = jnp.pad(mla_w_uq.reshape(DEPTH, MLA_Q_RANK, MLA_HEADS, MLA_NOPE + MLA_ROPE),
                       ((0, 0), (0, 0), (0, 0), (0, MLA_QK_PAD - MLA_NOPE - MLA_ROPE))
                       ).reshape(DEPTH, MLA_Q_RANK, MLA_HEADS * MLA_QK_PAD).astype(BF16)
    w_ukv4 = mla_w_ukv.reshape(DEPTH, MLA_KV_RANK, MLA_HEADS, MLA_NOPE + MLA_V)
    w_ukv_perm = jnp.concatenate([w_ukv4[..., :MLA_NOPE].reshape(DEPTH, MLA_KV_RANK, -1),
                                  w_ukv4[..., MLA_NOPE:].reshape(DEPTH, MLA_KV_RANK, -1)], axis=2).astype(BF16)
    s5_w = _s5_disc(s5_a_re, s5_a_im, s5_log_dt, s5_b_re, s5_b_im, s5_c_re, s5_c_im)

    for i in range(DEPTH):
        need_ctx = i < DEPTH - 1
        mod = [mods[i, :, k * D_MODEL:(k + 1) * D_MODEL].reshape(8, 1, D_MODEL) for k in range(6)]
        g_pre = norm_mix_pre[i].reshape(1, -1)

        u_l, qk_l, vs_l, q_l, k_l, v_l = _mix_in(
            xl, mod[0], mod[1], g_pre, w_in_pad, w_uq_pad, w_ukv_perm, mla_q_norm[i], mla_kv_norm[i],
            tabs, i, tm_p, lat_row(tm_p), True, SEQ)
        u_c, qk_c, vs_c, q_c, k_c, v_c = _mix_in(
            xc, mod[0], mod[1], g_pre, w_in_pad, w_uq_pad, w_ukv_perm, mla_q_norm[i], mla_kv_norm[i],
            tabs, i, tm_p, ctx_row, False, tm_p)

        yf, yb = _s5_scan(u_c, u_l, *s5_w, i, 0 if need_ctx else S5_CTX_CHUNKS)
        u3_l = u_l.reshape(BATCH, SEQ, S5_WIDTH)
        u3_c = u_c.reshape(BATCH, CTX_LEN, S5_WIDTH)

        swa_l = _swa_latent(qk_l, vs_l, qk_c, vs_c, swa_sink[i])

        mla_l = _mla_attn(q_l, [k_c, k_l], [v_c, v_l], SEQ, TQ_MLA)

        xl, fl = _out_proj(u3_l, yf, yb, 0, s5_d[i], glu_w_b, s5_glu_b[i], swa_l, mla_l, w_out_b, xl,
                           norm_mix_post[i], mod[2], norm_ffn_pre[i], mod[3], mod[4], i, tm_p, lat_row(tm_p))
        xl = _ffn(xl, fl, ffn_w1, ffn_w2, norm_ffn_post[i], mod[5], i, tm_f, lat_row(tm_f))

        if need_ctx:
            swa_c = _swa_context(qk_c, vs_c, swa_sink[i])
            mla_c = _mla_attn(q_c, [k_c], [v_c], CTX_LEN, TQ_MLA_CTX)
            xc, fc = _out_proj(u3_c, yf, yb, SEQ, s5_d[i], glu_w_b, s5_glu_b[i], swa_c, mla_c, w_out_b, xc,
                               norm_mix_post[i], mod[2], norm_ffn_pre[i], mod[3], mod[4], i, CTX_LEN, ctx_row)
            xc = _ffn(xc, fc, ffn_w1, ffn_w2, norm_ffn_post[i], mod[5], i, tm_f, ctx_row)

    return xl.reshape(BATCH, SEQ, D_MODEL)
```

```python
import functools

import jax
import jax.numpy as jnp
import numpy as np
from jax import lax
from jax.experimental import pallas as pl
from jax.experimental.pallas import tpu as pltpu

F32 = jnp.float32
BF16 = jnp.bfloat16

D_MODEL = 2048
BATCH = 4
SEQ = 2048
DEPTH = 2
GRID_W = 64
CTX_LEN = 256
EPS = 1e-6
ROPE_BASE = 10000.0
NEG_INF = -1e30
BLOCK = 128
HEAD_DIM = 128
S5_WIDTH = 512
S5_GROUP = 16
S5_GROUPS = 32
S5_STATE = 64
S5_LANES = S5_GROUPS * S5_STATE
SWA_HEADS = 6
SWA_KV_HEADS = 2
SWA_REP = SWA_HEADS // SWA_KV_HEADS
MLA_HEADS = 6
MLA_Q_RANK = 768
MLA_KV_RANK = 512
MLA_NOPE = 128
MLA_ROPE = 64
MLA_V = 128
MLA_QK_PAD = 256
MLA_V_PAD = 256
D_FF = 4 * D_MODEL
IN_WIDTHS = (512, 768, 256, 256, 768, 512, 64)
IN_WIDTH = sum(IN_WIDTHS)
OFF_U, OFF_QS, OFF_KS, OFF_VS, OFF_CQ, OFF_CKV, OFF_KR = 0, 512, 1280, 1536, 1792, 2560, 3072

LANE = 128
SUBLANE = 8
V7X_VMEM_BYTES = 64 * 1024 * 1024
VMEM_LIMIT = V7X_VMEM_BYTES * 7 // 8

TM_PROJ = 512
SUB_ROWS = 256
TM_FFN = 1024
TF_FFN = 512
TN_ADA = 1024
TQ_MLA = 512
TQ_MLA_CTX = CTX_LEN
S5_TC = 128
SWA_QB = 8


def _cparams(*sem):
    return pltpu.CompilerParams(dimension_semantics=sem, vmem_limit_bytes=VMEM_LIMIT)


def _dot(a, b):
    return jnp.dot(a, b, preferred_element_type=F32)


def _dot_nt(a, b):
    return lax.dot_general(a, b, (((1,), (1,)), ((), ())), preferred_element_type=F32)


def _rms(x, g):
    return x * lax.rsqrt(jnp.mean(x * x, axis=-1, keepdims=True) + EPS) * g


def _sub_tiles(rows):
    return [slice(r, r + SUB_ROWS) for r in range(0, rows, SUB_ROWS)]


def _rope(x, cos, sin, half):
    lane = lax.broadcasted_iota(jnp.int32, x.shape, 1)
    fwd = pltpu.roll(x, LANE - half, 1)
    bwd = pltpu.roll(x, half, 1)
    sw = jnp.where((lane % (2 * half)) < half, fwd, bwd)
    return x * cos + sw * sin


def _ada_kernel(c_ref, w_ref, b_ref, o_ref):
    c = c_ref[...]
    s = c * jax.nn.sigmoid(c)
    o_ref[0] = _dot(s.astype(BF16), w_ref[0].astype(BF16)) + b_ref[0]


def _ada(cvec, ada_w, ada_b):
    tn = TN_ADA
    n = 6 * D_MODEL
    return pl.pallas_call(
        _ada_kernel,
        grid=(DEPTH, n // tn),
        in_specs=[pl.BlockSpec((8, D_MODEL), lambda l, j: (0, 0)),
                  pl.BlockSpec((1, D_MODEL, tn), lambda l, j: (l, 0, j)),
                  pl.BlockSpec((1, 1, tn), lambda l, j: (l, 0, j))],
        out_specs=pl.BlockSpec((1, 8, tn), lambda l, j: (l, 0, j)),
        out_shape=jax.ShapeDtypeStruct((DEPTH, 8, n), F32),
        compiler_params=_cparams("arbitrary", "arbitrary"),
        name="ada",
    )(cvec, ada_w, ada_b.reshape(DEPTH, 1, n))


IN_COLS = 3200
SWA_QK_W = (SWA_HEADS + SWA_KV_HEADS) * HEAD_DIM


def _mix_in_kernel(x_ref, shift_ref, scale_ref, g_ref, w_ref, wq_ref, wkv_ref, qn_ref, kvn_ref,
                   cs_ref, ss_ref, cm_ref, sm_ref,
                   u_ref, qk_ref, vs_ref, q_ref, k_ref, v_ref, h_ref, *, rope):
    for rs in _sub_tiles(x_ref.shape[0]):
        h = _rms(x_ref[rs, :], g_ref[...]) * (1.0 + scale_ref[0]) + shift_ref[0]
        h_ref[rs, :] = h.astype(BF16)

        def proj(lo, width):
            return _dot(h_ref[rs, :], w_ref[:, lo:lo + width])

        u_ref[rs, :] = proj(OFF_U, S5_WIDTH)

        swa_scale = HEAD_DIM ** -0.5
        per = 4
        for c in range(SWA_QK_W // (per * HEAD_DIM)):
            zc = proj(OFF_QS + c * per * HEAD_DIM, per * HEAD_DIM)
            for hh in range(per):
                head = c * per + hh
                xh = zc[:, hh * HEAD_DIM:(hh + 1) * HEAD_DIM]
                if rope:
                    xh = _rope(xh, cs_ref[rs, :], ss_ref[rs, :], HEAD_DIM // 4)
                if head < SWA_HEADS:
                    xh = xh * swa_scale
                qk_ref[rs, head * HEAD_DIM:(head + 1) * HEAD_DIM] = xh.astype(BF16)

        vs_ref[rs, :] = proj(OFF_VS, SWA_KV_HEADS * HEAD_DIM).astype(BF16)

        mla_scale = (MLA_NOPE + MLA_ROPE) ** -0.5
        cq = proj(OFF_CQ, MLA_Q_RANK)
        q = _dot(_rms(cq, qn_ref[...]).astype(BF16), wq_ref[...])
        for hd in range(MLA_HEADS):
            lo = hd * MLA_QK_PAD
            q_ref[rs, lo:lo + MLA_NOPE] = (q[:, lo:lo + MLA_NOPE] * mla_scale).astype(BF16)
            r = q[:, lo + MLA_NOPE:lo + MLA_QK_PAD]
            if rope:
                r = _rope(r, cm_ref[rs, :], sm_ref[rs, :], MLA_ROPE // 4)
            q_ref[rs, lo + MLA_NOPE:lo + MLA_QK_PAD] = (r * mla_scale).astype(BF16)

        ckv = proj(OFF_CKV, MLA_KV_RANK)
        kv = _dot(_rms(ckv, kvn_ref[...]).astype(BF16), wkv_ref[...])
        kr = proj(OFF_KR, LANE)
        if rope:
            kr = _rope(kr, cm_ref[rs, :], sm_ref[rs, :], MLA_ROPE // 4)
        kr = kr.astype(BF16)
        for hd in range(MLA_HEADS):
            lo = hd * MLA_QK_PAD
            k_ref[rs, lo:lo + MLA_NOPE] = kv[:, hd * MLA_NOPE:(hd + 1) * MLA_NOPE].astype(BF16)
            k_ref[rs, lo + MLA_NOPE:lo + MLA_QK_PAD] = kr
            vlo = hd * MLA_V_PAD
            vh = kv[:, MLA_HEADS * MLA_NOPE + hd * MLA_V:MLA_HEADS * MLA_NOPE + (hd + 1) * MLA_V].astype(BF16)
            v_ref[rs, vlo:vlo + MLA_V] = vh
            v_ref[rs, vlo + MLA_V:vlo + MLA_V_PAD] = jnp.ones_like(vh)


def _mix_in(x2d, shift, scale, g, w_in, w_uq, w_ukv, q_norm, kv_norm, tabs, layer, tm, mod_row, rope, seq):
    rows = x2d.shape[0]
    once = pl.Buffered(1)
    tab = pl.BlockSpec((tm, LANE), lambda i: (i % (seq // tm), 0))

    def res(shape):
        return pl.BlockSpec((None,) + shape, lambda i: (layer, 0, 0), pipeline_mode=once)

    def out(width):
        return pl.BlockSpec((tm, width), lambda i: (i, 0))

    widths = (S5_WIDTH, SWA_QK_W, SWA_KV_HEADS * HEAD_DIM,
              MLA_HEADS * MLA_QK_PAD, MLA_HEADS * MLA_QK_PAD, MLA_HEADS * MLA_V_PAD)
    dtypes = (F32, BF16, BF16, BF16, BF16, BF16)
    return pl.pallas_call(
        functools.partial(_mix_in_kernel, rope=rope),
        grid=(rows // tm,),
        in_specs=[pl.BlockSpec((tm, D_MODEL), lambda i: (i, 0)),
                  pl.BlockSpec((1, 1, D_MODEL), lambda i: (mod_row(i), 0, 0)),
                  pl.BlockSpec((1, 1, D_MODEL), lambda i: (mod_row(i), 0, 0)),
                  pl.BlockSpec((1, D_MODEL), lambda i: (0, 0)),
                  res((D_MODEL, IN_COLS)),
                  res((MLA_Q_RANK, MLA_HEADS * MLA_QK_PAD)),
                  res((MLA_KV_RANK, MLA_HEADS * (MLA_NOPE + MLA_V))),
                  pl.BlockSpec((1, MLA_Q_RANK), lambda i: (0, 0)),
                  pl.BlockSpec((1, MLA_KV_RANK), lambda i: (0, 0)),
                  tab, tab, tab, tab],
        out_specs=[out(w) for w in widths],
        out_shape=[jax.ShapeDtypeStruct((rows, w), dt) for w, dt in zip(widths, dtypes)],
        scratch_shapes=[pltpu.VMEM((tm, D_MODEL), BF16)],
        compiler_params=_cparams("arbitrary"),
        name="mix_in",
    )(x2d, shift, scale, g, w_in, w_uq, w_ukv, q_norm.reshape(1, -1), kv_norm.reshape(1, -1), *tabs)


S5_GH = S5_GROUPS // 2


def _zoh(are, aim, ldt):
    dt = jnp.exp(ldt)
    mag = jnp.exp(are * dt)
    lr = mag * jnp.cos(aim * dt)
    li = mag * jnp.sin(aim * dt)
    den = are * are + aim * aim
    nr = lr - 1.0
    return lr, li, (nr * are + li * aim) / den, (li * are - nr * aim) / den


def _s5_disc_kernel(are_ref, aim_ref, ldt_ref, arec_ref, aimc_ref, ldtc_ref, br_ref, bi_ref, cr_ref, ci_ref,
                    lam_ref, wbr_ref, wbi_ref, wcr_ref, wci_ref):
    lr, li, _, _ = _zoh(are_ref[0], aim_ref[0], ldt_ref[0])
    lam_ref[0, 0:1, :] = lr
    lam_ref[0, 1:2, :] = li
    gh, n, p = S5_GH, S5_STATE, S5_GROUP
    kin, kst = gh * p, gh * n

    def iota(shape, axis):
        return lax.broadcasted_iota(jnp.int32, shape, axis)

    tile_n = jnp.where(iota((n, kst), 0) == (iota((n, kst), 1) & (n - 1)), 1.0, 0.0).astype(BF16)
    tile_p = jnp.where(iota((p, kin), 0) == (iota((p, kin), 1) & (p - 1)), 1.0, 0.0).astype(BF16)
    lp, ln = p.bit_length() - 1, n.bit_length() - 1
    diag_b = (iota((kin, kst), 0) >> lp) == (iota((kin, kst), 1) >> ln)
    diag_c = (iota((kst, kin), 0) >> ln) == (iota((kst, kin), 1) >> lp)
    for h in range(2):
        _, _, fr, fi = _zoh(arec_ref[0, h], aimc_ref[0, h], ldtc_ref[0, h])
        fr = jnp.broadcast_to(fr[:, None, :], (gh, p, n)).reshape(kin, n)
        fi = jnp.broadcast_to(fi[:, None, :], (gh, p, n)).reshape(kin, n)
        bbr = (fr * br_ref[0, h] - fi * bi_ref[0, h]).astype(BF16)
        bbi = (fr * bi_ref[0, h] + fi * br_ref[0, h]).astype(BF16)
        wbr_ref[0, h] = jnp.where(diag_b, _dot(bbr, tile_n), 0.0).astype(BF16)
        wbi_ref[0, h] = jnp.where(diag_b, _dot(bbi, tile_n), 0.0).astype(BF16)
        wcr_ref[0, h] = jnp.where(diag_c, _dot(cr_ref[0, h].astype(BF16), tile_p), 0.0).astype(BF16)
        wci_ref[0, h] = jnp.where(diag_c, _dot(ci_ref[0, h].astype(BF16), tile_p), 0.0).astype(BF16)


def _s5_disc(a_re, a_im, log_dt, b_re, b_im, c_re, c_im):
    nd = DEPTH * 2
    gh, n, p = S5_GH, S5_STATE, S5_GROUP
    kin, kst = gh * p, gh * n
    are = a_re.reshape(nd, 1, S5_LANES)
    aim = a_im.reshape(nd, 1, S5_LANES)
    ldt_gn = jnp.broadcast_to(log_dt[..., None], a_re.shape)
    ldt = ldt_gn.reshape(nd, 1, S5_LANES)
    compact = lambda a: a.reshape(nd, 2, gh, n)
    bt = lambda b: b.reshape(nd, 2, gh, n, p).transpose(0, 1, 2, 4, 3).reshape(nd, 2, kin, n)
    ct = lambda c: c.reshape(nd, 2, gh, p, n).transpose(0, 1, 2, 4, 3).reshape(nd, 2, kst, p)
    vec = pl.BlockSpec((1, 1, S5_LANES), lambda d: (d, 0, 0))
    cvec = pl.BlockSpec((1, 2, gh, n), lambda d: (d, 0, 0, 0))
    bin_spec = pl.BlockSpec((1, 2, kin, n), lambda d: (d, 0, 0, 0))
    cin_spec = pl.BlockSpec((1, 2, kst, p), lambda d: (d, 0, 0, 0))
    bspec = pl.BlockSpec((1, 2, kin, kst), lambda d: (d, 0, 0, 0))
    cspec = pl.BlockSpec((1, 2, kst, kin), lambda d: (d, 0, 0, 0))
    return pl.pallas_call(
        _s5_disc_kernel,
        grid=(nd,),
        in_specs=[vec, vec, vec, cvec, cvec, cvec, bin_spec, bin_spec, cin_spec, cin_spec],
        out_specs=[pl.BlockSpec((1, 2, S5_LANES), lambda d: (d, 0, 0)), bspec, bspec, cspec, cspec],
        out_shape=[jax.ShapeDtypeStruct((nd, 2, S5_LANES), F32),
                   jax.ShapeDtypeStruct((nd, 2, kin, kst), BF16),
                   jax.ShapeDtypeStruct((nd, 2, kin, kst), BF16),
                   jax.ShapeDtypeStruct((nd, 2, kst, kin), BF16),
                   jax.ShapeDtypeStruct((nd, 2, kst, kin), BF16)],
        compiler_params=_cparams("arbitrary"),
        name="s5_disc",
    )(are, aim, ldt, compact(a_re), compact(a_im), compact(ldt_gn),
      bt(b_re), bt(b_im), ct(c_re), ct(c_im))


S5_CTX_CHUNKS = CTX_LEN // S5_TC
S5_LAT_CHUNKS = SEQ // S5_TC


def _s5_scan_kernel(ucf_ref, ulf_ref, ucb_ref, ulb_ref, lam_ref, wbr_ref, wbi_ref, wcr_ref, wci_ref,
                    yf_ref, yb_ref, fr_ref, fi_ref, gr_ref, gi_ref, sr_ref, si_ref, ut_ref, yt_ref):
    c = pl.program_id(0)
    tc = S5_TC
    kin, kst = S5_WIDTH // 2, S5_LANES // 2
    nlb = kst // LANE

    @pl.when(c == 0)
    def _():
        sr_ref[...] = jnp.zeros_like(sr_ref)
        si_ref[...] = jnp.zeros_like(si_ref)

    in_ctx = c < S5_CTX_CHUNKS
    state = [(fr_ref, fi_ref), (gr_ref, gi_ref)]
    nub = S5_WIDTH // LANE

    u_dir = []
    for d, (uc_ref, ul_ref) in enumerate(((ucf_ref, ulf_ref), (ucb_ref, ulb_ref))):
        u = jnp.where(in_ctx, uc_ref[...], ul_ref[...])
        for j in range(nub):
            for b in range(BATCH):
                ut_ref[d, j, pl.ds(b, tc, stride=BATCH), :] = u[b, :, j * LANE:(j + 1) * LANE]
        u_dir.append(jnp.concatenate([ut_ref[d, j] for j in range(nub)], axis=1).astype(BF16))

    def project(d, h, part):
        w_ref = (wbr_ref, wbi_ref)[part]
        dst = state[d][part]
        p = _dot(u_dir[d][:, h * kin:(h + 1) * kin], w_ref[d, h])
        for j in range(nlb):
            dst[h * nlb + j] = p[:, j * LANE:(j + 1) * LANE]

    def scan_steps(h, k0, k1, carry):
        ls = slice(h * nlb, (h + 1) * nlb)
        fwd_rows = lax.broadcasted_iota(jnp.int32, (nlb, SUBLANE, LANE), 1) < BATCH
        la_r = jnp.where(fwd_rows, lam_ref[0, 0, ls], lam_ref[1, 0, ls])
        la_i = jnp.where(fwd_rows, lam_ref[0, 1, ls], lam_ref[1, 1, ls])
        lb_r = jnp.where(fwd_rows, lam_ref[1, 0, ls], lam_ref[0, 0, ls])
        lb_i = jnp.where(fwd_rows, lam_ref[1, 1, ls], lam_ref[0, 1, ls])
        s_r, s_i = carry
        for k in range(k0, k1):
            rf = slice(k * SUBLANE, (k + 1) * SUBLANE)
            kb = tc // 2 - 1 - k
            rb = slice(kb * SUBLANE, (kb + 1) * SUBLANE)
            f_r, f_i = fr_ref[ls, rf, :], fi_ref[ls, rf, :]
            g_r, g_i = gr_ref[ls, rb, :], gi_ref[ls, rb, :]
            a_r = la_r * s_r - la_i * s_i + jnp.where(fwd_rows, f_r, g_r)
            a_i = la_r * s_i + la_i * s_r + jnp.where(fwd_rows, f_i, g_i)
            t_r = pltpu.roll(a_r, BATCH, 1)
            t_i = pltpu.roll(a_i, BATCH, 1)
            b_r = lb_r * t_r - lb_i * t_i + jnp.where(fwd_rows, g_r, f_r)
            b_i = lb_r * t_i + lb_i * t_r + jnp.where(fwd_rows, g_i, f_i)
            fr_ref[ls, rf, :] = jnp.where(fwd_rows, a_r, b_r)
            fi_ref[ls, rf, :] = jnp.where(fwd_rows, a_i, b_i)
            gr_ref[ls, rb, :] = jnp.where(fwd_rows, b_r, a_r)
            gi_ref[ls, rb, :] = jnp.where(fwd_rows, b_i, a_i)
            s_r, s_i = pltpu.roll(b_r, BATCH, 1), pltpu.roll(b_i, BATCH, 1)
        return s_r, s_i

    def half_rows(src, h):
        return jnp.concatenate([src[h * nlb + j] for j in range(nlb)], axis=1).astype(BF16)

    def readout(d, h):
        y = _dot(half_rows(state[d][0], h), wcr_ref[d, h]) - _dot(half_rows(state[d][1], h), wci_ref[d, h])
        for j in range(kin // LANE):
            yt_ref[d, h * (kin // LANE) + j] = y[:, j * LANE:(j + 1) * LANE]

    def emit(d):
        y_ref = (yf_ref, yb_ref)[d]
        for b in range(BATCH):
            y_ref[b] = jnp.concatenate([yt_ref[d, j, pl.ds(b, tc, stride=BATCH), :] for j in range(nub)], axis=1)

    quarters = 4
    per = tc // 2 // quarters
    pieces = [(d, part) for d in range(2) for part in range(2)]
    for d, part in pieces:
        project(d, 0, part)
    carry = (sr_ref[0:nlb], si_ref[0:nlb])
    for q, (d, part) in enumerate(pieces):
        carry = scan_steps(0, q * per, (q + 1) * per, carry)
        project(d, 1, part)
    sr_ref[0:nlb], si_ref[0:nlb] = carry
    carry = (sr_ref[nlb:2 * nlb], si_ref[nlb:2 * nlb])
    for q in range(quarters):
        carry = scan_steps(1, q * per, (q + 1) * per, carry)
        if q % 2 == 1:
            readout(q // 2, 0)
    sr_ref[nlb:2 * nlb], si_ref[nlb:2 * nlb] = carry
    for d in range(2):
        readout(d, 1)
        emit(d)


def _s5_scan(u_c, u_l, lam, wbr, wbi, wcr, wci, layer, skip_chunks):
    tc = S5_TC
    nchunk = S5_CTX_CHUNKS + S5_LAT_CHUNKS
    kin, kst = S5_WIDTH // 2, S5_LANES // 2
    zc3 = u_c.reshape(BATCH, CTX_LEN, S5_WIDTH)
    zl3 = u_l.reshape(BATCH, SEQ, S5_WIDTH)

    def bwd_chunk(c):
        return jnp.where(c < S5_CTX_CHUNKS, S5_CTX_CHUNKS - 1 - c, nchunk + S5_CTX_CHUNKS - 1 - c)

    def ctx_spec(chunk_of):
        return pl.BlockSpec((BATCH, tc, S5_WIDTH),
                            lambda c: (0, jnp.clip(chunk_of(c), 0, S5_CTX_CHUNKS - 1), 0))

    def lat_spec(chunk_of):
        return pl.BlockSpec((BATCH, tc, S5_WIDTH),
                            lambda c: (0, jnp.clip(chunk_of(c) - S5_CTX_CHUNKS, 0, S5_LAT_CHUNKS - 1), 0))

    def out_block(chunk):
        return jnp.where(chunk >= S5_CTX_CHUNKS, chunk - S5_CTX_CHUNKS, S5_LAT_CHUNKS + chunk)

    def out_spec(chunk_of):
        return pl.BlockSpec((BATCH, tc, S5_WIDTH),
                            lambda c: (0, out_block(chunk_of(jnp.maximum(c, skip_chunks))), 0))

    once = pl.Buffered(1)
    bspec = pl.BlockSpec((2, 2, kin, kst), lambda c: (layer, 0, 0, 0), pipeline_mode=once)
    cspec = pl.BlockSpec((2, 2, kst, kin), lambda c: (layer, 0, 0, 0), pipeline_mode=once)
    t_out = (nchunk - skip_chunks) * tc
    nblk = S5_LANES // LANE
    big = pltpu.VMEM((nblk, tc * BATCH, LANE), F32)
    state = pltpu.VMEM((nblk, SUBLANE, LANE), F32)
    narrow = pltpu.VMEM((2, S5_WIDTH // LANE, tc * BATCH, LANE), F32)
    lam = lam.reshape(DEPTH * 2, 2, nblk, 1, LANE)
    fwd_chunk = lambda c: c
    return pl.pallas_call(
        _s5_scan_kernel,
        grid=(nchunk,),
        in_specs=[ctx_spec(fwd_chunk), lat_spec(fwd_chunk), ctx_spec(bwd_chunk), lat_spec(bwd_chunk),
                  pl.BlockSpec((2, 2, nblk, 1, LANE), lambda c: (layer, 0, 0, 0, 0)),
                  bspec, bspec, cspec, cspec],
        out_specs=[out_spec(fwd_chunk), out_spec(bwd_chunk)],
        out_shape=[jax.ShapeDtypeStruct((BATCH, t_out, S5_WIDTH), F32)] * 2,
        scratch_shapes=[big, big, big, big, state, state, narrow, narrow],
        compiler_params=_cparams("arbitrary"),
        name="s5_scan",
    )(zc3, zl3, zc3, zl3, lam, wbr, wbi, wcr, wci)


def _sink_rows(sink_ref, g, rows):
    r = lax.broadcasted_iota(jnp.int32, (rows, 1), 0) // (rows // SWA_REP)
    s0, s1, s2 = sink_ref[SWA_REP * g], sink_ref[SWA_REP * g + 1], sink_ref[SWA_REP * g + 2]
    return jnp.where(r == 0, s0, jnp.where(r == 1, s1, s2))


def _stack_heads(x, n):
    return jnp.concatenate([x[:, h * HEAD_DIM:(h + 1) * HEAD_DIM] for h in range(n)], axis=0)


def _unstack_heads(x, n):
    rows = x.shape[0] // n
    return jnp.concatenate([x[h * rows:(h + 1) * rows] for h in range(n)], axis=1)


def _with_ones(v):
    return jnp.concatenate([v, jnp.ones_like(v)], axis=1)


def _swa_latent_kernel(sink_ref, q_ref, kp_ref, km_ref, kn_ref, vp_ref, vm_ref, vn_ref,
                       kx_ref, vx_ref, o_ref):
    n = pl.program_id(1)
    rows = SWA_REP * BLOCK
    qi = lax.broadcasted_iota(jnp.int32, (rows, 3 * BLOCK), 0) % BLOCK
    kj = lax.broadcasted_iota(jnp.int32, (rows, 3 * BLOCK), 1)
    in_window = (kj >= qi) & (kj <= qi + 2 * BLOCK)
    kv = []
    for g in range(SWA_KV_HEADS):
        gs = slice(g * HEAD_DIM, (g + 1) * HEAD_DIM)
        kspan = jnp.concatenate([kp_ref[:, gs], km_ref[:, gs], kn_ref[:, gs]], axis=0)
        vspan = _with_ones(jnp.concatenate([vp_ref[:, gs], vm_ref[:, gs], vn_ref[:, gs]], axis=0))
        kv.append((kspan, vspan, kx_ref[:, gs], _with_ones(vx_ref[:, gs]), _sink_rows(sink_ref, g, rows)))

    def scores(g, j):
        kspan, _, kx, _, _ = kv[g]
        qs = slice(g * SWA_REP * HEAD_DIM, (g + 1) * SWA_REP * HEAD_DIM)
        q = _stack_heads(q_ref[j * BLOCK:(j + 1) * BLOCK, qs], SWA_REP)
        return _dot_nt(q, kx), _dot_nt(q, kspan[j * BLOCK:(j + 3) * BLOCK])

    work = [(g, j) for g in range(SWA_KV_HEADS) for j in range(SWA_QB)]
    s_next = scores(*work[0])
    for idx, (g, j) in enumerate(work):
        s_ctx, s_band = s_next
        if idx + 1 < len(work):
            s_next = scores(*work[idx + 1])
        _, vspan, _, vx, sk = kv[g]
        qs = slice(g * SWA_REP * HEAD_DIM, (g + 1) * SWA_REP * HEAD_DIM)
        kpos = (n * SWA_QB + j - 1) * BLOCK + kj
        valid = in_window & (kpos >= 0) & (kpos < SEQ)
        s_band = jnp.where(valid, s_band, NEG_INF)
        m = jnp.maximum(jnp.maximum(jnp.max(s_ctx, axis=-1, keepdims=True),
                                    jnp.max(s_band, axis=-1, keepdims=True)), sk)
        p_ctx = jnp.exp(s_ctx - m).astype(BF16)
        p_band = jnp.exp(s_band - m).astype(BF16)
        o = _dot(p_ctx, vx) + _dot(p_band, vspan[j * BLOCK:(j + 3) * BLOCK])
        den = o[:, HEAD_DIM:] + jnp.exp(sk - m)
        o_ref[j * BLOCK:(j + 1) * BLOCK, qs] = _unstack_heads(
            o[:, :HEAD_DIM] / den, SWA_REP).astype(o_ref.dtype)


def _swa_latent(qk, vs, qk_c, vs_c, sink):
    nb = SEQ // BLOCK
    ng = nb // SWA_QB
    span = SWA_QB * BLOCK
    kvw = SWA_KV_HEADS * HEAD_DIM
    kcol = SWA_HEADS * HEAD_DIM // kvw
    vcol = 0
    kxcol = kcol

    def edge(col, blk_of):
        return pl.BlockSpec((BLOCK, kvw), lambda b, n, s: (b * nb + jnp.clip(blk_of(n), 0, nb - 1), col))

    def main(col):
        return pl.BlockSpec((span, kvw), lambda b, n, s: (b * ng + n, col))

    prev_blk = lambda n: n * SWA_QB - 1
    next_blk = lambda n: (n + 1) * SWA_QB
    grid_spec = pltpu.PrefetchScalarGridSpec(
        num_scalar_prefetch=1,
        grid=(BATCH, ng),
        in_specs=[pl.BlockSpec((span, SWA_HEADS * HEAD_DIM), lambda b, n, s: (b * ng + n, 0)),
                  edge(kcol, prev_blk), main(kcol), edge(kcol, next_blk),
                  edge(vcol, prev_blk), main(vcol), edge(vcol, next_blk),
                  pl.BlockSpec((CTX_LEN, kvw), lambda b, n, s: (b, kxcol)),
                  pl.BlockSpec((CTX_LEN, kvw), lambda b, n, s: (b, vcol))],
        out_specs=pl.BlockSpec((span, SWA_HEADS * HEAD_DIM), lambda b, n, s: (b * ng + n, 0)),
    )
    return pl.pallas_call(
        _swa_latent_kernel,
        grid_spec=grid_spec,
        out_shape=jax.ShapeDtypeStruct((BATCH * SEQ, SWA_HEADS * HEAD_DIM), BF16),
        compiler_params=_cparams("arbitrary", "arbitrary"),
        name="swa_latent",
    )(sink, qk, qk, qk, qk, vs, vs, vs, qk_c, vs_c)


def _swa_context_kernel(sink_ref, q_ref, k_ref, v_ref, o_ref):
    g = pl.program_id(1)
    q = _stack_heads(q_ref[...], SWA_REP)
    s = _dot_nt(q, k_ref[...])
    sk = _sink_rows(sink_ref, g, SWA_REP * CTX_LEN)
    m = jnp.maximum(jnp.max(s, axis=-1, keepdims=True), sk)
    p = jnp.exp(s - m)
    den = jnp.sum(p, axis=-1, keepdims=True) + jnp.exp(sk - m)
    o = _dot(p.astype(BF16), v_ref[...]) / den
    o_ref[...] = _unstack_heads(o, SWA_REP).astype(o_ref.dtype)


def _swa_context(qk_c, vs_c, sink):
    qw = SWA_REP * HEAD_DIM
    grid_spec = pltpu.PrefetchScalarGridSpec(
        num_scalar_prefetch=1,
        grid=(BATCH, SWA_KV_HEADS),
        in_specs=[pl.BlockSpec((CTX_LEN, qw), lambda b, g, s: (b, g)),
                  pl.BlockSpec((CTX_LEN, HEAD_DIM), lambda b, g, s: (b, SWA_HEADS + g)),
                  pl.BlockSpec((CTX_LEN, HEAD_DIM), lambda b, g, s: (b, g))],
        out_specs=pl.BlockSpec((CTX_LEN, qw), lambda b, g, s: (b, g)),
    )
    return pl.pallas_call(
        _swa_context_kernel,
        grid_spec=grid_spec,
        out_shape=jax.ShapeDtypeStruct((BATCH * CTX_LEN, SWA_HEADS * HEAD_DIM), BF16),
        compiler_params=_cparams("arbitrary", "arbitrary"),
        name="swa_context",
    )(sink, qk_c, qk_c, vs_c)


def _mla_attn_kernel(*refs, nseg):
    q_ref = refs[0]
    k_refs = refs[1:1 + nseg]
    v_refs = refs[1 + nseg:1 + 2 * nseg]
    o_ref = refs[1 + 2 * nseg]
    def scores(h):
        q = q_ref[:, h * MLA_QK_PAD:(h + 1) * MLA_QK_PAD]
        return [_dot_nt(q, k[:, h * MLA_QK_PAD:(h + 1) * MLA_QK_PAD]) for k in k_refs]

    s_next = scores(0)
    for h in range(MLA_HEADS):
        s = s_next
        if h + 1 < MLA_HEADS:
            s_next = scores(h + 1)
        m = functools.reduce(jnp.maximum, [jnp.max(x, axis=-1, keepdims=True) for x in s])
        p = [jnp.exp(x - m).astype(BF16) for x in s]
        o = sum(_dot(x, v[:, h * MLA_V_PAD:(h + 1) * MLA_V_PAD]) for x, v in zip(p, v_refs))
        o_ref[:, h * MLA_V:(h + 1) * MLA_V] = (o[:, :MLA_V] / o[:, MLA_V:]).astype(o_ref.dtype)


def _mla_attn(q, ks, vs, n_q, tq):
    nseg = len(ks)
    nq = n_q // tq
    lens = [k.shape[0] // BATCH for k in ks]
    kw, vw = MLA_HEADS * MLA_QK_PAD, MLA_HEADS * MLA_V
    in_specs = [pl.BlockSpec((tq, kw), lambda b, i: (b * nq + i, 0))]
    in_specs += [pl.BlockSpec((n, kw), lambda b, i: (b, 0)) for n in lens]
    in_specs += [pl.BlockSpec((n, MLA_HEADS * MLA_V_PAD), lambda b, i: (b, 0)) for n in lens]
    return pl.pallas_call(
        functools.partial(_mla_attn_kernel, nseg=nseg),
        grid=(BATCH, nq),
        in_specs=in_specs,
        out_specs=pl.BlockSpec((tq, vw), lambda b, i: (b * nq + i, 0)),
        out_shape=jax.ShapeDtypeStruct((BATCH * n_q, vw), BF16),
        compiler_params=_cparams("arbitrary", "arbitrary"),
        name="mla_attn",
    )(q, *ks, *vs)


def _out_proj_kernel(u_ref, yf_ref, yb_ref, d_ref, gw_ref, gb_ref, a2_ref, a3_ref, w_ref, x_ref, g_ref, gate_ref,
                     fg_ref, fshift_ref, fscale_ref, o_ref, f_ref):
    nc = 512
    k1 = S5_WIDTH
    k2 = k1 + a2_ref.shape[1]
    for rs in _sub_tiles(o_ref.shape[0]):
        y = d_ref[...] * u_ref[0, rs, :] + yf_ref[0, rs, :] + yb_ref[0, rs, :]
        gl = jax.nn.gelu(y)
        a1 = (gl * jax.nn.sigmoid(_dot(gl.astype(BF16), gw_ref[...]) + gb_ref[...])).astype(BF16)
        for n in range(D_MODEL // nc):
            cs = slice(n * nc, (n + 1) * nc)
            o_ref[rs, cs] = (_dot(a2_ref[rs, :], w_ref[k1:k2, cs]) + _dot(a3_ref[rs, :], w_ref[k2:D_MODEL, cs])
                             + _dot(a1, w_ref[0:k1, cs]))
        xn = x_ref[rs, :] + gate_ref[0] * _rms(o_ref[rs, :], g_ref[...])
        o_ref[rs, :] = xn
        f_ref[rs, :] = (_rms(xn, fg_ref[...]) * (1.0 + fscale_ref[0]) + fshift_ref[0]).astype(BF16)


def _out_proj(u3, yf, yb, y_off, d, glu_w, glu_b, a2, a3, w, x2d, g, gate, fg, fshift, fscale, layer, tm, mod_row):
    rows = a2.shape[0]
    per_b = u3.shape[1] // tm
    once = pl.Buffered(1)
    uspec = pl.BlockSpec((1, tm, S5_WIDTH), lambda i: (i // per_b, i % per_b, 0))
    yspec = pl.BlockSpec((1, tm, S5_WIDTH), lambda i: (i // per_b, i % per_b + y_off // tm, 0))
    vec5 = pl.BlockSpec((1, S5_WIDTH), lambda i: (0, 0))
    vec = pl.BlockSpec((1, D_MODEL), lambda i: (0, 0))
    mod = pl.BlockSpec((1, 1, D_MODEL), lambda i: (mod_row(i), 0, 0))
    row = pl.BlockSpec((tm, D_MODEL), lambda i: (i, 0))
    return pl.pallas_call(
        _out_proj_kernel,
        grid=(rows // tm,),
        in_specs=[uspec, yspec, yspec, vec5,
                  pl.BlockSpec((None, S5_WIDTH, S5_WIDTH), lambda i: (layer, 0, 0), pipeline_mode=once), vec5,
                  pl.BlockSpec((tm, a2.shape[1]), lambda i: (i, 0)),
                  pl.BlockSpec((tm, a3.shape[1]), lambda i: (i, 0)),
                  pl.BlockSpec((None, D_MODEL, D_MODEL), lambda i: (layer, 0, 0), pipeline_mode=once),
                  row, vec, mod, vec, mod, mod],
        out_specs=[row, row],
        out_shape=[jax.ShapeDtypeStruct((rows, D_MODEL), F32), jax.ShapeDtypeStruct((rows, D_MODEL), BF16)],
        compiler_params=_cparams("arbitrary"),
        name="out_proj",
    )(u3, yf, yb, d.reshape(1, -1), glu_w, glu_b.reshape(1, -1), a2, a3, w, x2d, g.reshape(1, -1), gate,
      fg.reshape(1, -1), fshift, fscale)


def _ffn_kernel(x_ref, f_ref, w1_ref, w2_ref, gpost_ref, gate_ref, o_ref, h_ref):
    j = pl.program_id(1)

    @pl.when(j == 0)
    def _():
        o_ref[...] = jnp.zeros_like(o_ref)

    nc = 512
    last = j == pl.num_programs(1) - 1

    def accumulate(rs):
        h = _dot(f_ref[rs, :], w1_ref[...].astype(BF16))
        h_ref[rs, :] = jnp.square(jnp.maximum(h, 0.0)).astype(BF16)
        for n in range(D_MODEL // nc):
            cs = slice(n * nc, (n + 1) * nc)
            o_ref[rs, cs] += _dot(h_ref[rs, :], w2_ref[:, cs].astype(BF16))

    @pl.when(jnp.logical_not(last))
    def _():
        accumulate(slice(None))

    @pl.when(last)
    def _():
        half = o_ref.shape[0] // 2
        for rs in (slice(0, half), slice(half, 2 * half)):
            accumulate(rs)
            o_ref[rs, :] = x_ref[rs, :] + gate_ref[0] * _rms(o_ref[rs, :], gpost_ref[...])


def _ffn(x2d, f2d, w1, w2, gpost, gate, layer, tm, mod_row):
    rows = x2d.shape[0]
    tf = TF_FFN
    mod = pl.BlockSpec((1, 1, D_MODEL), lambda i, j: (mod_row(i), 0, 0))
    vec = pl.BlockSpec((1, D_MODEL), lambda i, j: (0, 0))
    once = pl.Buffered(1)
    return pl.pallas_call(
        _ffn_kernel,
        grid=(rows // tm, D_FF // tf),
        in_specs=[pl.BlockSpec((tm, D_MODEL), lambda i, j: (i, 0)),
                  pl.BlockSpec((tm, D_MODEL), lambda i, j: (i, 0), pipeline_mode=once),
                  pl.BlockSpec((None, D_MODEL, tf), lambda i, j: (layer, 0, j)),
                  pl.BlockSpec((None, tf, D_MODEL), lambda i, j: (layer, j, 0)),
                  vec, mod],
        out_specs=pl.BlockSpec((tm, D_MODEL), lambda i, j: (i, 0), pipeline_mode=once),
        out_shape=jax.ShapeDtypeStruct((rows, D_MODEL), F32),
        scratch_shapes=[pltpu.VMEM((tm, tf), BF16)],
        compiler_params=_cparams("arbitrary", "arbitrary"),
        name="ffn",
    )(x2d, f2d, w1, w2, gpost.reshape(1, -1), gate)


def _rope_tables():
    t = np.arange(SEQ)
    row = (t // GRID_W).astype(np.float32)[:, None]
    col = (t % GRID_W).astype(np.float32)[:, None]

    def tables(rot_dim):
        quarter = rot_dim // 4
        inv_freq = np.float32(ROPE_BASE) ** (-np.arange(quarter, dtype=np.float32) / np.float32(quarter))
        ar, ac = row * inv_freq, col * inv_freq
        cos = np.concatenate([np.cos(ar), np.cos(ar), np.cos(ac), np.cos(ac)], axis=1)
        sin = np.concatenate([-np.sin(ar), np.sin(ar), -np.sin(ac), np.sin(ac)], axis=1)
        pad = LANE - rot_dim
        if pad:
            cos = np.concatenate([cos, np.ones((SEQ, pad), np.float32)], axis=1)
            sin = np.concatenate([sin, np.zeros((SEQ, pad), np.float32)], axis=1)
        return jnp.asarray(cos, F32), jnp.asarray(sin, F32)

    return tables(HEAD_DIM), tables(MLA_ROPE)


def kernel(x, c, ctx, c_ctx, ada_w, ada_b, norm_mix_pre, norm_mix_post, norm_ffn_pre, norm_ffn_post, w_in, w_out, s5_a_re, s5_a_im, s5_log_dt, s5_b_re, s5_b_im, s5_c_re, s5_c_im, s5_d, s5_glu_w, s5_glu_b, swa_sink, mla_q_norm, mla_w_uq, mla_kv_norm, mla_w_ukv, ffn_w1, ffn_w2):
    (cos_swa, sin_swa), (cos_mla, sin_mla) = _rope_tables()
    tabs = (cos_swa, sin_swa, cos_mla, sin_mla)

    cvec = jnp.concatenate([c, c_ctx[None, :], jnp.zeros((3, D_MODEL), F32)], axis=0)
    mods = _ada(cvec, ada_w, ada_b)

    tm_p, tm_f = TM_PROJ, TM_FFN

    def lat_row(tm):
        return lambda i: i // (SEQ // tm)

    ctx_row = lambda i: 4

    xl = x.reshape(BATCH * SEQ, D_MODEL)
    xc = ctx.reshape(BATCH * CTX_LEN, D_MODEL)

    w_in_pad = jnp.pad(w_in, ((0, 0), (0, 0), (0, IN_COLS - IN_WIDTH))).astype(BF16)
    w_out_b = w_out.astype(BF16)
    glu_w_b = s5_glu_w.astype(BF16)
    w_uq_pad = jnp.pad(mla_w_uq.reshape(DEPTH, MLA_Q_RANK, MLA_HEADS, MLA_NOPE + MLA_ROPE),
                       ((0, 0), (0, 0), (0, 0), (0, MLA_QK_PAD - MLA_NOPE - MLA_ROPE))
                       ).reshape(DEPTH, MLA_Q_RANK, MLA_HEADS * MLA_QK_PAD).astype(BF16)
    w_ukv4 = mla_w_ukv.reshape(DEPTH, MLA_KV_RANK, MLA_HEADS, MLA_NOPE + MLA_V)
    w_ukv_perm = jnp.concatenate([w_ukv4[..., :MLA_NOPE].reshape(DEPTH, MLA_KV_RANK, -1),
                                  w_ukv4[..., MLA_NOPE:].reshape(DEPTH, MLA_KV_RANK, -1)], axis=2).astype(BF16)
    s5_w = _s5_disc(s5_a_re, s5_a_im, s5_log_dt, s5_b_re, s5_b_im, s5_c_re, s5_c_im)

    for i in range(DEPTH):
        need_ctx = i < DEPTH - 1
        mod = [mods[i, :, k * D_MODEL:(k + 1) * D_MODEL].reshape(8, 1, D_MODEL) for k in range(6)]
        g_pre = norm_mix_pre[i].reshape(1, -1)

        u_l, qk_l, vs_l, q_l, k_l, v_l = _mix_in(
            xl, mod[0], mod[1], g_pre, w_in_pad, w_uq_pad, w_ukv_perm, mla_q_norm[i], mla_kv_norm[i],
            tabs, i, tm_p, lat_row(tm_p), True, SEQ)
        u_c, qk_c, vs_c, q_c, k_c, v_c = _mix_in(
            xc, mod[0], mod[1], g_pre, w_in_pad, w_uq_pad, w_ukv_perm, mla_q_norm[i], mla_kv_norm[i],
            tabs, i, tm_p, ctx_row, False, tm_p)

        yf, yb = _s5_scan(u_c, u_l, *s5_w, i, 0 if need_ctx else S5_CTX_CHUNKS)
        u3_l = u_l.reshape(BATCH, SEQ, S5_WIDTH)
        u3_c = u_c.reshape(BATCH, CTX_LEN, S5_WIDTH)

        swa_l = _swa_latent(qk_l, vs_l, qk_c, vs_c, swa_sink[i])

        mla_l = _mla_attn(q_l, [k_c, k_l], [v_c, v_l], SEQ, TQ_MLA)

        xl, fl = _out_proj(u3_l, yf, yb, 0, s5_d[i], glu_w_b, s5_glu_b[i], swa_l, mla_l, w_out_b, xl,
                           norm_mix_post[i], mod[2], norm_ffn_pre[i], mod[3], mod[4], i, tm_p, lat_row(tm_p))
        xl = _ffn(xl, fl, ffn_w1, ffn_w2, norm_ffn_post[i], mod[5], i, tm_f, lat_row(tm_f))

        if need_ctx:
            swa_c = _swa_context(qk_c, vs_c, swa_sink[i])
            mla_c = _mla_attn(q_c, [k_c], [v_c], CTX_LEN, TQ_MLA_CTX)
            xc, fc = _out_proj(u3_c, yf, yb, SEQ, s5_d[i], glu_w_b, s5_glu_b[i], swa_c, mla_c, w_out_b, xc,
                               norm_mix_post[i], mod[2], norm_ffn_pre[i], mod[3], mod[4], i, CTX_LEN, ctx_row)
            xc = _ffn(xc, fc, ffn_w1, ffn_w2, norm_ffn_post[i], mod[5], i, tm_f, ctx_row)

    return xl.reshape(BATCH, SEQ, D_MODEL)
```

```python
import functools

import jax
import jax.numpy as jnp
import numpy as np
from jax import lax
from jax.experimental import pallas as pl
from jax.experimental.pallas import tpu as pltpu

F32 = jnp.float32
BF16 = jnp.bfloat16

D_MODEL = 2048
BATCH = 4
SEQ = 2048
DEPTH = 2
GRID_W = 64
CTX_LEN = 256
EPS = 1e-6
ROPE_BASE = 10000.0
NEG_INF = -1e30
BLOCK = 128
HEAD_DIM = 128
S5_WIDTH = 512
S5_GROUP = 16
S5_GROUPS = 32
S5_STATE = 64
S5_LANES = S5_GROUPS * S5_STATE
SWA_HEADS = 6
SWA_KV_HEADS = 2
SWA_REP = SWA_HEADS // SWA_KV_HEADS
MLA_HEADS = 6
MLA_Q_RANK = 768
MLA_KV_RANK = 512
MLA_NOPE = 128
MLA_ROPE = 64
MLA_V = 128
MLA_QK_PAD = 256
MLA_V_PAD = 256
D_FF = 4 * D_MODEL
IN_WIDTHS = (512, 768, 256, 256, 768, 512, 64)
IN_WIDTH = sum(IN_WIDTHS)
OFF_U, OFF_QS, OFF_KS, OFF_VS, OFF_CQ, OFF_CKV, OFF_KR = 0, 512, 1280, 1536, 1792, 2560, 3072

LANE = 128
SUBLANE = 8
V7X_VMEM_BYTES = 64 * 1024 * 1024
VMEM_LIMIT = V7X_VMEM_BYTES * 7 // 8

TM_PROJ = 512
SUB_ROWS = 256
TM_FFN = 1024
TF_FFN = 512
TN_ADA = 1024
TQ_MLA = 512
TQ_MLA_CTX = CTX_LEN
S5_TC = 128
SWA_QB = 8


def _cparams(*sem):
    return pltpu.CompilerParams(dimension_semantics=sem, vmem_limit_bytes=VMEM_LIMIT)


def _dot(a, b):
    return jnp.dot(a, b, preferred_element_type=F32)


def _dot_nt(a, b):
    return lax.dot_general(a, b, (((1,), (1,)), ((), ())), preferred_element_type=F32)


def _rms(x, g):
    return x * lax.rsqrt(jnp.mean(x * x, axis=-1, keepdims=True) + EPS) * g


def _sub_tiles(rows):
    return [slice(r, r + SUB_ROWS) for r in range(0, rows, SUB_ROWS)]


def _rope(x, cos, sin, half):
    lane = lax.broadcasted_iota(jnp.int32, x.shape, 1)
    fwd = pltpu.roll(x, LANE - half, 1)
    bwd = pltpu.roll(x, half, 1)
    sw = jnp.where((lane % (2 * half)) < half, fwd, bwd)
    return x * cos + sw * sin


def _ada_kernel(c_ref, w_ref, b_ref, o_ref):
    c = c_ref[...]
    s = c * jax.nn.sigmoid(c)
    o_ref[0] = _dot(s.astype(BF16), w_ref[0].astype(BF16)) + b_ref[0]


def _ada(cvec, ada_w, ada_b):
    tn = TN_ADA
    n = 6 * D_MODEL
    return pl.pallas_call(
        _ada_kernel,
        grid=(DEPTH, n // tn),
        in_specs=[pl.BlockSpec((8, D_MODEL), lambda l, j: (0, 0)),
                  pl.BlockSpec((1, D_MODEL, tn), lambda l, j: (l, 0, j)),
                  pl.BlockSpec((1, 1, tn), lambda l, j: (l, 0, j))],
        out_specs=pl.BlockSpec((1, 8, tn), lambda l, j: (l, 0, j)),
        out_shape=jax.ShapeDtypeStruct((DEPTH, 8, n), F32),
        compiler_params=_cparams("arbitrary", "arbitrary"),
        name="ada",
    )(cvec, ada_w, ada_b.reshape(DEPTH, 1, n))


IN_COLS = 3200
SWA_QK_W = (SWA_HEADS + SWA_KV_HEADS) * HEAD_DIM


def _mix_in_kernel(x_ref, shift_ref, scale_ref, g_ref, w_ref, wq_ref, wkv_ref, qn_ref, kvn_ref,
                   cs_ref, ss_ref, cm_ref, sm_ref,
                   u_ref, qk_ref, vs_ref, q_ref, k_ref, v_ref, h_ref, *, rope):
    for rs in _sub_tiles(x_ref.shape[0]):
        h = _rms(x_ref[rs, :], g_ref[...]) * (1.0 + scale_ref[0]) + shift_ref[0]
        h_ref[rs, :] = h.astype(BF16)

        def proj(lo, width):
            return _dot(h_ref[rs, :], w_ref[:, lo:lo + width])

        u_ref[rs, :] = proj(OFF_U, S5_WIDTH)

        swa_scale = HEAD_DIM ** -0.5
        per = 4
        for c in range(SWA_QK_W // (per * HEAD_DIM)):
            zc = proj(OFF_QS + c * per * HEAD_DIM, per * HEAD_DIM)
            for hh in range(per):
                head = c * per + hh
                xh = zc[:, hh * HEAD_DIM:(hh + 1) * HEAD_DIM]
                if rope:
                    xh = _rope(xh, cs_ref[rs, :], ss_ref[rs, :], HEAD_DIM // 4)
                if head < SWA_HEADS:
                    xh = xh * swa_scale
                qk_ref[rs, head * HEAD_DIM:(head + 1) * HEAD_DIM] = xh.astype(BF16)

        vs_ref[rs, :] = proj(OFF_VS, SWA_KV_HEADS * HEAD_DIM).astype(BF16)

        mla_scale = (MLA_NOPE + MLA_ROPE) ** -0.5
        cq = proj(OFF_CQ, MLA_Q_RANK)
        q = _dot(_rms(cq, qn_ref[...]).astype(BF16), wq_ref[...])
        for hd in range(MLA_HEADS):
            lo = hd * MLA_QK_PAD
            q_ref[rs, lo:lo + MLA_NOPE] = (q[:, lo:lo + MLA_NOPE] * mla_scale).astype(BF16)
            r = q[:, lo + MLA_NOPE:lo + MLA_QK_PAD]
            if rope:
                r = _rope(r, cm_ref[rs, :], sm_ref[rs, :], MLA_ROPE // 4)
            q_ref[rs, lo + MLA_NOPE:lo + MLA_QK_PAD] = (r * mla_scale).astype(BF16)

        ckv = proj(OFF_CKV, MLA_KV_RANK)
        kv = _dot(_rms(ckv, kvn_ref[...]).astype(BF16), wkv_ref[...])
        kr = proj(OFF_KR, LANE)
        if rope:
            kr = _rope(kr, cm_ref[rs, :], sm_ref[rs, :], MLA_ROPE // 4)
        kr = kr.astype(BF16)
        for hd in range(MLA_HEADS):
            lo = hd * MLA_QK_PAD
            k_ref[rs, lo:lo + MLA_NOPE] = kv[:, hd * MLA_NOPE:(hd + 1) * MLA_NOPE].astype(BF16)
            k_ref[rs, lo + MLA_NOPE:lo + MLA_QK_PAD] = kr
            vlo = hd * MLA_V_PAD
            vh = kv[:, MLA_HEADS * MLA_NOPE + hd * MLA_V:MLA_HEADS * MLA_NOPE + (hd + 1) * MLA_V].astype(BF16)
            v_ref[rs, vlo:vlo + MLA_V] = vh
            v_ref[rs, vlo + MLA_V:vlo + MLA_V_PAD] = jnp.ones_like(vh)


def _mix_in(x2d, shift, scale, g, w_in, w_uq, w_ukv, q_norm, kv_norm, tabs, layer, tm, mod_row, rope, seq):
    rows = x2d.shape[0]
    once = pl.Buffered(1)
    tab = pl.BlockSpec((tm, LANE), lambda i: (i % (seq // tm), 0))

    def res(shape):
        return pl.BlockSpec((None,) + shape, lambda i: (layer, 0, 0), pipeline_mode=once)

    def out(width):
        return pl.BlockSpec((tm, width), lambda i: (i, 0))

    widths = (S5_WIDTH, SWA_QK_W, SWA_KV_HEADS * HEAD_DIM,
              MLA_HEADS * MLA_QK_PAD, MLA_HEADS * MLA_QK_PAD, MLA_HEADS * MLA_V_PAD)
    dtypes = (F32, BF16, BF16, BF16, BF16, BF16)
    return pl.pallas_call(
        functools.partial(_mix_in_kernel, rope=rope),
        grid=(rows // tm,),
        in_specs=[pl.BlockSpec((tm, D_MODEL), lambda i: (i, 0)),
                  pl.BlockSpec((1, 1, D_MODEL), lambda i: (mod_row(i), 0, 0)),
                  pl.BlockSpec((1, 1, D_MODEL), lambda i: (mod_row(i), 0, 0)),
                  pl.BlockSpec((1, D_MODEL), lambda i: (0, 0)),
                  res((D_MODEL, IN_COLS)),
                  res((MLA_Q_RANK, MLA_HEADS * MLA_QK_PAD)),
                  res((MLA_KV_RANK, MLA_HEADS * (MLA_NOPE + MLA_V))),
                  pl.BlockSpec((1, MLA_Q_RANK), lambda i: (0, 0)),
                  pl.BlockSpec((1, MLA_KV_RANK), lambda i: (0, 0)),
                  tab, tab, tab, tab],
        out_specs=[out(w) for w in widths],
        out_shape=[jax.ShapeDtypeStruct((rows, w), dt) for w, dt in zip(widths, dtypes)],
        scratch_shapes=[pltpu.VMEM((tm, D_MODEL), BF16)],
        compiler_params=_cparams("arbitrary"),
        name="mix_in",
    )(x2d, shift, scale, g, w_in, w_uq, w_ukv, q_norm.reshape(1, -1), kv_norm.reshape(1, -1), *tabs)


S5_GH = S5_GROUPS // 2


def _zoh(are, aim, ldt):
    dt = jnp.exp(ldt)
    mag = jnp.exp(are * dt)
    lr = mag * jnp.cos(aim * dt)
    li = mag * jnp.sin(aim * dt)
    den = are * are + aim * aim
    nr = lr - 1.0
    return lr, li, (nr * are + li * aim) / den, (li * are - nr * aim) / den


def _s5_disc_kernel(are_ref, aim_ref, ldt_ref, arec_ref, aimc_ref, ldtc_ref, br_ref, bi_ref, cr_ref, ci_ref,
                    lam_ref, wbr_ref, wbi_ref, wcr_ref, wci_ref):
    lr, li, _, _ = _zoh(are_ref[0], aim_ref[0], ldt_ref[0])
    lam_ref[0, 0:1, :] = lr
    lam_ref[0, 1:2, :] = li
    gh, n, p = S5_GH, S5_STATE, S5_GROUP
    kin, kst = gh * p, gh * n

    def iota(shape, axis):
        return lax.broadcasted_iota(jnp.int32, shape, axis)

    tile_n = jnp.where(iota((n, kst), 0) == (iota((n, kst), 1) & (n - 1)), 1.0, 0.0).astype(BF16)
    tile_p = jnp.where(iota((p, kin), 0) == (iota((p, kin), 1) & (p - 1)), 1.0, 0.0).astype(BF16)
    lp, ln = p.bit_length() - 1, n.bit_length() - 1
    diag_b = (iota((kin, kst), 0) >> lp) == (iota((kin, kst), 1) >> ln)
    diag_c = (iota((kst, kin), 0) >> ln) == (iota((kst, kin), 1) >> lp)
    for h in range(2):
        _, _, fr, fi = _zoh(arec_ref[0, h], aimc_ref[0, h], ldtc_ref[0, h])
        fr = jnp.broadcast_to(fr[:, None, :], (gh, p, n)).reshape(kin, n)
        fi = jnp.broadcast_to(fi[:, None, :], (gh, p, n)).reshape(kin, n)
        bbr = (fr * br_ref[0, h] - fi * bi_ref[0, h]).astype(BF16)
        bbi = (fr * bi_ref[0, h] + fi * br_ref[0, h]).astype(BF16)
        wbr_ref[0, h] = jnp.where(diag_b, _dot(bbr, tile_n), 0.0).astype(BF16)
        wbi_ref[0, h] = jnp.where(diag_b, _dot(bbi, tile_n), 0.0).astype(BF16)
        wcr_ref[0, h] = jnp.where(diag_c, _dot(cr_ref[0, h].astype(BF16), tile_p), 0.0).astype(BF16)
        wci_ref[0, h] = jnp.where(diag_c, _dot(ci_ref[0, h].astype(BF16), tile_p), 0.0).astype(BF16)


def _s5_disc(a_re, a_im, log_dt, b_re, b_im, c_re, c_im):
    nd = DEPTH * 2
    gh, n, p = S5_GH, S5_STATE, S5_GROUP
    kin, kst = gh * p, gh * n
    are = a_re.reshape(nd, 1, S5_LANES)
    aim = a_im.reshape(nd, 1, S5_LANES)
    ldt_gn = jnp.broadcast_to(log_dt[..., None], a_re.shape)
    ldt = ldt_gn.reshape(nd, 1, S5_LANES)
    compact = lambda a: a.reshape(nd, 2, gh, n)
    bt = lambda b: b.reshape(nd, 2, gh, n, p).transpose(0, 1, 2, 4, 3).reshape(nd, 2, kin, n)
    ct = lambda c: c.reshape(nd, 2, gh, p, n).transpose(0, 1, 2, 4, 3).reshape(nd, 2, kst, p)
    vec = pl.BlockSpec((1, 1, S5_LANES), lambda d: (d, 0, 0))
    cvec = pl.BlockSpec((1, 2, gh, n), lambda d: (d, 0, 0, 0))
    bin_spec = pl.BlockSpec((1, 2, kin, n), lambda d: (d, 0, 0, 0))
    cin_spec = pl.BlockSpec((1, 2, kst, p), lambda d: (d, 0, 0, 0))
    bspec = pl.BlockSpec((1, 2, kin, kst), lambda d: (d, 0, 0, 0))
    cspec = pl.BlockSpec((1, 2, kst, kin), lambda d: (d, 0, 0, 0))
    return pl.pallas_call(
        _s5_disc_kernel,
        grid=(nd,),
        in_specs=[vec, vec, vec, cvec, cvec, cvec, bin_spec, bin_spec, cin_spec, cin_spec],
        out_specs=[pl.BlockSpec((1, 2, S5_LANES), lambda d: (d, 0, 0)), bspec, bspec, cspec, cspec],
        out_shape=[jax.ShapeDtypeStruct((nd, 2, S5_LANES), F32),
                   jax.ShapeDtypeStruct((nd, 2, kin, kst), BF16),
                   jax.ShapeDtypeStruct((nd, 2, kin, kst), BF16),
                   jax.ShapeDtypeStruct((nd, 2, kst, kin), BF16),
                   jax.ShapeDtypeStruct((nd, 2, kst, kin), BF16)],
        compiler_params=_cparams("arbitrary"),
        name="s5_disc",
    )(are, aim, ldt, compact(a_re), compact(a_im), compact(ldt_gn),
      bt(b_re), bt(b_im), ct(c_re), ct(c_im))


S5_CTX_CHUNKS = CTX_LEN // S5_TC
S5_LAT_CHUNKS = SEQ // S5_TC


def _s5_scan_kernel(ucf_ref, ulf_ref, ucb_ref, ulb_ref, lam_ref, wbr_ref, wbi_ref, wcr_ref, wci_ref,
                    yf_ref, yb_ref, fr_ref, fi_ref, gr_ref, gi_ref, sr_ref, si_ref, ut_ref, yt_ref):
    c = pl.program_id(0)
    tc = S5_TC
    kin, kst = S5_WIDTH // 2, S5_LANES // 2
    nlb = kst // LANE

    @pl.when(c == 0)
    def _():
        sr_ref[...] = jnp.zeros_like(sr_ref)
        si_ref[...] = jnp.zeros_like(si_ref)

    in_ctx = c < S5_CTX_CHUNKS
    state = [(fr_ref, fi_ref), (gr_ref, gi_ref)]
    nub = S5_WIDTH // LANE

    u_dir = []
    for d, (uc_ref, ul_ref) in enumerate(((ucf_ref, ulf_ref), (ucb_ref, ulb_ref))):
        u = jnp.where(in_ctx, uc_ref[...], ul_ref[...])
        for j in range(nub):
            for b in range(BATCH):
                ut_ref[d, j, pl.ds(b, tc, stride=BATCH), :] = u[b, :, j * LANE:(j + 1) * LANE]
        u_dir.append(jnp.concatenate([ut_ref[d, j] for j in range(nub)], axis=1).astype(BF16))

    def project(d, h, part):
        w_ref = (wbr_ref, wbi_ref)[part]
        dst = state[d][part]
        p = _dot(u_dir[d][:, h * kin:(h + 1) * kin], w_ref[d, h])
        for j in range(nlb):
            dst[h * nlb + j] = p[:, j * LANE:(j + 1) * LANE]

    def scan_steps(h, k0, k1, carry):
        ls = slice(h * nlb, (h + 1) * nlb)
        fwd_rows = lax.broadcasted_iota(jnp.int32, (nlb, SUBLANE, LANE), 1) < BATCH
        la_r = jnp.where(fwd_rows, lam_ref[0, 0, ls], lam_ref[1, 0, ls])
        la_i = jnp.where(fwd_rows, lam_ref[0, 1, ls], lam_ref[1, 1, ls])
        lb_r = jnp.where(fwd_rows, lam_ref[1, 0, ls], lam_ref[0, 0, ls])
        lb_i = jnp.where(fwd_rows, lam_ref[1, 1, ls], lam_ref[0, 1, ls])
        s_r, s_i = carry
        for k in range(k0, k1):
            rf = slice(k * SUBLANE, (k + 1) * SUBLANE)
            kb = tc // 2 - 1 - k
            rb = slice(kb * SUBLANE, (kb + 1) * SUBLANE)
            f_r, f_i = fr_ref[ls, rf, :], fi_ref[ls, rf, :]
            g_r, g_i = gr_ref[ls, rb, :], gi_ref[ls, rb, :]
            a_r = la_r * s_r - la_i * s_i + jnp.where(fwd_rows, f_r, g_r)
            a_i = la_r * s_i + la_i * s_r + jnp.where(fwd_rows, f_i, g_i)
            t_r = pltpu.roll(a_r, BATCH, 1)
            t_i = pltpu.roll(a_i, BATCH, 1)
            b_r = lb_r * t_r - lb_i * t_i + jnp.where(fwd_rows, g_r, f_r)
            b_i = lb_r * t_i + lb_i * t_r + jnp.where(fwd_rows, g_i, f_i)
            fr_ref[ls, rf, :] = jnp.where(fwd_rows, a_r, b_r)
            fi_ref[ls, rf, :] = jnp.where(fwd_rows, a_i, b_i)
            gr_ref[ls, rb, :] = jnp.where(fwd_rows, b_r, a_r)
            gi_ref[ls, rb, :] = jnp.where(fwd_rows, b_i, a_i)
            s_r, s_i = pltpu.roll(b_r, BATCH, 1), pltpu.roll(b_i, BATCH, 1)
        return s_r, s_i

    def half_rows(src, h):
        return jnp.concatenate([src[h * nlb + j] for j in range(nlb)], axis=1).astype(BF16)

    def readout(d, h):
        y = _dot(half_rows(state[d][0], h), wcr_ref[d, h]) - _dot(half_rows(state[d][1], h), wci_ref[d, h])
        for j in range(kin // LANE):
            yt_ref[d, h * (kin // LANE) + j] = y[:, j * LANE:(j + 1) * LANE]

    def emit(d):
        y_ref = (yf_ref, yb_ref)[d]
        for b in range(BATCH):
            y_ref[b] = jnp.concatenate([yt_ref[d, j, pl.ds(b, tc, stride=BATCH), :] for j in range(nub)], axis=1)

    quarters = 4
    per = tc // 2 // quarters
    pieces = [(d, part) for d in range(2) for part in range(2)]
    for d, part in pieces:
        project(d, 0, part)
    carry = (sr_ref[0:nlb], si_ref[0:nlb])
    for q, (d, part) in enumerate(pieces):
        carry = scan_steps(0, q * per, (q + 1) * per, carry)
        project(d, 1, part)
    sr_ref[0:nlb], si_ref[0:nlb] = carry
    carry = (sr_ref[nlb:2 * nlb], si_ref[nlb:2 * nlb])
    for q in range(quarters):
        carry = scan_steps(1, q * per, (q + 1) * per, carry)
        if q % 2 == 1:
            readout(q // 2, 0)
    sr_ref[nlb:2 * nlb], si_ref[nlb:2 * nlb] = carry
    for d in range(2):
        readout(d, 1)
        emit(d)


def _s5_scan(u_c, u_l, lam, wbr, wbi, wcr, wci, layer, skip_chunks):
    tc = S5_TC
    nchunk = S5_CTX_CHUNKS + S5_LAT_CHUNKS
    kin, kst = S5_WIDTH // 2, S5_LANES // 2
    zc3 = u_c.reshape(BATCH, CTX_LEN, S5_WIDTH)
    zl3 = u_l.reshape(BATCH, SEQ, S5_WIDTH)

    def bwd_chunk(c):
        return jnp.where(c < S5_CTX_CHUNKS, S5_CTX_CHUNKS - 1 - c, nchunk + S5_CTX_CHUNKS - 1 - c)

    def ctx_spec(chunk_of):
        return pl.BlockSpec((BATCH, tc, S5_WIDTH),
                            lambda c: (0, jnp.clip(chunk_of(c), 0, S5_CTX_CHUNKS - 1), 0))

    def lat_spec(chunk_of):
        return pl.BlockSpec((BATCH, tc, S5_WIDTH),
                            lambda c: (0, jnp.clip(chunk_of(c) - S5_CTX_CHUNKS, 0, S5_LAT_CHUNKS - 1), 0))

    def out_block(chunk):
        return jnp.where(chunk >= S5_CTX_CHUNKS, chunk - S5_CTX_CHUNKS, S5_LAT_CHUNKS + chunk)

    def out_spec(chunk_of):
        return pl.BlockSpec((BATCH, tc, S5_WIDTH),
                            lambda c: (0, out_block(chunk_of(jnp.maximum(c, skip_chunks))), 0))

    once = pl.Buffered(1)
    bspec = pl.BlockSpec((2, 2, kin, kst), lambda c: (layer, 0, 0, 0), pipeline_mode=once)
    cspec = pl.BlockSpec((2, 2, kst, kin), lambda c: (layer, 0, 0, 0), pipeline_mode=once)
    t_out = (nchunk - skip_chunks) * tc
    nblk = S5_LANES // LANE
    big = pltpu.VMEM((nblk, tc * BATCH, LANE), F32)
    state = pltpu.VMEM((nblk, SUBLANE, LANE), F32)
    narrow = pltpu.VMEM((2, S5_WIDTH // LANE, tc * BATCH, LANE), F32)
    lam = lam.reshape(DEPTH * 2, 2, nblk, 1, LANE)
    fwd_chunk = lambda c: c
    return pl.pallas_call(
        _s5_scan_kernel,
        grid=(nchunk,),
        in_specs=[ctx_spec(fwd_chunk), lat_spec(fwd_chunk), ctx_spec(bwd_chunk), lat_spec(bwd_chunk),
                  pl.BlockSpec((2, 2, nblk, 1, LANE), lambda c: (layer, 0, 0, 0, 0)),
                  bspec, bspec, cspec, cspec],
        out_specs=[out_spec(fwd_chunk), out_spec(bwd_chunk)],
        out_shape=[jax.ShapeDtypeStruct((BATCH, t_out, S5_WIDTH), F32)] * 2,
        scratch_shapes=[big, big, big, big, state, state, narrow, narrow],
        compiler_params=_cparams("arbitrary"),
        name="s5_scan",
    )(zc3, zl3, zc3, zl3, lam, wbr, wbi, wcr, wci)


def _sink_rows(sink_ref, g, rows):
    r = lax.broadcasted_iota(jnp.int32, (rows, 1), 0) // (rows // SWA_REP)
    s0, s1, s2 = sink_ref[SWA_REP * g], sink_ref[SWA_REP * g + 1], sink_ref[SWA_REP * g + 2]
    return jnp.where(r == 0, s0, jnp.where(r == 1, s1, s2))


def _stack_heads(x, n):
    return jnp.concatenate([x[:, h * HEAD_DIM:(h + 1) * HEAD_DIM] for h in range(n)], axis=0)


def _unstack_heads(x, n):
    rows = x.shape[0] // n
    return jnp.concatenate([x[h * rows:(h + 1) * rows] for h in range(n)], axis=1)


def _with_ones(v):
    return jnp.concatenate([v, jnp.ones_like(v)], axis=1)


def _swa_latent_kernel(sink_ref, q_ref, kp_ref, km_ref, kn_ref, vp_ref, vm_ref, vn_ref,
                       kx_ref, vx_ref, o_ref):
    n = pl.program_id(1)
    rows = SWA_REP * BLOCK
    qi = lax.broadcasted_iota(jnp.int32, (rows, 3 * BLOCK), 0) % BLOCK
    kj = lax.broadcasted_iota(jnp.int32, (rows, 3 * BLOCK), 1)
    in_window = (kj >= qi) & (kj <= qi + 2 * BLOCK)
    kv = []
    for g in range(SWA_KV_HEADS):
        gs = slice(g * HEAD_DIM, (g + 1) * HEAD_DIM)
        kspan = jnp.concatenate([kp_ref[:, gs], km_ref[:, gs], kn_ref[:, gs]], axis=0)
        vspan = _with_ones(jnp.concatenate([vp_ref[:, gs], vm_ref[:, gs], vn_ref[:, gs]], axis=0))
        kv.append((kspan, vspan, kx_ref[:, gs], _with_ones(vx_ref[:, gs]), _sink_rows(sink_ref, g, rows)))

    def scores(g, j):
        kspan, _, kx, _, _ = kv[g]
        qs = slice(g * SWA_REP * HEAD_DIM, (g + 1) * SWA_REP * HEAD_DIM)
        q = _stack_heads(q_ref[j * BLOCK:(j + 1) * BLOCK, qs], SWA_REP)
        return _dot_nt(q, kx), _dot_nt(q, kspan[j * BLOCK:(j + 3) * BLOCK])

    work = [(g, j) for g in range(SWA_KV_HEADS) for j in range(SWA_QB)]
    s_next = scores(*work[0])
    for idx, (g, j) in enumerate(work):
        s_ctx, s_band = s_next
        if idx + 1 < len(work):
            s_next = scores(*work[idx + 1])
        _, vspan, _, vx, sk = kv[g]
        qs = slice(g * SWA_REP * HEAD_DIM, (g + 1) * SWA_REP * HEAD_DIM)
        kpos = (n * SWA_QB + j - 1) * BLOCK + kj
        valid = in_window & (kpos >= 0) & (kpos < SEQ)
        s_band = jnp.where(valid, s_band, NEG_INF)
        m = jnp.maximum(jnp.maximum(jnp.max(s_ctx, axis=-1, keepdims=True),
                                    jnp.max(s_band, axis=-1, keepdims=True)), sk)
        p_ctx = jnp.exp(s_ctx - m).astype(BF16)
        p_band = jnp.exp(s_band - m).astype(BF16)
        o = _dot(p_ctx, vx) + _dot(p_band, vspan[j * BLOCK:(j + 3) * BLOCK])
        den = o[:, HEAD_DIM:] + jnp.exp(sk - m)
        o_ref[j * BLOCK:(j + 1) * BLOCK, qs] = _unstack_heads(
            o[:, :HEAD_DIM] / den, SWA_REP).astype(o_ref.dtype)


def _swa_latent(qk, vs, qk_c, vs_c, sink):
    nb = SEQ // BLOCK
    ng = nb // SWA_QB
    span = SWA_QB * BLOCK
    kvw = SWA_KV_HEADS * HEAD_DIM
    kcol = SWA_HEADS * HEAD_DIM // kvw
    vcol = 0
    kxcol = kcol

    def edge(col, blk_of):
        return pl.BlockSpec((BLOCK, kvw), lambda b, n, s: (b * nb + jnp.clip(blk_of(n), 0, nb - 1), col))

    def main(col):
        return pl.BlockSpec((span, kvw), lambda b, n, s: (b * ng + n, col))

    prev_blk = lambda n: n * SWA_QB - 1
    next_blk = lambda n: (n + 1) * SWA_QB
    grid_spec = pltpu.PrefetchScalarGridSpec(
        num_scalar_prefetch=1,
        grid=(BATCH, ng),
        in_specs=[pl.BlockSpec((span, SWA_HEADS * HEAD_DIM), lambda b, n, s: (b * ng + n, 0)),
                  edge(kcol, prev_blk), main(kcol), edge(kcol, next_blk),
                  edge(vcol, prev_blk), main(vcol), edge(vcol, next_blk),
                  pl.BlockSpec((CTX_LEN, kvw), lambda b, n, s: (b, kxcol)),
                  pl.BlockSpec((CTX_LEN, kvw), lambda b, n, s: (b, vcol))],
        out_specs=pl.BlockSpec((span, SWA_HEADS * HEAD_DIM), lambda b, n, s: (b * ng + n, 0)),
    )
    return pl.pallas_call(
        _swa_latent_kernel,
        grid_spec=grid_spec,
        out_shape=jax.ShapeDtypeStruct((BATCH * SEQ, SWA_HEADS * HEAD_DIM), BF16),
        compiler_params=_cparams("arbitrary", "arbitrary"),
        name="swa_latent",
    )(sink, qk, qk, qk, qk, vs, vs, vs, qk_c, vs_c)


def _swa_context_kernel(sink_ref, q_ref, k_ref, v_ref, o_ref):
    g = pl.program_id(1)
    q = _stack_heads(q_ref[...], SWA_REP)
    s = _dot_nt(q, k_ref[...])
    sk = _sink_rows(sink_ref, g, SWA_REP * CTX_LEN)
    m = jnp.maximum(jnp.max(s, axis=-1, keepdims=True), sk)
    p = jnp.exp(s - m)
    den = jnp.sum(p, axis=-1, keepdims=True) + jnp.exp(sk - m)
    o = _dot(p.astype(BF16), v_ref[...]) / den
    o_ref[...] = _unstack_heads(o, SWA_REP).astype(o_ref.dtype)


def _swa_context(qk_c, vs_c, sink):
    qw = SWA_REP * HEAD_DIM
    grid_spec = pltpu.PrefetchScalarGridSpec(
        num_scalar_prefetch=1,
        grid=(BATCH, SWA_KV_HEADS),
        in_specs=[pl.BlockSpec((CTX_LEN, qw), lambda b, g, s: (b, g)),
                  pl.BlockSpec((CTX_LEN, HEAD_DIM), lambda b, g, s: (b, SWA_HEADS + g)),
                  pl.BlockSpec((CTX_LEN, HEAD_DIM), lambda b, g, s: (b, g))],
        out_specs=pl.BlockSpec((CTX_LEN, qw), lambda b, g, s: (b, g)),
    )
    return pl.pallas_call(
        _swa_context_kernel,
        grid_spec=grid_spec,
        out_shape=jax.ShapeDtypeStruct((BATCH * CTX_LEN, SWA_HEADS * HEAD_DIM), BF16),
        compiler_params=_cparams("arbitrary", "arbitrary"),
        name="swa_context",
    )(sink, qk_c, qk_c, vs_c)


def _mla_attn_kernel(*refs, nseg):
    q_ref = refs[0]
    k_refs = refs[1:1 + nseg]
    v_refs = refs[1 + nseg:1 + 2 * nseg]
    o_ref = refs[1 + 2 * nseg]
    def scores(h):
        q = q_ref[:, h * MLA_QK_PAD:(h + 1) * MLA_QK_PAD]
        return [_dot_nt(q, k[:, h * MLA_QK_PAD:(h + 1) * MLA_QK_PAD]) for k in k_refs]

    s_next = scores(0)
    for h in range(MLA_HEADS):
        s = s_next
        if h + 1 < MLA_HEADS:
            s_next = scores(h + 1)
        m = functools.reduce(jnp.maximum, [jnp.max(x, axis=-1, keepdims=True) for x in s])
        p = [jnp.exp(x - m).astype(BF16) for x in s]
        o = sum(_dot(x, v[:, h * MLA_V_PAD:(h + 1) * MLA_V_PAD]) for x, v in zip(p, v_refs))
        o_ref[:, h * MLA_V:(h + 1) * MLA_V] = (o[:, :MLA_V] / o[:, MLA_V:]).astype(o_ref.dtype)


def _mla_attn(q, ks, vs, n_q, tq):
    nseg = len(ks)
    nq = n_q // tq
    lens = [k.shape[0] // BATCH for k in ks]
    kw, vw = MLA_HEADS * MLA_QK_PAD, MLA_HEADS * MLA_V
    in_specs = [pl.BlockSpec((tq, kw), lambda b, i: (b * nq + i, 0))]
    in_specs += [pl.BlockSpec((n, kw), lambda b, i: (b, 0)) for n in lens]
    in_specs += [pl.BlockSpec((n, MLA_HEADS * MLA_V_PAD), lambda b, i: (b, 0)) for n in lens]
    return pl.pallas_call(
        functools.partial(_mla_attn_kernel, nseg=nseg),
        grid=(BATCH, nq),
        in_specs=in_specs,
        out_specs=pl.BlockSpec((tq, vw), lambda b, i: (b * nq + i, 0)),
        out_shape=jax.ShapeDtypeStruct((BATCH * n_q, vw), BF16),
        compiler_params=_cparams("arbitrary", "arbitrary"),
        name="mla_attn",
    )(q, *ks, *vs)


def _out_proj_kernel(u_ref, yf_ref, yb_ref, d_ref, gw_ref, gb_ref, a2_ref, a3_ref, w_ref, x_ref, g_ref, gate_ref,
                     fg_ref, fshift_ref, fscale_ref, o_ref, f_ref):
    nc = 512
    k1 = S5_WIDTH
    k2 = k1 + a2_ref.shape[1]
    for rs in _sub_tiles(o_ref.shape[0]):
        y = d_ref[...] * u_ref[0, rs, :] + yf_ref[0, rs, :] + yb_ref[0, rs, :]
        gl = jax.nn.gelu(y)
        a1 = (gl * jax.nn.sigmoid(_dot(gl.astype(BF16), gw_ref[...]) + gb_ref[...])).astype(BF16)
        for n in range(D_MODEL // nc):
            cs = slice(n * nc, (n + 1) * nc)
            o_ref[rs, cs] = (_dot(a2_ref[rs, :], w_ref[k1:k2, cs]) + _dot(a3_ref[rs, :], w_ref[k2:D_MODEL, cs])
                             + _dot(a1, w_ref[0:k1, cs]))
        xn = x_ref[rs, :] + gate_ref[0] * _rms(o_ref[rs, :], g_ref[...])
        o_ref[rs, :] = xn
        f_ref[rs, :] = (_rms(xn, fg_ref[...]) * (1.0 + fscale_ref[0]) + fshift_ref[0]).astype(BF16)


def _out_proj(u3, yf, yb, y_off, d, glu_w, glu_b, a2, a3, w, x2d, g, gate, fg, fshift, fscale, layer, tm, mod_row):
    rows = a2.shape[0]
    per_b = u3.shape[1] // tm
    once = pl.Buffered(1)
    uspec = pl.BlockSpec((1, tm, S5_WIDTH), lambda i: (i // per_b, i % per_b, 0))
    yspec = pl.BlockSpec((1, tm, S5_WIDTH), lambda i: (i // per_b, i % per_b + y_off // tm, 0))
    vec5 = pl.BlockSpec((1, S5_WIDTH), lambda i: (0, 0))
    vec = pl.BlockSpec((1, D_MODEL), lambda i: (0, 0))
    mod = pl.BlockSpec((1, 1, D_MODEL), lambda i: (mod_row(i), 0, 0))
    row = pl.BlockSpec((tm, D_MODEL), lambda i: (i, 0))
    return pl.pallas_call(
        _out_proj_kernel,
        grid=(rows // tm,),
        in_specs=[uspec, yspec, yspec, vec5,
                  pl.BlockSpec((None, S5_WIDTH, S5_WIDTH), lambda i: (layer, 0, 0), pipeline_mode=once), vec5,
                  pl.BlockSpec((tm, a2.shape[1]), lambda i: (i, 0)),
                  pl.BlockSpec((tm, a3.shape[1]), lambda i: (i, 0)),
                  pl.BlockSpec((None, D_MODEL, D_MODEL), lambda i: (layer, 0, 0), pipeline_mode=once),
                  row, vec, mod, vec, mod, mod],
        out_specs=[row, row],
        out_shape=[jax.ShapeDtypeStruct((rows, D_MODEL), F32), jax.ShapeDtypeStruct((rows, D_MODEL), BF16)],
        compiler_params=_cparams("arbitrary"),
        name="out_proj",
    )(u3, yf, yb, d.reshape(1, -1), glu_w, glu_b.reshape(1, -1), a2, a3, w, x2d, g.reshape(1, -1), gate,
      fg.reshape(1, -1), fshift, fscale)


def _ffn_kernel(x_ref, f_ref, w1_ref, w2_ref, gpost_ref, gate_ref, o_ref, h_ref):
    j = pl.program_id(1)
    nc = 512
    last = j == pl.num_programs(1) - 1

    def accumulate(rs, first=False):
        h = _dot(f_ref[rs, :], w1_ref[...].astype(BF16))
        h_ref[rs, :] = jnp.square(jnp.maximum(h, 0.0)).astype(BF16)
        for n in range(D_MODEL // nc):
            cs = slice(n * nc, (n + 1) * nc)
            part = _dot(h_ref[rs, :], w2_ref[:, cs].astype(BF16))
            if first:
                o_ref[rs, cs] = part
            else:
                o_ref[rs, cs] += part

    @pl.when(j == 0)
    def _():
        accumulate(slice(None), first=True)

    @pl.when(jnp.logical_and(j > 0, jnp.logical_not(last)))
    def _():
        accumulate(slice(None))

    @pl.when(last)
    def _():
        half = o_ref.shape[0] // 2
        for rs in (slice(0, half), slice(half, 2 * half)):
            accumulate(rs)
            o_ref[rs, :] = x_ref[rs, :] + gate_ref[0] * _rms(o_ref[rs, :], gpost_ref[...])


def _ffn(x2d, f2d, w1, w2, gpost, gate, layer, tm, mod_row):
    rows = x2d.shape[0]
    tf = TF_FFN
    mod = pl.BlockSpec((1, 1, D_MODEL), lambda i, j: (mod_row(i), 0, 0))
    vec = pl.BlockSpec((1, D_MODEL), lambda i, j: (0, 0))
    once = pl.Buffered(1)
    return pl.pallas_call(
        _ffn_kernel,
        grid=(rows // tm, D_FF // tf),
        in_specs=[pl.BlockSpec((tm, D_MODEL), lambda i, j: (i, 0)),
                  pl.BlockSpec((tm, D_MODEL), lambda i, j: (i, 0)),
                  pl.BlockSpec((None, D_MODEL, tf), lambda i, j: (layer, 0, j)),
                  pl.BlockSpec((None, tf, D_MODEL), lambda i, j: (layer, j, 0)),
                  vec, mod],
        out_specs=pl.BlockSpec((tm, D_MODEL), lambda i, j: (i, 0), pipeline_mode=once),
        out_shape=jax.ShapeDtypeStruct((rows, D_MODEL), F32),
        scratch_shapes=[pltpu.VMEM((tm, tf), BF16)],
        compiler_params=_cparams("arbitrary", "arbitrary"),
        name="ffn",
    )(x2d, f2d, w1, w2, gpost.reshape(1, -1), gate)


def _rope_tables():
    t = np.arange(SEQ)
    row = (t // GRID_W).astype(np.float32)[:, None]
    col = (t % GRID_W).astype(np.float32)[:, None]

    def tables(rot_dim):
        quarter = rot_dim // 4
        inv_freq = np.float32(ROPE_BASE) ** (-np.arange(quarter, dtype=np.float32) / np.float32(quarter))
        ar, ac = row * inv_freq, col * inv_freq
        cos = np.concatenate([np.cos(ar), np.cos(ar), np.cos(ac), np.cos(ac)], axis=1)
        sin = np.concatenate([-np.sin(ar), np.sin(ar), -np.sin(ac), np.sin(ac)], axis=1)
        pad = LANE - rot_dim
        if pad:
            cos = np.concatenate([cos, np.ones((SEQ, pad), np.float32)], axis=1)
            sin = np.concatenate([sin, np.zeros((SEQ, pad), np.float32)], axis=1)
        return jnp.asarray(cos, F32), jnp.asarray(sin, F32)

    return tables(HEAD_DIM), tables(MLA_ROPE)


def kernel(x, c, ctx, c_ctx, ada_w, ada_b, norm_mix_pre, norm_mix_post, norm_ffn_pre, norm_ffn_post, w_in, w_out, s5_a_re, s5_a_im, s5_log_dt, s5_b_re, s5_b_im, s5_c_re, s5_c_im, s5_d, s5_glu_w, s5_glu_b, swa_sink, mla_q_norm, mla_w_uq, mla_kv_norm, mla_w_ukv, ffn_w1, ffn_w2):
    (cos_swa, sin_swa), (cos_mla, sin_mla) = _rope_tables()
    tabs = (cos_swa, sin_swa, cos_mla, sin_mla)

    cvec = jnp.concatenate([c, c_ctx[None, :], jnp.zeros((3, D_MODEL), F32)], axis=0)
    mods = _ada(cvec, ada_w, ada_b)

    tm_p, tm_f = TM_PROJ, TM_FFN

    def lat_row(tm):
        return lambda i: i // (SEQ // tm)

    ctx_row = lambda i: 4

    xl = x.reshape(BATCH * SEQ, D_MODEL)
    xc = ctx.reshape(BATCH * CTX_LEN, D_MODEL)

    w_in_pad = jnp.pad(w_in, ((0, 0), (0, 0), (0, IN_COLS - IN_WIDTH))).astype(BF16)
    w_out_b = w_out.astype(BF16)
    glu_w_b = s5_glu_w.astype(BF16)
    w_uq_pad = jnp.pad(mla_w_uq.reshape(DEPTH, MLA_Q_RANK, MLA_HEADS, MLA_NOPE + MLA_ROPE),
                       ((0, 0), (0, 0), (0, 0), (0, MLA_QK_PAD - MLA_NOPE - MLA_ROPE))
                       ).reshape(DEPTH, MLA_Q_RANK, MLA_HEADS * MLA_QK_PAD).astype(BF16)
    w_ukv4 = mla_w_ukv.reshape(DEPTH, MLA_KV_RANK, MLA_HEADS, MLA_NOPE + MLA_V)
    w_ukv_perm = jnp.concatenate([w_ukv4[..., :MLA_NOPE].reshape(DEPTH, MLA_KV_RANK, -1),
                                  w_ukv4[..., MLA_NOPE:].reshape(DEPTH, MLA_KV_RANK, -1)], axis=2).astype(BF16)
    s5_w = _s5_disc(s5_a_re, s5_a_im, s5_log_dt, s5_b_re, s5_b_im, s5_c_re, s5_c_im)

    for i in range(DEPTH):
        need_ctx = i < DEPTH - 1
        mod = [mods[i, :, k * D_MODEL:(k + 1) * D_MODEL].reshape(8, 1, D_MODEL) for k in range(6)]
        g_pre = norm_mix_pre[i].reshape(1, -1)

        u_l, qk_l, vs_l, q_l, k_l, v_l = _mix_in(
            xl, mod[0], mod[1], g_pre, w_in_pad, w_uq_pad, w_ukv_perm, mla_q_norm[i], mla_kv_norm[i],
            tabs, i, tm_p, lat_row(tm_p), True, SEQ)
        u_c, qk_c, vs_c, q_c, k_c, v_c = _mix_in(
            xc, mod[0], mod[1], g_pre, w_in_pad, w_uq_pad, w_ukv_perm, mla_q_norm[i], mla_kv_norm[i],
            tabs, i, tm_p, ctx_row, False, tm_p)

        yf, yb = _s5_scan(u_c, u_l, *s5_w, i, 0 if need_ctx else S5_CTX_CHUNKS)
        u3_l = u_l.reshape(BATCH, SEQ, S5_WIDTH)
        u3_c = u_c.reshape(BATCH, CTX_LEN, S5_WIDTH)

        swa_l = _swa_latent(qk_l, vs_l, qk_c, vs_c, swa_sink[i])

        mla_l = _mla_attn(q_l, [k_c, k_l], [v_c, v_l], SEQ, TQ_MLA)

        xl, fl = _out_proj(u3_l, yf, yb, 0, s5_d[i], glu_w_b, s5_glu_b[i], swa_l, mla_l, w_out_b, xl,
                           norm_mix_post[i], mod[2], norm_ffn_pre[i], mod[3], mod[4], i, tm_p, lat_row(tm_p))
        xl = _ffn(xl, fl, ffn_w1, ffn_w2, norm_ffn_post[i], mod[5], i, tm_f, lat_row(tm_f))

        if need_ctx:
            swa_c = _swa_context(qk_c, vs_c, swa_sink[i])
            mla_c = _mla_attn(q_c, [k_c], [v_c], CTX_LEN, TQ_MLA_CTX)
            xc, fc = _out_proj(u3_c, yf, yb, SEQ, s5_d[i], glu_w_b, s5_glu_b[i], swa_c, mla_c, w_out_b, xc,
                               norm_mix_post[i], mod[2], norm_ffn_pre[i], mod[3], mod[4], i, CTX_LEN, ctx_row)
            xc = _ffn(xc, fc, ffn_w1, ffn_w2, norm_ffn_post[i], mod[5], i, tm_f, ctx_row)

    return xl.reshape(BATCH, SEQ, D_MODEL)
```
